```python
import math
import jax, jax.numpy as jnp
from jax import lax
import numpy as np

D_MODEL = 2048
BATCH = 4
SEQ = 8192
DEPTH = 1

D_PLE = 256
EPS = 1e-6
POOL_WINDOWS = (2, 4, 8, 16)
N_POOL_GROUPS = 4
D_POOL = D_MODEL // 2
POOL_GW = D_POOL // N_POOL_GROUPS
DN_HEADS = 8
DN_DK = 128
DN_DV = 128
D_DN_QK = DN_HEADS * DN_DK
D_DN_V = DN_HEADS * DN_DV
D_QKV = 2 * D_DN_QK + D_DN_V
CONV_K = 4
CHUNK = 64
D_IN = D_POOL + D_QKV + D_DN_V + 2 * DN_HEADS + 2 * D_MODEL
N_GROUPS = 4
EXPERTS_PER_GROUP = 8
N_EXPERTS = N_GROUPS * EXPERTS_PER_GROUP
TOP_K = 2
D_EXPERT = 512
MOE_BLOCK = 128

kernel_name = "hybrid_pool_deltanet_hmoe_block"


def rms_norm(x, gain):
    xf = x.astype(jnp.float32)
    y = xf * lax.rsqrt(jnp.mean(xf * xf, axis=-1, keepdims=True) + EPS)
    return (y * gain.astype(jnp.float32)).astype(x.dtype)


def l2_normalize(x):
    return x * lax.rsqrt(jnp.sum(x * x, axis=-1, keepdims=True) + EPS)


def split_columns(proj):
    sizes = (D_POOL, D_QKV, D_DN_V, DN_HEADS, DN_HEADS, D_MODEL, D_MODEL)
    parts = []
    start = 0
    for size in sizes:
        parts.append(proj[..., start:start + size])
        start += size
    return parts


def causal_pool_mixer(u, w_grp, scale):
    bsz, s, _ = u.shape
    ug = u.astype(jnp.float32).reshape(bsz, s, N_POOL_GROUPS, POOL_GW)
    t1 = jnp.arange(1, s + 1, dtype=jnp.float32)
    outs = []
    for gi, w in enumerate(POOL_WINDOWS):
        xg = ug[:, :, gi]
        c = jnp.cumsum(xg, axis=1)
        cp = jnp.pad(c, ((0, 0), (w, 0), (0, 0)))
        win = cp[:, w:] - cp[:, :s]
        cnt = jnp.minimum(t1, float(w))
        outs.append(win / cnt[None, :, None] - xg)
    d = jnp.stack(outs, axis=2).astype(u.dtype)
    y = jnp.einsum('bsgc,gcd->bsgd', d, w_grp).reshape(bsz, s, D_POOL)
    return y * scale


def causal_short_conv(u, w):
    s = u.shape[1]
    up = jnp.pad(u, ((0, 0), (CONV_K - 1, 0), (0, 0)))
    y = up[:, 0:s] * w[0]
    for k in range(1, CONV_K):
        y = y + up[:, k:k + s] * w[k]
    return jax.nn.silu(y)


def gated_delta_rule(q, k, v, g, beta):
    bsz, s, h, dk = q.shape
    dv = v.shape[-1]
    nc = s // CHUNK
    f32 = jnp.float32
    q = l2_normalize(q.astype(f32)) * (dk ** -0.5)
    k = l2_normalize(k.astype(f32))
    v = v.astype(f32)

    def to_chunks(t):
        return t.reshape(bsz, nc, CHUNK, h, t.shape[-1]).transpose(1, 0, 3, 2, 4)

    qc, kc, vc = to_chunks(q), to_chunks(k), to_chunks(v)
    gc = to_chunks(g.astype(f32)[..., None])[..., 0]
    bc = to_chunks(beta.astype(f32)[..., None])[..., 0]
    gcum = jnp.cumsum(gc, axis=-1)
    incl = jnp.tril(jnp.ones((CHUNK, CHUNK), dtype=bool))
    strict = jnp.tril(jnp.ones((CHUNK, CHUNK), dtype=bool), -1)
    decay = jnp.exp(jnp.where(incl, gcum[..., :, None] - gcum[..., None, :], -jnp.inf))
    kb = kc * bc[..., None]
    lower = jnp.where(strict, jnp.einsum('nbhid,nbhjd->nbhij', kb, kc) * decay, 0.0)
    a_mat = lower + jnp.eye(CHUNK, dtype=f32)
    rhs = jnp.concatenate([vc * bc[..., None], kb * jnp.exp(gcum)[..., None]], axis=-1)
    sol = lax.linalg.triangular_solve(a_mat, rhs, left_side=True, lower=True, unit_diagonal=True)
    u_c, w_c = sol[..., :dv], sol[..., dv:]

    def step(state, inp):
        q_i, k_i, u_i, w_i, g_i, dec_i = inp
        attn = jnp.einsum('bhid,bhjd->bhij', q_i, k_i) * dec_i
        v_new = u_i - jnp.einsum('bhcd,bhde->bhce', w_i, state)
        o = (jnp.einsum('bhcd,bhde->bhce', q_i * jnp.exp(g_i)[..., None], state)
             + jnp.einsum('bhij,bhje->bhie', attn, v_new))
        g_last = g_i[..., -1]
        k_dec = k_i * jnp.exp(g_last[..., None] - g_i)[..., None]
        state = state * jnp.exp(g_last)[..., None, None] + jnp.einsum('bhcd,bhce->bhde', k_dec, v_new)
        return state, o

    state0 = jnp.zeros((bsz, h, dk, dv), dtype=f32)
    _, o = lax.scan(step, state0, (qc, kc, u_c, w_c, gcum, decay))
    return o.transpose(1, 0, 3, 2, 4).reshape(bsz, s, h, dv)


def hierarchical_moe(xt, w_rg, b_rg, w_re, b_re, w_gate, w_up, w_down):
    t = xt.shape[0]
    xf = xt.astype(jnp.float32)
    logits_grp = xf @ w_rg.astype(jnp.float32) + b_rg.astype(jnp.float32)
    p_grp = jax.nn.softmax(logits_grp, axis=-1)
    gi = jnp.argmax(logits_grp, axis=-1)
    p_sel = jnp.take_along_axis(p_grp, gi[:, None], axis=1)[:, 0]
    logits_exp = (xf @ w_re.astype(jnp.float32) + b_re.astype(jnp.float32)).reshape(t, N_GROUPS, EXPERTS_PER_GROUP)
    le = jnp.take_along_axis(logits_exp, gi[:, None, None], axis=1)[:, 0]
    top_vals, top_idx = lax.top_k(le, TOP_K)
    weights = p_sel[:, None] * jax.nn.softmax(top_vals, axis=-1)
    expert_ids = gi[:, None] * EXPERTS_PER_GROUP + top_idx

    n_assign = t * TOP_K
    flat_e = expert_ids.reshape(n_assign).astype(jnp.int32)
    flat_w = weights.reshape(n_assign)
    order = jnp.argsort(flat_e)
    sorted_e = flat_e[order]
    sorted_tok = (order // TOP_K).astype(jnp.int32)
    sorted_w = flat_w[order]
    counts = jnp.bincount(flat_e, length=N_EXPERTS)
    starts = jnp.cumsum(counts) - counts
    padded = ((counts + MOE_BLOCK - 1) // MOE_BLOCK) * MOE_BLOCK
    pends = jnp.cumsum(padded)
    pstarts = pends - padded
    pos = pstarts[sorted_e] + (jnp.arange(n_assign) - starts[sorted_e])
    cap = n_assign + N_EXPERTS * MOE_BLOCK
    n_blocks = cap // MOE_BLOCK
    tok_pad = jnp.zeros((cap,), jnp.int32).at[pos].set(sorted_tok)
    w_pad = jnp.zeros((cap,), xt.dtype).at[pos].set(sorted_w.astype(xt.dtype))
    blk_e = jnp.minimum(jnp.searchsorted(pends, jnp.arange(n_blocks) * MOE_BLOCK, side='right'), N_EXPERTS - 1)

    def blk_step(acc, inp):
        idx, wt, e = inp
        xb = xt[idx]
        hmid = jax.nn.silu(xb @ w_gate[e]) * (xb @ w_up[e])
        yb = hmid @ w_down[e]
        return acc.at[idx].add(yb * wt[:, None]), None

    acc, _ = lax.scan(blk_step, jnp.zeros_like(xt),
                      (tok_pad.reshape(n_blocks, MOE_BLOCK), w_pad.reshape(n_blocks, MOE_BLOCK), blk_e))
    return acc


def setup_inputs(seed: int = 0) -> dict:
    key = jax.random.key(seed)
    ks = jax.random.split(key, 32)
    nrm = jax.random.normal
    f32 = jnp.float32
    x = nrm(ks[0], (BATCH, SEQ, D_MODEL), f32)
    p = nrm(ks[1], (DEPTH, BATCH, SEQ, D_PLE), f32)
    norm_mix = 1.0 + 0.02 * nrm(ks[2], (DEPTH, D_MODEL), f32)
    w_in = nrm(ks[3], (DEPTH, D_MODEL, D_IN), f32) * D_MODEL ** -0.5
    pool_w = nrm(ks[4], (DEPTH, N_POOL_GROUPS, POOL_GW, POOL_GW), f32) * POOL_GW ** -0.5
    pool_scale = 1.0 + 0.1 * nrm(ks[5], (DEPTH, D_POOL), f32)
    conv_w = nrm(ks[6], (DEPTH, CONV_K, D_QKV), f32) * CONV_K ** -0.5
    a_log = jnp.log(jax.random.uniform(ks[7], (DEPTH, DN_HEADS), f32, minval=1.0, maxval=16.0))
    dt = jnp.exp(jax.random.uniform(ks[8], (DEPTH, DN_HEADS), f32, minval=math.log(1e-3), maxval=math.log(0.1)))
    dt_bias = dt + jnp.log(-jnp.expm1(-dt))
    dn_norm = 1.0 + 0.02 * nrm(ks[9], (DEPTH, DN_DV), f32)
    w_up_pool = nrm(ks[10], (DEPTH, D_POOL, D_MODEL), f32) * D_POOL ** -0.5
    w_up_dn = nrm(ks[11], (DEPTH, D_DN_V, D_MODEL), f32) * D_DN_V ** -0.5
    w_out = nrm(ks[12], (DEPTH, D_MODEL, D_MODEL), f32) * D_MODEL ** -0.5
    norm_moe = 1.0 + 0.02 * nrm(ks[13], (DEPTH, D_MODEL), f32)
    w_router_group = nrm(ks[14], (DEPTH, D_MODEL, N_GROUPS), f32) * D_MODEL ** -0.5
    b_router_group = 0.01 * nrm(ks[15], (DEPTH, N_GROUPS), f32)
    w_router_expert = nrm(ks[16], (DEPTH, D_MODEL, N_EXPERTS), f32) * D_MODEL ** -0.5
    b_router_expert = 0.01 * nrm(ks[17], (DEPTH, N_EXPERTS), f32)
    w_gate = nrm(ks[18], (DEPTH, N_EXPERTS, D_MODEL, D_EXPERT), f32) * D_MODEL ** -0.5
    w_up = nrm(ks[19], (DEPTH, N_EXPERTS, D_MODEL, D_EXPERT), f32) * D_MODEL ** -0.5
    w_down = nrm(ks[20], (DEPTH, N_EXPERTS, D_EXPERT, D_MODEL), f32) * D_EXPERT ** -0.5
    norm_ple = 1.0 + 0.02 * nrm(ks[21], (DEPTH, D_MODEL), f32)
    w_ple_gate = nrm(ks[22], (DEPTH, D_MODEL, D_MODEL), f32) * D_MODEL ** -0.5
    w_ple_proj = nrm(ks[23], (DEPTH, D_PLE, D_MODEL), f32) * D_PLE ** -0.5
    norm_final = 1.0 + 0.02 * nrm(ks[24], (D_MODEL,), f32)
    return {"x": x, "p": p, "norm_mix": norm_mix, "w_in": w_in, "pool_w": pool_w,
            "pool_scale": pool_scale, "conv_w": conv_w, "a_log": a_log, "dt_bias": dt_bias,
            "dn_norm": dn_norm, "w_up_pool": w_up_pool, "w_up_dn": w_up_dn, "w_out": w_out,
            "norm_moe": norm_moe, "w_router_group": w_router_group, "b_router_group": b_router_group,
            "w_router_expert": w_router_expert, "b_router_expert": b_router_expert,
            "w_gate": w_gate, "w_up": w_up, "w_down": w_down, "norm_ple": norm_ple,
            "w_ple_gate": w_ple_gate, "w_ple_proj": w_ple_proj, "norm_final": norm_final}


def reference(x, p, norm_mix, w_in, pool_w, pool_scale, conv_w, a_log, dt_bias, dn_norm,
              w_up_pool, w_up_dn, w_out, norm_moe, w_router_group, b_router_group,
              w_router_expert, b_router_expert, w_gate, w_up, w_down, norm_ple,
              w_ple_gate, w_ple_proj, norm_final):
    bsz, s, d = x.shape
    h = x
    for i in range(DEPTH):
        n1 = rms_norm(h, norm_mix[i])
        proj = n1 @ w_in[i]
        u_pool, qkv, z, b_lin, a_lin, gate_pool, gate_dn = split_columns(proj)
        y_pool = causal_pool_mixer(u_pool, pool_w[i], pool_scale[i])
        qkv = causal_short_conv(qkv, conv_w[i])
        q = qkv[..., :D_DN_QK].reshape(bsz, s, DN_HEADS, DN_DK)
        k = qkv[..., D_DN_QK:2 * D_DN_QK].reshape(bsz, s, DN_HEADS, DN_DK)
        v = qkv[..., 2 * D_DN_QK:].reshape(bsz, s, DN_HEADS, DN_DV)
        beta = jax.nn.sigmoid(b_lin.astype(jnp.float32))
        g = -jnp.exp(a_log[i].astype(jnp.float32)) * jax.nn.softplus(a_lin.astype(jnp.float32) + dt_bias[i].astype(jnp.float32))
        o = gated_delta_rule(q, k, v, g, beta)
        o = rms_norm(o, dn_norm[i]) * jax.nn.silu(z.astype(jnp.float32).reshape(bsz, s, DN_HEADS, DN_DV))
        y_dn = o.reshape(bsz, s, D_DN_V).astype(x.dtype)
        merged = (jax.nn.sigmoid(gate_pool) * (y_pool @ w_up_pool[i])
                  + jax.nn.sigmoid(gate_dn) * (y_dn @ w_up_dn[i]))
        h = h + merged @ w_out[i]
        n2 = rms_norm(h, norm_moe[i])
        y_moe = hierarchical_moe(n2.reshape(bsz * s, d), w_router_group[i], b_router_group[i],
                                 w_router_expert[i], b_router_expert[i], w_gate[i], w_up[i], w_down[i])
        h = h + y_moe.reshape(bsz, s, d)
        n3 = rms_norm(h, norm_ple[i])
        h = h + jax.nn.sigmoid(n3 @ w_ple_gate[i]) * (p[i] @ w_ple_proj[i])
    return rms_norm(h, norm_final)
```

```python
import functools

import jax
import jax.numpy as jnp
from jax import lax
from jax.experimental import pallas as pl
from jax.experimental.pallas import tpu as pltpu

F32 = jnp.float32
BF16 = jnp.bfloat16

EPS = 1e-6
POOL_WINDOWS = (2, 4, 8, 16)
POOL_HALO = 16
CONV_K = 4
CONV_HALO = 8
DN_HEADS = 8
DN_DK = 128
CHUNK = 64
N_GROUPS = 4
EXPERTS_PER_GROUP = 8
N_EXPERTS = N_GROUPS * EXPERTS_PER_GROUP
LANES = 128
VMEM_LIMIT = 56 * 1024 * 1024

COL_GATE_POOL = 0
COL_GATE_DN = 2048
COL_POOL = 4096
COL_Q = 5120
COL_K = 6144
COL_V = 7168
COL_Z = 8192
N_MAIN = 9216


def _cparams(sem):
    return pltpu.CompilerParams(dimension_semantics=sem, vmem_limit_bytes=VMEM_LIMIT)


def _resident(shape):
    nd = len(shape)
    return pl.BlockSpec(shape, lambda *_: (0,) * nd, pipeline_mode=pl.Buffered(1))


def _inproj_kernel(x_ref, gain_ref, w_ref, wba_ref, out_ref, ba_ref, n1_ref, *, sub):
    tm = x_ref.shape[0]

    @pl.when(pl.program_id(1) == 0)
    def _():
        def body(r, _):
            rows = pl.ds(pl.multiple_of(r * sub, sub), sub)
            x = x_ref[rows, :]
            y = x * lax.rsqrt(jnp.mean(x * x, axis=-1, keepdims=True) + EPS) * gain_ref[...]
            n1_ref[rows, :] = y.astype(BF16)
            return 0
        lax.fori_loop(0, tm // sub, body, 0)
        ba_ref[...] = jnp.dot(n1_ref[...], wba_ref[...], preferred_element_type=F32)

    out_ref[...] = jnp.dot(n1_ref[...], w_ref[...], preferred_element_type=F32)


def _inproj(x2d, gain, w_main, w_ba, *, tm, tn):
    t, d = x2d.shape
    n = w_main.shape[1]
    return pl.pallas_call(
        functools.partial(_inproj_kernel, sub=min(tm, 256)),
        grid=(t // tm, n // tn),
        in_specs=[
            pl.BlockSpec((tm, d), lambda i, j: (i, 0)),
            _resident((1, d)),
            pl.BlockSpec((d, tn), lambda i, j: (0, j)),
            _resident((d, LANES)),
        ],
        out_specs=[
            pl.BlockSpec((tm, tn), lambda i, j: (i, j)),
            pl.BlockSpec((tm, LANES), lambda i, j: (i, 0)),
        ],
        out_shape=[jax.ShapeDtypeStruct((t, n), F32), jax.ShapeDtypeStruct((t, LANES), F32)],
        scratch_shapes=[pltpu.VMEM((tm, d), BF16)],
        compiler_params=_cparams(("parallel", "arbitrary")),
        name="inproj",
    )(x2d, gain, w_main, w_ba)


def _pool_kernel(u_ref, gp_ref, pw_ref, scale_ref, wup_ref, out_ref, ext_ref):
    ts = u_ref.shape[0]
    gw = pw_ref.shape[1]
    s = pl.program_id(1)

    @pl.when(s == 0)
    def _():
        ext_ref[0:POOL_HALO, :] = jnp.zeros((POOL_HALO, ext_ref.shape[1]), F32)

    @pl.when(s > 0)
    def _():
        ext_ref[0:POOL_HALO, :] = ext_ref[ts:ts + POOL_HALO, :]

    ext_ref[POOL_HALO:POOL_HALO + ts, :] = u_ref[...]

    t1 = (s * ts + 1 + lax.broadcasted_iota(jnp.int32, (ts, 1), 0)).astype(F32)
    acc = jnp.zeros(out_ref.shape, F32)
    for g, w in enumerate(POOL_WINDOWS):
        cols = slice(g * gw, (g + 1) * gw)
        cur = ext_ref[POOL_HALO:POOL_HALO + ts, cols]
        win = cur
        for j in range(1, w):
            win = win + ext_ref[POOL_HALO - j:POOL_HALO - j + ts, cols]
        d = win / jnp.minimum(t1, float(w)) - cur
        y = jnp.dot(d.astype(BF16), pw_ref[g], preferred_element_type=F32) * scale_ref[:, cols]
        acc = acc + jnp.dot(y.astype(BF16), wup_ref[cols, :], preferred_element_type=F32)
    out_ref[...] = jax.nn.sigmoid(gp_ref[...]) * acc


def _pool(proj, pool_w, pool_scale, w_up_pool, *, bsz, seq, ts):
    t = proj.shape[0]
    ns = seq // ts
    d_pool = w_up_pool.shape[0]
    d = w_up_pool.shape[1]
    return pl.pallas_call(
        _pool_kernel,
        grid=(bsz, ns),
        in_specs=[
            pl.BlockSpec((ts, d_pool), lambda b, s: (b * ns + s, COL_POOL // d_pool)),
            pl.BlockSpec((ts, d), lambda b, s: (b * ns + s, COL_GATE_POOL // d)),
            _resident(pool_w.shape),
            _resident((1, d_pool)),
            _resident(w_up_pool.shape),
        ],
        out_specs=pl.BlockSpec((ts, d), lambda b, s: (b * ns + s, 0)),
        out_shape=jax.ShapeDtypeStruct((t, d), F32),
        scratch_shapes=[pltpu.VMEM((POOL_HALO + ts, d_pool), F32)],
        compiler_params=_cparams(("parallel", "arbitrary")),
        name="pool",
    )(proj, proj, pool_w, pool_scale, w_up_pool)


def _prep_kernel(q_ref, k_ref, v_ref, ba_ref, cw_ref, alog_ref, dtb_ref,
                 qn_ref, kn_ref, vv_ref, bx_ref, gx_ref, gt_ref, ext_ref):
    ts = q_ref.shape[0]
    dq = q_ref.shape[1]
    s = pl.program_id(1)

    @pl.when(s == 0)
    def _():
        ext_ref[0:CONV_HALO, :] = jnp.zeros((CONV_HALO, ext_ref.shape[1]), F32)

    @pl.when(s > 0)
    def _():
        ext_ref[0:CONV_HALO, :] = ext_ref[ts:ts + CONV_HALO, :]

    ext_ref[CONV_HALO:CONV_HALO + ts, 0:dq] = q_ref[...]
    ext_ref[CONV_HALO:CONV_HALO + ts, dq:2 * dq] = k_ref[...]
    ext_ref[CONV_HALO:CONV_HALO + ts, 2 * dq:3 * dq] = v_ref[...]

    def conv_silu(c0, width):
        cols = slice(c0, c0 + width)
        first = CONV_HALO - (CONV_K - 1)
        y = ext_ref[first:first + ts, cols] * cw_ref[0:1, cols]
        for tap in range(1, CONV_K):
            y = y + ext_ref[first + tap:first + tap + ts, cols] * cw_ref[tap:tap + 1, cols]
        return y * jax.nn.sigmoid(y)

    for h in range(DN_HEADS):
        hc = slice(h * DN_DK, (h + 1) * DN_DK)
        qh = conv_silu(h * DN_DK, DN_DK)
        qn_ref[:, hc] = qh * (lax.rsqrt(jnp.sum(qh * qh, axis=-1, keepdims=True) + EPS) * (DN_DK ** -0.5))
        kh = conv_silu(dq + h * DN_DK, DN_DK)
        kn_ref[:, hc] = kh * lax.rsqrt(jnp.sum(kh * kh, axis=-1, keepdims=True) + EPS)
        vv_ref[:, hc] = conv_silu(2 * dq + h * DN_DK, DN_DK)

    ba = ba_ref[...]
    beta = jax.nn.sigmoid(ba)
    xs = ba + dtb_ref[...]
    softplus = jnp.maximum(xs, 0.0) + jnp.log1p(jnp.exp(-jnp.abs(xs)))
    g = -jnp.exp(alog_ref[...]) * softplus
    ri = lax.broadcasted_iota(jnp.int32, (ts, ts), 0)
    ci = lax.broadcasted_iota(jnp.int32, (ts, ts), 1)
    tri = jnp.where((ri // CHUNK == ci // CHUNK) & (ci <= ri), 1.0, 0.0).astype(F32)
    gcum = jnp.dot(tri, g, preferred_element_type=F32, precision=lax.Precision.HIGHEST)
    gt_ref[0] = gcum.T[DN_HEADS:2 * DN_HEADS, :]
    for h in range(DN_HEADS):
        hc = slice(h * DN_DK, (h + 1) * DN_DK)
        bx_ref[:, hc] = jnp.broadcast_to(beta[:, h:h + 1], (ts, DN_DK))
        gx_ref[:, hc] = jnp.broadcast_to(gcum[:, DN_HEADS + h:DN_HEADS + h + 1], (ts, DN_DK))


def _prep(proj, ba, conv_w, alog_pad, dtb_pad, *, bsz, seq, ts):
    t = proj.shape[0]
    ns = seq // ts
    dq = DN_HEADS * DN_DK
    row = lambda b, s: b * ns + s
    blk = lambda col: pl.BlockSpec((ts, dq), lambda b, s: (row(b, s), col // dq))
    out_blk = pl.BlockSpec((ts, dq), lambda b, s: (row(b, s), 0))
    return pl.pallas_call(
        _prep_kernel,
        grid=(bsz, ns),
        in_specs=[
            blk(COL_Q), blk(COL_K), blk(COL_V),
            pl.BlockSpec((ts, LANES), lambda b, s: (row(b, s), 0)),
            _resident(conv_w.shape),
            _resident((1, LANES)),
            _resident((1, LANES)),
        ],
        out_specs=[out_blk, out_blk, out_blk, out_blk, out_blk,
                   pl.BlockSpec((1, DN_HEADS, ts), lambda b, s: (row(b, s), 0, 0))],
        out_shape=[jax.ShapeDtypeStruct((t, dq), F32)] * 5
                  + [jax.ShapeDtypeStruct((t // ts, DN_HEADS, ts), F32)],
        scratch_shapes=[pltpu.VMEM((CONV_HALO + ts, 3 * dq), F32)],
        compiler_params=_cparams(("parallel", "arbitrary")),
        name="prep",
    )(proj, proj, proj, ba, conv_w, alog_pad, dtb_pad)


def _dot_nt(a, b):
    return lax.dot_general(a, b, (((1,), (1,)), ((), ())), preferred_element_type=F32)


def _dot_tn(a, b):
    return lax.dot_general(a, b, (((0,), (0,)), ((), ())), preferred_element_type=F32)


def _delta_kernel(q_ref, k_ref, v_ref, z_ref, bx_ref, gx_ref, gt_ref, dnw_ref, out_ref, state_ref):
    ts = q_ref.shape[0]

    @pl.when(pl.program_id(2) == 0)
    def _():
        state_ref[...] = jnp.zeros(state_ref.shape, F32)

    ii = lax.broadcasted_iota(jnp.int32, (CHUNK, CHUNK), 0)
    jj = lax.broadcasted_iota(jnp.int32, (CHUNK, CHUNK), 1)
    incl = ii >= jj
    strict = ii > jj

    for c in range(ts // CHUNK):
        rows = slice(c * CHUNK, (c + 1) * CHUNK)
        q = q_ref[rows, :]
        k = k_ref[rows, :]
        v = v_ref[rows, :]
        beta = bx_ref[rows, :]
        gc = gx_ref[rows, :]
        gr = gt_ref[0, :, rows]
        decay = jnp.where(incl, jnp.exp(gc[:, :CHUNK] - gr), 0.0)
        kb = k * beta
        k16 = k.astype(BF16)
        lower = jnp.where(strict, _dot_nt(kb.astype(BF16), k16) * decay, 0.0)
        attn = _dot_nt(q.astype(BF16), k16) * decay
        egc = jnp.exp(gc)
        sol = jnp.concatenate([v * beta, kb * egc], axis=1)
        powers = [lower]
        while (1 << len(powers)) < CHUNK:
            p16 = powers[-1].astype(BF16)
            powers.append(jnp.dot(p16, p16, preferred_element_type=F32))
        for pw in reversed(powers[1:]):
            sol = sol + jnp.dot(pw.astype(BF16), sol.astype(BF16), preferred_element_type=F32)
        sol = sol - jnp.dot(powers[0].astype(BF16), sol.astype(BF16), preferred_element_type=F32)
        u = sol[:, :DN_DK]
        w = sol[:, DN_DK:]

        state = state_ref[...]
        s16 = state.astype(BF16)
        v_new = u - jnp.dot(w.astype(BF16), s16, preferred_element_type=F32)
        vn16 = v_new.astype(BF16)
        o = (jnp.dot((q * egc).astype(BF16), s16, preferred_element_type=F32)
             + jnp.dot(attn.astype(BF16), vn16, preferred_element_type=F32))
        g_last = gc[CHUNK - 1:CHUNK, :]
        k_dec = k * jnp.exp(g_last - gc)
        state_ref[...] = state * jnp.exp(g_last) + _dot_tn(k_dec.astype(BF16), vn16)

        on = o * lax.rsqrt(jnp.mean(o * o, axis=-1, keepdims=True) + EPS) * dnw_ref[...]
        z = z_ref[rows, :]
        out_ref[rows, :] = (on * (z * jax.nn.sigmoid(z))).astype(out_ref.dtype)


def _delta(qn, kn, vv, proj, bx, gx, gt, dn_norm, *, bsz, seq, ts):
    t = qn.shape[0]
    ns = seq // ts
    row = lambda b, s: b * ns + s
    hblk = pl.BlockSpec((ts, DN_DK), lambda b, h, s: (row(b, s), h))
    return pl.pallas_call(
        _delta_kernel,
        grid=(bsz, DN_HEADS, ns),
        in_specs=[
            hblk, hblk, hblk,
            pl.BlockSpec((ts, DN_DK), lambda b, h, s: (row(b, s), COL_Z // DN_DK + h)),
            hblk, hblk,
            pl.BlockSpec((1, 1, ts), lambda b, h, s: (row(b, s) * DN_HEADS + h, 0, 0)),
            _resident((1, DN_DK)),
        ],
        out_specs=hblk,
        out_shape=jax.ShapeDtypeStruct((t, DN_HEADS * DN_DK), BF16),
        scratch_shapes=[pltpu.VMEM((DN_DK, DN_DK), F32)],
        compiler_params=_cparams(("parallel", "parallel", "arbitrary")),
        name="delta",
    )(qn, kn, vv, proj, bx, gx, gt.reshape(-1, 1, ts), dn_norm)


ROUTE_E1, ROUTE_E2, ROUTE_W1, ROUTE_W2, ROUTE_R1, ROUTE_R2 = range(6)
ROUTER_LANE0 = N_GROUPS
NEG_BIG = -1e30


def _merge_kernel(ydn_ref, gd_ref, mp_ref, x_ref, wupdn_ref, wout_ref, gain_ref, wr_ref, br_ref,
                  h1_ref, n2_ref, route_ref, cnt_ref, carry_ref):
    tm = x_ref.shape[0]

    @pl.when(pl.program_id(0) == 0)
    def _():
        carry_ref[...] = jnp.zeros(carry_ref.shape, F32)

    md = jax.nn.sigmoid(gd_ref[...]) * jnp.dot(ydn_ref[...], wupdn_ref[...], preferred_element_type=F32)
    merged = mp_ref[...] + md
    h1 = x_ref[...] + jnp.dot(merged.astype(BF16), wout_ref[...], preferred_element_type=F32)
    h1_ref[...] = h1
    n2 = h1 * lax.rsqrt(jnp.mean(h1 * h1, axis=-1, keepdims=True) + EPS) * gain_ref[...]
    n2_ref[...] = n2

    hi = n2.astype(BF16)
    lo = (n2 - hi.astype(F32)).astype(BF16)
    r_hi = jnp.dot(hi, wr_ref[...], preferred_element_type=F32)
    r_lo = jnp.dot(lo, wr_ref[:, :LANES], preferred_element_type=F32)
    logits = r_hi[:, :LANES] + r_hi[:, LANES:] + r_lo + br_ref[...]

    lane = lax.broadcasted_iota(jnp.int32, (tm, LANES), 1)
    lane_f = lane.astype(F32)
    first_max = lambda v, m: jnp.min(jnp.where(v == m, lane_f, float(LANES)), axis=-1, keepdims=True)

    is_grp = lane < N_GROUPS
    lg = jnp.where(is_grp, logits, NEG_BIG)
    gmax = jnp.max(lg, axis=-1, keepdims=True)
    gi = first_max(lg, gmax)
    p_sel = 1.0 / jnp.sum(jnp.where(is_grp, jnp.exp(lg - gmax), 0.0), axis=-1, keepdims=True)

    e_lane = lane - ROUTER_LANE0
    in_grp = (e_lane >= 0) & (e_lane < N_EXPERTS) & ((e_lane // EXPERTS_PER_GROUP).astype(F32) == gi)
    le = jnp.where(in_grp, logits, NEG_BIG)
    v1 = jnp.max(le, axis=-1, keepdims=True)
    i1 = first_max(le, v1)
    le2 = jnp.where(lane_f == i1, NEG_BIG, le)
    v2 = jnp.max(le2, axis=-1, keepdims=True)
    i2 = first_max(le2, v2)
    s = jnp.exp(v2 - v1)
    w1 = p_sel / (1.0 + s)
    w2 = p_sel * s / (1.0 + s)
    e1 = i1 - float(ROUTER_LANE0)
    e2 = i2 - float(ROUTER_LANE0)

    oh1 = jnp.where(lane_f == e1, 1.0, 0.0)
    oh2 = jnp.where(lane_f == e2, 1.0, 0.0)
    both = oh1 + oh2
    ri = lax.broadcasted_iota(jnp.int32, (tm, tm), 0)
    ci = lax.broadcasted_iota(jnp.int32, (tm, tm), 1)
    tri = jnp.where(ri > ci, 1.0, 0.0).astype(BF16)
    prior = jnp.dot(tri, both.astype(BF16), preferred_element_type=F32) + carry_ref[...]
    r1 = jnp.sum(prior * oh1, axis=-1, keepdims=True)
    r2 = jnp.sum(prior * oh2, axis=-1, keepdims=True)
    carry = carry_ref[...] + jnp.sum(both, axis=0, keepdims=True)
    carry_ref[...] = carry
    cnt_ref[...] = carry

    route = jnp.zeros((tm, LANES), F32)
    for ln, val in ((ROUTE_E1, e1), (ROUTE_E2, e2), (ROUTE_W1, w1), (ROUTE_W2, w2),
                    (ROUTE_R1, r1), (ROUTE_R2, r2)):
        route = jnp.where(lane == ln, val, route)
    route_ref[...] = route


def _merge(y_dn, proj, mp, x2d, w_up_dn, w_out, gain, w_router, b_router, *, tm):
    t, d = x2d.shape
    dv = y_dn.shape[1]
    rows = lambda blk: pl.BlockSpec(blk, lambda i: (i, 0))
    return pl.pallas_call(
        _merge_kernel,
        grid=(t // tm,),
        in_specs=[
            rows((tm, dv)),
            pl.BlockSpec((tm, d), lambda i: (i, COL_GATE_DN // d)),
            rows((tm, d)), rows((tm, d)),
            _resident(w_up_dn.shape), _resident(w_out.shape), _resident((1, d)),
            _resident(w_router.shape), _resident((1, LANES)),
        ],
        out_specs=[rows((tm, d)), rows((tm, d)), rows((tm, LANES)),
                   pl.BlockSpec((1, LANES), lambda i: (0, 0))],
        out_shape=[jax.ShapeDtypeStruct((t, d), F32), jax.ShapeDtypeStruct((t, d), F32),
                   jax.ShapeDtypeStruct((t, LANES), F32), jax.ShapeDtypeStruct((1, LANES), F32)],
        scratch_shapes=[pltpu.VMEM((1, LANES), F32)],
        compiler_params=_cparams(("arbitrary",)),
        name="merge",
    )(y_dn, proj, mp, x2d, w_up_dn, w_out, gain, w_router, b_router)


def _dispatch_kernel(pos_ref, n2_ref, init_ref, xs_ref, sem, *, tb):
    del init_ref
    base = pl.program_id(0) * tb

    def row_copy(tok, k):
        dst = pos_ref[2 * tok + k]
        return pltpu.make_async_copy(n2_ref.at[pl.ds(tok, 1)], xs_ref.at[pl.ds(dst, 1)], sem)

    def body(j, _):
        row_copy(base + j, 0).start()
        row_copy(base + j, 1).start()
        return 0
    lax.fori_loop(0, tb, body, 0)
    pltpu.make_async_copy(n2_ref.at[pl.ds(0, 2 * tb)], xs_ref.at[pl.ds(0, 2 * tb)], sem).wait()


def _dispatch(pos, n2, cap, *, tb):
    t, d = n2.shape
    return pl.pallas_call(
        functools.partial(_dispatch_kernel, tb=tb),
        grid_spec=pltpu.PrefetchScalarGridSpec(
            num_scalar_prefetch=1,
            grid=(t // tb,),
            in_specs=[pl.BlockSpec(memory_space=pl.ANY), pl.BlockSpec(memory_space=pl.ANY)],
            out_specs=pl.BlockSpec(memory_space=pl.ANY),
            scratch_shapes=[pltpu.SemaphoreType.DMA],
        ),
        out_shape=jax.ShapeDtypeStruct((cap, d), F32),
        input_output_aliases={2: 0},
        compiler_params=_cparams(("arbitrary",)),
        name="dispatch",
    )(pos, n2, jnp.zeros((cap, d), F32))


def _experts_kernel(be_ref, nu_ref, x_ref, wg_ref, wu_ref, wd_ref, y_ref):
    del be_ref

    @pl.when(pl.program_id(0) < nu_ref[0])
    def _():
        xb = x_ref[...].astype(BF16)
        g = jnp.dot(xb, wg_ref[0], preferred_element_type=F32)
        u = jnp.dot(xb, wu_ref[0], preferred_element_type=F32)
        hmid = (g * jax.nn.sigmoid(g)) * u
        y_ref[...] = jnp.dot(hmid.astype(BF16), wd_ref[0], preferred_element_type=F32)

    @pl.when(pl.program_id(0) >= nu_ref[0])
    def _():
        y_ref[...] = jnp.zeros(y_ref.shape, F32)


def _experts(blk_e, n_used, xs, w_gate, w_up, w_down, *, bm):
    cap, d = xs.shape
    de = w_gate.shape[2]
    live = lambda i, be, nu: (jnp.minimum(i, nu[0] - 1), 0)
    return pl.pallas_call(
        _experts_kernel,
        grid_spec=pltpu.PrefetchScalarGridSpec(
            num_scalar_prefetch=2,
            grid=(cap // bm,),
            in_specs=[
                pl.BlockSpec((bm, d), live),
                pl.BlockSpec((1, d, de), lambda i, be, nu: (be[i], 0, 0)),
                pl.BlockSpec((1, d, de), lambda i, be, nu: (be[i], 0, 0)),
                pl.BlockSpec((1, de, d), lambda i, be, nu: (be[i], 0, 0)),
            ],
            out_specs=pl.BlockSpec((bm, d), lambda i, be, nu: (i, 0)),
        ),
        out_shape=jax.ShapeDtypeStruct((cap, d), F32),
        compiler_params=_cparams(("arbitrary",)),
        name="experts",
    )(blk_e, n_used, xs, w_gate, w_up, w_down)


def _combine_kernel(pos_ref, h1_ref, route_ref, p_ref, ys_ref, wpg_ref, wpp_ref, gple_ref, gfin_ref,
                    out_ref, ybuf, sem, *, final_norm):
    tm = h1_ref.shape[0]
    i = pl.program_id(0)
    n = pl.num_programs(0)
    slot = i % 2

    def issue(blk, sl):
        def body(j, _):
            tok = blk * tm + j
            for k in range(2):
                pltpu.make_async_copy(ys_ref.at[pl.ds(pos_ref[2 * tok + k], 1)],
                                      ybuf.at[sl, pl.ds(k * tm + j, 1)], sem.at[sl]).start()
            return 0
        lax.fori_loop(0, tm, body, 0)

    @pl.when(i == 0)
    def _():
        issue(0, 0)

    @pl.when(i + 1 < n)
    def _():
        issue(i + 1, 1 - slot)

    pltpu.make_async_copy(ys_ref.at[pl.ds(0, 2 * tm)], ybuf.at[slot], sem.at[slot]).wait()

    route = route_ref[...]
    y_moe = (ybuf[slot, 0:tm, :] * route[:, ROUTE_W1:ROUTE_W1 + 1]
             + ybuf[slot, tm:2 * tm, :] * route[:, ROUTE_W2:ROUTE_W2 + 1])
    h2 = h1_ref[...] + y_moe
    n3 = h2 * lax.rsqrt(jnp.mean(h2 * h2, axis=-1, keepdims=True) + EPS) * gple_ref[...]
    gate = jax.nn.sigmoid(jnp.dot(n3.astype(BF16), wpg_ref[...], preferred_element_type=F32))
    h3 = h2 + gate * jnp.dot(p_ref[...].astype(BF16), wpp_ref[...], preferred_element_type=F32)
    if final_norm:
        h3 = h3 * lax.rsqrt(jnp.mean(h3 * h3, axis=-1, keepdims=True) + EPS) * gfin_ref[...]
    out_ref[...] = h3


def _combine(pos, h1, route, p2d, ys, w_ple_gate, w_ple_proj, g_ple, g_fin, *, tm, final_norm):
    t, d = h1.shape
    rows = lambda w: pl.BlockSpec((tm, w), lambda i, pos: (i, 0))
    res = lambda shape: pl.BlockSpec(shape, lambda i, pos: (0,) * len(shape), pipeline_mode=pl.Buffered(1))
    return pl.pallas_call(
        functools.partial(_combine_kernel, final_norm=final_norm),
        grid_spec=pltpu.PrefetchScalarGridSpec(
            num_scalar_prefetch=1,
            grid=(t // tm,),
            in_specs=[rows(d), rows(LANES), rows(p2d.shape[1]), pl.BlockSpec(memory_space=pl.ANY),
                      res(w_ple_gate.shape), res(w_ple_proj.shape), res((1, d)), res((1, d))],
            out_specs=rows(d),
            scratch_shapes=[pltpu.VMEM((2, 2 * tm, d), F32), pltpu.SemaphoreType.DMA((2,))],
        ),
        out_shape=jax.ShapeDtypeStruct((t, d), F32),
        compiler_params=_cparams(("arbitrary",)),
        name="combine",
    )(pos, h1, route, p2d, ys, w_ple_gate, w_ple_proj, g_ple, g_fin)


def _pick(n, pref):
    b = min(n, pref)
    while n % b:
        b -= 8
    return b


def _stage_mixers(x, prm, i):
    bsz, seq, d = x.shape
    t = bsz * seq
    x2d = x.reshape(t, d)
    w_in = prm["w_in"][i]
    d_pool = prm["w_up_pool"].shape[1]
    dq = DN_HEADS * DN_DK
    o_qkv = d_pool
    o_z = o_qkv + 3 * dq
    o_b = o_z + dq
    o_gp = o_b + 2 * DN_HEADS
    o_gd = o_gp + d
    w_main = jnp.concatenate(
        [w_in[:, o_gp:o_gp + d], w_in[:, o_gd:o_gd + d], w_in[:, :d_pool],
         w_in[:, o_qkv:o_qkv + 3 * dq], w_in[:, o_z:o_z + dq]], axis=1).astype(BF16)
    assert w_main.shape[1] == N_MAIN
    w_ba = jnp.pad(w_in[:, o_b:o_b + 2 * DN_HEADS], ((0, 0), (0, LANES - 2 * DN_HEADS))).astype(BF16)
    proj, ba = _inproj(x2d, prm["norm_mix"][i][None, :], w_main, w_ba,
                       tm=_pick(t, 1024), tn=1024)

    ts = _pick(seq, 512)
    mp = _pool(proj, prm["pool_w"][i].astype(BF16), prm["pool_scale"][i][None, :],
               prm["w_up_pool"][i].astype(BF16), bsz=bsz, seq=seq, ts=ts)

    lane_pad = lambda a: jnp.pad(a, (DN_HEADS, LANES - 2 * DN_HEADS))[None, :]
    qn, kn, vv, bx, gx, gt = _prep(proj, ba, prm["conv_w"][i], lane_pad(prm["a_log"][i]),
                                   lane_pad(prm["dt_bias"][i]), bsz=bsz, seq=seq, ts=ts)
    y_dn = _delta(qn, kn, vv, proj, bx, gx, gt, prm["dn_norm"][i][None, :], bsz=bsz, seq=seq, ts=ts)
    return dict(proj=proj, ba=ba, mp=mp, qn=qn, kn=kn, vv=vv, bx=bx, gx=gx, gt=gt, y_dn=y_dn)


MOE_BM = 256


def _router_weights(w_rg, b_rg, w_re, b_re):
    d = w_rg.shape[0]
    w = jnp.concatenate([w_rg, w_re, jnp.zeros((d, LANES - N_GROUPS - N_EXPERTS), F32)], axis=1)
    w_hi = w.astype(BF16)
    w_lo = (w - w_hi.astype(F32)).astype(BF16)
    b = jnp.concatenate([b_rg, b_re, jnp.zeros((LANES - N_GROUPS - N_EXPERTS,), F32)])[None, :]
    return jnp.concatenate([w_hi, w_lo], axis=1), b


def _stage_moe_plan(route, cnt, t):
    counts = cnt[0, :N_EXPERTS].astype(jnp.int32)
    padded = ((counts + MOE_BM - 1) // MOE_BM) * MOE_BM
    pends = jnp.cumsum(padded)
    pstarts = pends - padded
    eid = route[:, ROUTE_E1:ROUTE_E2 + 1].astype(jnp.int32)
    rank = route[:, ROUTE_R1:ROUTE_R2 + 1].astype(jnp.int32)
    pos = (pstarts[eid] + rank).reshape(2 * t)
    cap = 2 * t + N_EXPERTS * MOE_BM
    n_blocks = cap // MOE_BM
    n_used = (pends[-1] // MOE_BM).astype(jnp.int32)
    blk = jnp.minimum(jnp.arange(n_blocks, dtype=jnp.int32), n_used - 1)
    blk_e = jnp.minimum(jnp.searchsorted(pends, blk * MOE_BM, side="right"), N_EXPERTS - 1).astype(jnp.int32)
    return pos, blk_e, n_used.reshape(1), cap


def kernel(x, p, norm_mix, w_in, pool_w, pool_scale, conv_w, a_log, dt_bias, dn_norm, w_up_pool, w_up_dn, w_out, norm_moe, w_router_group, b_router_group, w_router_expert, b_router_expert, w_gate, w_up, w_down, norm_ple, w_ple_gate, w_ple_proj, norm_final):
    bsz, seq, d = x.shape
    t = bsz * seq
    depth = norm_mix.shape[0]
    prm = dict(norm_mix=norm_mix, w_in=w_in, pool_w=pool_w, pool_scale=pool_scale, conv_w=conv_w,
               a_log=a_log, dt_bias=dt_bias, dn_norm=dn_norm, w_up_pool=w_up_pool)
    h = x
    for i in range(depth):
        st = _stage_mixers(h, prm, i)
        w_router, b_router = _router_weights(w_router_group[i], b_router_group[i],
                                             w_router_expert[i], b_router_expert[i])
        tm = _pick(t, 256)
        h1, n2, route, cnt = _merge(st["y_dn"], st["proj"], st["mp"], h.reshape(t, d),
                                    w_up_dn[i].astype(BF16), w_out[i].astype(BF16), norm_moe[i][None, :],
                                    w_router, b_router, tm=tm)
        pos, blk_e, n_used, cap = _stage_moe_plan(route, cnt, t)
        xs = _dispatch(pos, n2, cap, tb=_pick(t, 512))
        ys = _experts(blk_e, n_used, xs, w_gate[i].astype(BF16), w_up[i].astype(BF16),
                      w_down[i].astype(BF16), bm=MOE_BM)
        out = _combine(pos, h1, route, p[i].reshape(t, -1), ys, w_ple_gate[i].astype(BF16),
                       w_ple_proj[i].astype(BF16), norm_ple[i][None, :], norm_final[None, :],
                       tm=tm, final_norm=(i == depth - 1))
        h = out.reshape(bsz, seq, d)
    return h
```

```python
import functools

import jax
import jax.numpy as jnp
from jax import lax
from jax.experimental import pallas as pl
from jax.experimental.pallas import tpu as pltpu

F32 = jnp.float32
BF16 = jnp.bfloat16

EPS = 1e-6
POOL_WINDOWS = (2, 4, 8, 16)
POOL_HALO = 16
CONV_K = 4
CONV_HALO = 8
DN_HEADS = 8
DN_DK = 128
CHUNK = 64
N_GROUPS = 4
EXPERTS_PER_GROUP = 8
N_EXPERTS = N_GROUPS * EXPERTS_PER_GROUP
LANES = 128
VMEM_LIMIT = 56 * 1024 * 1024

COL_GATE_POOL = 0
COL_GATE_DN = 2048
COL_POOL = 4096
COL_Q = 5120
COL_K = 6144
COL_V = 7168
COL_Z = 8192
N_MAIN = 9216


def _cparams(sem):
    return pltpu.CompilerParams(dimension_semantics=sem, vmem_limit_bytes=VMEM_LIMIT)


def _resident(shape):
    nd = len(shape)
    return pl.BlockSpec(shape, lambda *_: (0,) * nd, pipeline_mode=pl.Buffered(1))


def _inproj_kernel(x_ref, gain_ref, w_ref, wba_ref, out_ref, ba_ref, n1_ref, *, sub):
    tm = x_ref.shape[0]

    @pl.when(pl.program_id(1) == 0)
    def _():
        def body(r, _):
            rows = pl.ds(pl.multiple_of(r * sub, sub), sub)
            x = x_ref[rows, :]
            y = x * lax.rsqrt(jnp.mean(x * x, axis=-1, keepdims=True) + EPS) * gain_ref[...]
            n1_ref[rows, :] = y.astype(BF16)
            return 0
        lax.fori_loop(0, tm // sub, body, 0)
        ba_ref[...] = jnp.dot(n1_ref[...], wba_ref[...], preferred_element_type=F32)

    out_ref[...] = jnp.dot(n1_ref[...], w_ref[...], preferred_element_type=F32)


def _inproj(x2d, gain, w_main, w_ba, *, tm, tn):
    t, d = x2d.shape
    n = w_main.shape[1]
    return pl.pallas_call(
        functools.partial(_inproj_kernel, sub=min(tm, 256)),
        grid=(t // tm, n // tn),
        in_specs=[
            pl.BlockSpec((tm, d), lambda i, j: (i, 0)),
            _resident((1, d)),
            pl.BlockSpec((d, tn), lambda i, j: (0, j)),
            _resident((d, LANES)),
        ],
        out_specs=[
            pl.BlockSpec((tm, tn), lambda i, j: (i, j)),
            pl.BlockSpec((tm, LANES), lambda i, j: (i, 0)),
        ],
        out_shape=[jax.ShapeDtypeStruct((t, n), F32), jax.ShapeDtypeStruct((t, LANES), F32)],
        scratch_shapes=[pltpu.VMEM((tm, d), BF16)],
        compiler_params=_cparams(("parallel", "arbitrary")),
        name="inproj",
    )(x2d, gain, w_main, w_ba)


def _pool_kernel(u_ref, gp_ref, pw_ref, scale_ref, wup_ref, out_ref, ext_ref):
    ts = u_ref.shape[0]
    gw = pw_ref.shape[1]
    s = pl.program_id(1)

    @pl.when(s == 0)
    def _():
        ext_ref[0:POOL_HALO, :] = jnp.zeros((POOL_HALO, ext_ref.shape[1]), F32)

    @pl.when(s > 0)
    def _():
        ext_ref[0:POOL_HALO, :] = ext_ref[ts:ts + POOL_HALO, :]

    ext_ref[POOL_HALO:POOL_HALO + ts, :] = u_ref[...]

    t1 = (s * ts + 1 + lax.broadcasted_iota(jnp.int32, (ts, 1), 0)).astype(F32)
    acc = jnp.zeros(out_ref.shape, F32)
    for g, w in enumerate(POOL_WINDOWS):
        cols = slice(g * gw, (g + 1) * gw)
        cur = ext_ref[POOL_HALO:POOL_HALO + ts, cols]
        win = cur
        for j in range(1, w):
            win = win + ext_ref[POOL_HALO - j:POOL_HALO - j + ts, cols]
        d = win / jnp.minimum(t1, float(w)) - cur
        y = jnp.dot(d.astype(BF16), pw_ref[g], preferred_element_type=F32) * scale_ref[:, cols]
        acc = acc + jnp.dot(y.astype(BF16), wup_ref[cols, :], preferred_element_type=F32)
    out_ref[...] = jax.nn.sigmoid(gp_ref[...]) * acc


def _pool(proj, pool_w, pool_scale, w_up_pool, *, bsz, seq, ts):
    t = proj.shape[0]
    ns = seq // ts
    d_pool = w_up_pool.shape[0]
    d = w_up_pool.shape[1]
    return pl.pallas_call(
        _pool_kernel,
        grid=(bsz, ns),
        in_specs=[
            pl.BlockSpec((ts, d_pool), lambda b, s: (b * ns + s, COL_POOL // d_pool)),
            pl.BlockSpec((ts, d), lambda b, s: (b * ns + s, COL_GATE_POOL // d)),
            _resident(pool_w.shape),
            _resident((1, d_pool)),
            _resident(w_up_pool.shape),
        ],
        out_specs=pl.BlockSpec((ts, d), lambda b, s: (b * ns + s, 0)),
        out_shape=jax.ShapeDtypeStruct((t, d), F32),
        scratch_shapes=[pltpu.VMEM((POOL_HALO + ts, d_pool), F32)],
        compiler_params=_cparams(("parallel", "arbitrary")),
        name="pool",
    )(proj, proj, pool_w, pool_scale, w_up_pool)


def _prep_kernel(q_ref, k_ref, v_ref, ba_ref, cw_ref, alog_ref, dtb_ref,
                 qn_ref, kn_ref, vv_ref, bx_ref, gx_ref, gt_ref, ext_ref):
    ts = q_ref.shape[0]
    dq = q_ref.shape[1]
    s = pl.program_id(1)

    @pl.when(s == 0)
    def _():
        ext_ref[0:CONV_HALO, :] = jnp.zeros((CONV_HALO, ext_ref.shape[1]), F32)

    @pl.when(s > 0)
    def _():
        ext_ref[0:CONV_HALO, :] = ext_ref[ts:ts + CONV_HALO, :]

    ext_ref[CONV_HALO:CONV_HALO + ts, 0:dq] = q_ref[...]
    ext_ref[CONV_HALO:CONV_HALO + ts, dq:2 * dq] = k_ref[...]
    ext_ref[CONV_HALO:CONV_HALO + ts, 2 * dq:3 * dq] = v_ref[...]

    def conv_silu(c0, width):
        cols = slice(c0, c0 + width)
        first = CONV_HALO - (CONV_K - 1)
        y = ext_ref[first:first + ts, cols] * cw_ref[0:1, cols]
        for tap in range(1, CONV_K):
            y = y + ext_ref[first + tap:first + tap + ts, cols] * cw_ref[tap:tap + 1, cols]
        return y * jax.nn.sigmoid(y)

    for h in range(DN_HEADS):
        hc = slice(h * DN_DK, (h + 1) * DN_DK)
        qh = conv_silu(h * DN_DK, DN_DK)
        qn_ref[:, hc] = qh * (lax.rsqrt(jnp.sum(qh * qh, axis=-1, keepdims=True) + EPS) * (DN_DK ** -0.5))
        kh = conv_silu(dq + h * DN_DK, DN_DK)
        kn_ref[:, hc] = kh * lax.rsqrt(jnp.sum(kh * kh, axis=-1, keepdims=True) + EPS)
        vv_ref[:, hc] = conv_silu(2 * dq + h * DN_DK, DN_DK)

    ba = ba_ref[...]
    beta = jax.nn.sigmoid(ba)
    xs = ba + dtb_ref[...]
    softplus = jnp.maximum(xs, 0.0) + jnp.log1p(jnp.exp(-jnp.abs(xs)))
    g = -jnp.exp(alog_ref[...]) * softplus
    ri = lax.broadcasted_iota(jnp.int32, (ts, ts), 0)
    ci = lax.broadcasted_iota(jnp.int32, (ts, ts), 1)
    tri = jnp.where((ri // CHUNK == ci // CHUNK) & (ci <= ri), 1.0, 0.0).astype(F32)
    gcum = jnp.dot(tri, g, preferred_element_type=F32, precision=lax.Precision.HIGHEST)
    gt_ref[0] = gcum.T[DN_HEADS:2 * DN_HEADS, :]
    for h in range(DN_HEADS):
        hc = slice(h * DN_DK, (h + 1) * DN_DK)
        bx_ref[:, hc] = jnp.broadcast_to(beta[:, h:h + 1], (ts, DN_DK))
        gx_ref[:, hc] = jnp.broadcast_to(gcum[:, DN_HEADS + h:DN_HEADS + h + 1], (ts, DN_DK))


def _prep(proj, ba, conv_w, alog_pad, dtb_pad, *, bsz, seq, ts):
    t = proj.shape[0]
    ns = seq // ts
    dq = DN_HEADS * DN_DK
    row = lambda b, s: b * ns + s
    blk = lambda col: pl.BlockSpec((ts, dq), lambda b, s: (row(b, s), col // dq))
    out_blk = pl.BlockSpec((ts, dq), lambda b, s: (row(b, s), 0))
    return pl.pallas_call(
        _prep_kernel,
        grid=(bsz, ns),
        in_specs=[
            blk(COL_Q), blk(COL_K), blk(COL_V),
            pl.BlockSpec((ts, LANES), lambda b, s: (row(b, s), 0)),
            _resident(conv_w.shape),
            _resident((1, LANES)),
            _resident((1, LANES)),
        ],
        out_specs=[out_blk, out_blk, out_blk, out_blk, out_blk,
                   pl.BlockSpec((1, DN_HEADS, ts), lambda b, s: (row(b, s), 0, 0))],
        out_shape=[jax.ShapeDtypeStruct((t, dq), F32)] * 5
                  + [jax.ShapeDtypeStruct((t // ts, DN_HEADS, ts), F32)],
        scratch_shapes=[pltpu.VMEM((CONV_HALO + ts, 3 * dq), F32)],
        compiler_params=_cparams(("parallel", "arbitrary")),
        name="prep",
    )(proj, proj, proj, ba, conv_w, alog_pad, dtb_pad)


def _dot_nt(a, b):
    return lax.dot_general(a, b, (((1,), (1,)), ((), ())), preferred_element_type=F32)


def _dot_tn(a, b):
    return lax.dot_general(a, b, (((0,), (0,)), ((), ())), preferred_element_type=F32)


def _delta_kernel(q_ref, k_ref, v_ref, z_ref, bx_ref, gx_ref, gt_ref, dnw_ref, out_ref, state_ref):
    ts = q_ref.shape[0]

    @pl.when(pl.program_id(2) == 0)
    def _():
        state_ref[...] = jnp.zeros(state_ref.shape, F32)

    ii = lax.broadcasted_iota(jnp.int32, (CHUNK, CHUNK), 0)
    jj = lax.broadcasted_iota(jnp.int32, (CHUNK, CHUNK), 1)
    incl = ii >= jj
    strict = ii > jj

    for c in range(ts // CHUNK):
        rows = slice(c * CHUNK, (c + 1) * CHUNK)
        q = q_ref[rows, :]
        k = k_ref[rows, :]
        v = v_ref[rows, :]
        beta = bx_ref[rows, :]
        gc = gx_ref[rows, :]
        gr = gt_ref[0, :, rows]
        decay = jnp.where(incl, jnp.exp(gc[:, :CHUNK] - gr), 0.0)
        kb = k * beta
        k16 = k.astype(BF16)
        lower = jnp.where(strict, _dot_nt(kb.astype(BF16), k16) * decay, 0.0)
        attn = _dot_nt(q.astype(BF16), k16) * decay
        egc = jnp.exp(gc)
        sol = jnp.concatenate([v * beta, kb * egc], axis=1)
        powers = [lower]
        while (1 << len(powers)) < CHUNK:
            p16 = powers[-1].astype(BF16)
            powers.append(jnp.dot(p16, p16, preferred_element_type=F32))
        for pw in reversed(powers[1:]):
            sol = sol + jnp.dot(pw.astype(BF16), sol.astype(BF16), preferred_element_type=F32)
        sol = sol - jnp.dot(powers[0].astype(BF16), sol.astype(BF16), preferred_element_type=F32)
        u = sol[:, :DN_DK]
        w = sol[:, DN_DK:]

        state = state_ref[...]
        s16 = state.astype(BF16)
        v_new = u - jnp.dot(w.astype(BF16), s16, preferred_element_type=F32)
        vn16 = v_new.astype(BF16)
        o = (jnp.dot((q * egc).astype(BF16), s16, preferred_element_type=F32)
             + jnp.dot(attn.astype(BF16), vn16, preferred_element_type=F32))
        g_last = gc[CHUNK - 1:CHUNK, :]
        k_dec = k * jnp.exp(g_last - gc)
        state_ref[...] = state * jnp.exp(g_last) + _dot_tn(k_dec.astype(BF16), vn16)

        on = o * lax.rsqrt(jnp.mean(o * o, axis=-1, keepdims=True) + EPS) * dnw_ref[...]
        z = z_ref[rows, :]
        out_ref[rows, :] = (on * (z * jax.nn.sigmoid(z))).astype(out_ref.dtype)


def _delta(qn, kn, vv, proj, bx, gx, gt, dn_norm, *, bsz, seq, ts):
    t = qn.shape[0]
    ns = seq // ts
    row = lambda b, s: b * ns + s
    hblk = pl.BlockSpec((ts, DN_DK), lambda b, h, s: (row(b, s), h))
    return pl.pallas_call(
        _delta_kernel,
        grid=(bsz, DN_HEADS, ns),
        in_specs=[
            hblk, hblk, hblk,
            pl.BlockSpec((ts, DN_DK), lambda b, h, s: (row(b, s), COL_Z // DN_DK + h)),
            hblk, hblk,
            pl.BlockSpec((1, 1, ts), lambda b, h, s: (row(b, s) * DN_HEADS + h, 0, 0)),
            _resident((1, DN_DK)),
        ],
        out_specs=hblk,
        out_shape=jax.ShapeDtypeStruct((t, DN_HEADS * DN_DK), BF16),
        scratch_shapes=[pltpu.VMEM((DN_DK, DN_DK), F32)],
        compiler_params=_cparams(("parallel", "parallel", "arbitrary")),
        name="delta",
    )(qn, kn, vv, proj, bx, gx, gt.reshape(-1, 1, ts), dn_norm)


ROUTE_E1, ROUTE_E2, ROUTE_W1, ROUTE_W2, ROUTE_R1, ROUTE_R2 = range(6)
ROUTER_LANE0 = N_GROUPS
NEG_BIG = -1e30


def _merge_kernel(ydn_ref, gd_ref, mp_ref, x_ref, wupdn_ref, wout_ref, gain_ref, wr_ref, br_ref,
                  h1_ref, n2_ref, route_ref, cnt_ref, carry_ref):
    tm = x_ref.shape[0]

    @pl.when(pl.program_id(0) == 0)
    def _():
        carry_ref[...] = jnp.zeros(carry_ref.shape, F32)

    md = jax.nn.sigmoid(gd_ref[...]) * jnp.dot(ydn_ref[...], wupdn_ref[...], preferred_element_type=F32)
    merged = mp_ref[...] + md
    h1 = x_ref[...] + jnp.dot(merged.astype(BF16), wout_ref[...], preferred_element_type=F32)
    h1_ref[...] = h1
    n2 = h1 * lax.rsqrt(jnp.mean(h1 * h1, axis=-1, keepdims=True) + EPS) * gain_ref[...]
    n2_ref[...] = n2

    hi = n2.astype(BF16)
    lo = (n2 - hi.astype(F32)).astype(BF16)
    r_hi = jnp.dot(hi, wr_ref[...], preferred_element_type=F32)
    r_lo = jnp.dot(lo, wr_ref[:, :LANES], preferred_element_type=F32)
    logits = r_hi[:, :LANES] + r_hi[:, LANES:] + r_lo + br_ref[...]

    lane = lax.broadcasted_iota(jnp.int32, (tm, LANES), 1)
    lane_f = lane.astype(F32)
    first_max = lambda v, m: jnp.min(jnp.where(v == m, lane_f, float(LANES)), axis=-1, keepdims=True)

    is_grp = lane < N_GROUPS
    lg = jnp.where(is_grp, logits, NEG_BIG)
    gmax = jnp.max(lg, axis=-1, keepdims=True)
    gi = first_max(lg, gmax)
    p_sel = 1.0 / jnp.sum(jnp.where(is_grp, jnp.exp(lg - gmax), 0.0), axis=-1, keepdims=True)

    e_lane = lane - ROUTER_LANE0
    in_grp = (e_lane >= 0) & (e_lane < N_EXPERTS) & ((e_lane // EXPERTS_PER_GROUP).astype(F32) == gi)
    le = jnp.where(in_grp, logits, NEG_BIG)
    v1 = jnp.max(le, axis=-1, keepdims=True)
    i1 = first_max(le, v1)
    le2 = jnp.where(lane_f == i1, NEG_BIG, le)
    v2 = jnp.max(le2, axis=-1, keepdims=True)
    i2 = first_max(le2, v2)
    s = jnp.exp(v2 - v1)
    w1 = p_sel / (1.0 + s)
    w2 = p_sel * s / (1.0 + s)
    e1 = i1 - float(ROUTER_LANE0)
    e2 = i2 - float(ROUTER_LANE0)

    oh1 = jnp.where(lane_f == e1, 1.0, 0.0)
    oh2 = jnp.where(lane_f == e2, 1.0, 0.0)
    both = oh1 + oh2
    ri = lax.broadcasted_iota(jnp.int32, (tm, tm), 0)
    ci = lax.broadcasted_iota(jnp.int32, (tm, tm), 1)
    tri = jnp.where(ri > ci, 1.0, 0.0).astype(BF16)
    prior = jnp.dot(tri, both.astype(BF16), preferred_element_type=F32) + carry_ref[...]
    r1 = jnp.sum(prior * oh1, axis=-1, keepdims=True)
    r2 = jnp.sum(prior * oh2, axis=-1, keepdims=True)
    carry = carry_ref[...] + jnp.sum(both, axis=0, keepdims=True)
    carry_ref[...] = carry
    cnt_ref[...] = carry

    route = jnp.zeros((tm, LANES), F32)
    for ln, val in ((ROUTE_E1, e1), (ROUTE_E2, e2), (ROUTE_W1, w1), (ROUTE_W2, w2),
                    (ROUTE_R1, r1), (ROUTE_R2, r2)):
        route = jnp.where(lane == ln, val, route)
    route_ref[...] = route


def _merge(y_dn, proj, mp, x2d, w_up_dn, w_out, gain, w_router, b_router, *, tm):
    t, d = x2d.shape
    dv = y_dn.shape[1]
    rows = lambda blk: pl.BlockSpec(blk, lambda i: (i, 0))
    return pl.pallas_call(
        _merge_kernel,
        grid=(t // tm,),
        in_specs=[
            rows((tm, dv)),
            pl.BlockSpec((tm, d), lambda i: (i, COL_GATE_DN // d)),
            rows((tm, d)), rows((tm, d)),
            _resident(w_up_dn.shape), _resident(w_out.shape), _resident((1, d)),
            _resident(w_router.shape), _resident((1, LANES)),
        ],
        out_specs=[rows((tm, d)), rows((tm, d)), rows((tm, LANES)),
                   pl.BlockSpec((1, LANES), lambda i: (0, 0))],
        out_shape=[jax.ShapeDtypeStruct((t, d), F32), jax.ShapeDtypeStruct((t, d), F32),
                   jax.ShapeDtypeStruct((t, LANES), F32), jax.ShapeDtypeStruct((1, LANES), F32)],
        scratch_shapes=[pltpu.VMEM((1, LANES), F32)],
        compiler_params=_cparams(("arbitrary",)),
        name="merge",
    )(y_dn, proj, mp, x2d, w_up_dn, w_out, gain, w_router, b_router)


def _dispatch_kernel(pos_ref, n2_ref, init_ref, xs_ref, sem, *, tb):
    del init_ref
    base = pl.program_id(0) * tb

    def body(j, _):
        for k in range(2):
            dst = pos_ref[2 * (base + j) + k]
            pltpu.make_async_copy(n2_ref.at[pl.ds(j, 1)], xs_ref.at[pl.ds(dst, 1)], sem).start()
        return 0
    lax.fori_loop(0, tb, body, 0)
    for _ in range(2):
        pltpu.make_async_copy(n2_ref, xs_ref.at[pl.ds(0, tb)], sem).wait()


def _dispatch(pos, n2, cap, *, tb):
    t, d = n2.shape
    return pl.pallas_call(
        functools.partial(_dispatch_kernel, tb=tb),
        grid_spec=pltpu.PrefetchScalarGridSpec(
            num_scalar_prefetch=1,
            grid=(t // tb,),
            in_specs=[pl.BlockSpec((tb, d), lambda i, pos: (i, 0)), pl.BlockSpec(memory_space=pl.ANY)],
            out_specs=pl.BlockSpec(memory_space=pl.ANY),
            scratch_shapes=[pltpu.SemaphoreType.DMA],
        ),
        out_shape=jax.ShapeDtypeStruct((cap, d), F32),
        input_output_aliases={2: 0},
        compiler_params=_cparams(("arbitrary",)),
        name="dispatch",
    )(pos, n2, jnp.zeros((cap, d), F32))


def _experts_kernel(be_ref, nu_ref, x_ref, wg_ref, wu_ref, wd_ref, y_ref):
    del be_ref

    @pl.when(pl.program_id(0) < nu_ref[0])
    def _():
        xb = x_ref[...].astype(BF16)
        g = jnp.dot(xb, wg_ref[0], preferred_element_type=F32)
        u = jnp.dot(xb, wu_ref[0], preferred_element_type=F32)
        hmid = (g * jax.nn.sigmoid(g)) * u
        y_ref[...] = jnp.dot(hmid.astype(BF16), wd_ref[0], preferred_element_type=F32)

    @pl.when(pl.program_id(0) >= nu_ref[0])
    def _():
        y_ref[...] = jnp.zeros(y_ref.shape, F32)


def _experts(blk_e, n_used, xs, w_gate, w_up, w_down, *, bm):
    cap, d = xs.shape
    de = w_gate.shape[2]
    live = lambda i, be, nu: (jnp.minimum(i, nu[0] - 1), 0)
    return pl.pallas_call(
        _experts_kernel,
        grid_spec=pltpu.PrefetchScalarGridSpec(
            num_scalar_prefetch=2,
            grid=(cap // bm,),
            in_specs=[
                pl.BlockSpec((bm, d), live),
                pl.BlockSpec((1, d, de), lambda i, be, nu: (be[i], 0, 0)),
                pl.BlockSpec((1, d, de), lambda i, be, nu: (be[i], 0, 0)),
                pl.BlockSpec((1, de, d), lambda i, be, nu: (be[i], 0, 0)),
            ],
            out_specs=pl.BlockSpec((bm, d), lambda i, be, nu: (i, 0)),
        ),
        out_shape=jax.ShapeDtypeStruct((cap, d), F32),
        compiler_params=_cparams(("arbitrary",)),
        name="experts",
    )(blk_e, n_used, xs, w_gate, w_up, w_down)


def _combine_kernel(pos_ref, h1_ref, route_ref, p_ref, ys_ref, wpg_ref, wpp_ref, gple_ref, gfin_ref,
                    out_ref, ybuf, sem, *, final_norm):
    tm = h1_ref.shape[0]
    i = pl.program_id(0)
    n = pl.num_programs(0)
    slot = i % 2

    def issue(blk, sl):
        def body(j, _):
            tok = blk * tm + j
            for k in range(2):
                pltpu.make_async_copy(ys_ref.at[pl.ds(pos_ref[2 * tok + k], 1)],
                                      ybuf.at[sl, pl.ds(k * tm + j, 1)], sem.at[sl]).start()
            return 0
        lax.fori_loop(0, tm, body, 0)

    @pl.when(i == 0)
    def _():
        issue(0, 0)

    @pl.when(i + 1 < n)
    def _():
        issue(i + 1, 1 - slot)

    pltpu.make_async_copy(ys_ref.at[pl.ds(0, 2 * tm)], ybuf.at[slot], sem.at[slot]).wait()

    route = route_ref[...]
    y_moe = (ybuf[slot, 0:tm, :] * route[:, ROUTE_W1:ROUTE_W1 + 1]
             + ybuf[slot, tm:2 * tm, :] * route[:, ROUTE_W2:ROUTE_W2 + 1])
    h2 = h1_ref[...] + y_moe
    n3 = h2 * lax.rsqrt(jnp.mean(h2 * h2, axis=-1, keepdims=True) + EPS) * gple_ref[...]
    gate = jax.nn.sigmoid(jnp.dot(n3.astype(BF16), wpg_ref[...], preferred_element_type=F32))
    h3 = h2 + gate * jnp.dot(p_ref[...].astype(BF16), wpp_ref[...], preferred_element_type=F32)
    if final_norm:
        h3 = h3 * lax.rsqrt(jnp.mean(h3 * h3, axis=-1, keepdims=True) + EPS) * gfin_ref[...]
    out_ref[...] = h3


def _combine(pos, h1, route, p2d, ys, w_ple_gate, w_ple_proj, g_ple, g_fin, *, tm, final_norm):
    t, d = h1.shape
    rows = lambda w: pl.BlockSpec((tm, w), lambda i, pos: (i, 0))
    res = lambda shape: pl.BlockSpec(shape, lambda i, pos: (0,) * len(shape), pipeline_mode=pl.Buffered(1))
    return pl.pallas_call(
        functools.partial(_combine_kernel, final_norm=final_norm),
        grid_spec=pltpu.PrefetchScalarGridSpec(
            num_scalar_prefetch=1,
            grid=(t // tm,),
            in_specs=[rows(d), rows(LANES), rows(p2d.shape[1]), pl.BlockSpec(memory_space=pl.ANY),
                      res(w_ple_gate.shape), res(w_ple_proj.shape), res((1, d)), res((1, d))],
            out_specs=rows(d),
            scratch_shapes=[pltpu.VMEM((2, 2 * tm, d), F32), pltpu.SemaphoreType.DMA((2,))],
        ),
        out_shape=jax.ShapeDtypeStruct((t, d), F32),
        compiler_params=_cparams(("arbitrary",)),
        name="combine",
    )(pos, h1, route, p2d, ys, w_ple_gate, w_ple_proj, g_ple, g_fin)


def _pick(n, pref):
    b = min(n, pref)
    while n % b:
        b -= 8
    return b


def _stage_mixers(x, prm, i):
    bsz, seq, d = x.shape
    t = bsz * seq
    x2d = x.reshape(t, d)
    w_in = prm["w_in"][i]
    d_pool = prm["w_up_pool"].shape[1]
    dq = DN_HEADS * DN_DK
    o_qkv = d_pool
    o_z = o_qkv + 3 * dq
    o_b = o_z + dq
    o_gp = o_b + 2 * DN_HEADS
    o_gd = o_gp + d
    w_main = jnp.concatenate(
        [w_in[:, o_gp:o_gp + d], w_in[:, o_gd:o_gd + d], w_in[:, :d_pool],
         w_in[:, o_qkv:o_qkv + 3 * dq], w_in[:, o_z:o_z + dq]], axis=1).astype(BF16)
    assert w_main.shape[1] == N_MAIN
    w_ba = jnp.pad(w_in[:, o_b:o_b + 2 * DN_HEADS], ((0, 0), (0, LANES - 2 * DN_HEADS))).astype(BF16)
    proj, ba = _inproj(x2d, prm["norm_mix"][i][None, :], w_main, w_ba,
                       tm=_pick(t, 1024), tn=1024)

    ts = _pick(seq, 512)
    mp = _pool(proj, prm["pool_w"][i].astype(BF16), prm["pool_scale"][i][None, :],
               prm["w_up_pool"][i].astype(BF16), bsz=bsz, seq=seq, ts=ts)

    lane_pad = lambda a: jnp.pad(a, (DN_HEADS, LANES - 2 * DN_HEADS))[None, :]
    qn, kn, vv, bx, gx, gt = _prep(proj, ba, prm["conv_w"][i], lane_pad(prm["a_log"][i]),
                                   lane_pad(prm["dt_bias"][i]), bsz=bsz, seq=seq, ts=ts)
    y_dn = _delta(qn, kn, vv, proj, bx, gx, gt, prm["dn_norm"][i][None, :], bsz=bsz, seq=seq, ts=ts)
    return dict(proj=proj, ba=ba, mp=mp, qn=qn, kn=kn, vv=vv, bx=bx, gx=gx, gt=gt, y_dn=y_dn)


MOE_BM = 256


def _router_weights(w_rg, b_rg, w_re, b_re):
    d = w_rg.shape[0]
    w = jnp.concatenate([w_rg, w_re, jnp.zeros((d, LANES - N_GROUPS - N_EXPERTS), F32)], axis=1)
    w_hi = w.astype(BF16)
    w_lo = (w - w_hi.astype(F32)).astype(BF16)
    b = jnp.concatenate([b_rg, b_re, jnp.zeros((LANES - N_GROUPS - N_EXPERTS,), F32)])[None, :]
    return jnp.concatenate([w_hi, w_lo], axis=1), b


def _stage_moe_plan(route, cnt, t):
    counts = cnt[0, :N_EXPERTS].astype(jnp.int32)
    padded = ((counts + MOE_BM - 1) // MOE_BM) * MOE_BM
    pends = jnp.cumsum(padded)
    pstarts = pends - padded
    eid = route[:, ROUTE_E1:ROUTE_E2 + 1].astype(jnp.int32)
    rank = route[:, ROUTE_R1:ROUTE_R2 + 1].astype(jnp.int32)
    pos = (pstarts[eid] + rank).reshape(2 * t)
    cap = 2 * t + N_EXPERTS * MOE_BM
    n_blocks = cap // MOE_BM
    n_used = (pends[-1] // MOE_BM).astype(jnp.int32)
    blk = jnp.minimum(jnp.arange(n_blocks, dtype=jnp.int32), n_used - 1)
    blk_e = jnp.sum((pends[None, :] <= (blk * MOE_BM)[:, None]).astype(jnp.int32), axis=1)
    blk_e = jnp.minimum(blk_e, N_EXPERTS - 1)
    return pos, blk_e, n_used.reshape(1), cap


def kernel(x, p, norm_mix, w_in, pool_w, pool_scale, conv_w, a_log, dt_bias, dn_norm, w_up_pool, w_up_dn, w_out, norm_moe, w_router_group, b_router_group, w_router_expert, b_router_expert, w_gate, w_up, w_down, norm_ple, w_ple_gate, w_ple_proj, norm_final):
    bsz, seq, d = x.shape
    t = bsz * seq
    depth = norm_mix.shape[0]
    prm = dict(norm_mix=norm_mix, w_in=w_in, pool_w=pool_w, pool_scale=pool_scale, conv_w=conv_w,
               a_log=a_log, dt_bias=dt_bias, dn_norm=dn_norm, w_up_pool=w_up_pool)
    h = x
    for i in range(depth):
        st = _stage_mixers(h, prm, i)
        w_router, b_router = _router_weights(w_router_group[i], b_router_group[i],
                                             w_router_expert[i], b_router_expert[i])
        tm = _pick(t, 256)
        h1, n2, route, cnt = _merge(st["y_dn"], st["proj"], st["mp"], h.reshape(t, d),
                                    w_up_dn[i].astype(BF16), w_out[i].astype(BF16), norm_moe[i][None, :],
                                    w_router, b_router, tm=tm)
        pos, blk_e, n_used, cap = _stage_moe_plan(route, cnt, t)
        xs = _dispatch(pos, n2, cap, tb=_pick(t, 512))
        ys = _experts(blk_e, n_used, xs, w_gate[i].astype(BF16), w_up[i].astype(BF16),
                      w_down[i].astype(BF16), bm=MOE_BM)
        out = _combine(pos, h1, route, p[i].reshape(t, -1), ys, w_ple_gate[i].astype(BF16),
                       w_ple_proj[i].astype(BF16), norm_ple[i][None, :], norm_final[None, :],
                       tm=tm, final_norm=(i == depth - 1))
        h = out.reshape(bsz, seq, d)
    return h
```

```python
import functools

import jax
import jax.numpy as jnp
from jax import lax
from jax.experimental import pallas as pl
from jax.experimental.pallas import tpu as pltpu

F32 = jnp.float32
BF16 = jnp.bfloat16

EPS = 1e-6
POOL_WINDOWS = (2, 4, 8, 16)
POOL_HALO = 16
CONV_K = 4
CONV_HALO = 8
DN_HEADS = 8
DN_DK = 128
CHUNK = 64
N_GROUPS = 4
EXPERTS_PER_GROUP = 8
N_EXPERTS = N_GROUPS * EXPERTS_PER_GROUP
LANES = 128
VMEM_LIMIT = 56 * 1024 * 1024

COL_GATE_POOL = 0
COL_GATE_DN = 2048
COL_POOL = 4096
COL_Q = 5120
COL_K = 6144
COL_V = 7168
COL_Z = 8192
N_MAIN = 9216


def _cparams(sem):
    return pltpu.CompilerParams(dimension_semantics=sem, vmem_limit_bytes=VMEM_LIMIT)


def _resident(shape):
    nd = len(shape)
    return pl.BlockSpec(shape, lambda *_: (0,) * nd, pipeline_mode=pl.Buffered(1))


def _inproj_kernel(x_ref, gain_ref, w_ref, wba_ref, out_ref, ba_ref, n1_ref, *, sub):
    tm = x_ref.shape[0]

    @pl.when(pl.program_id(1) == 0)
    def _():
        def body(r, _):
            rows = pl.ds(pl.multiple_of(r * sub, sub), sub)
            x = x_ref[rows, :]
            y = x * lax.rsqrt(jnp.mean(x * x, axis=-1, keepdims=True) + EPS) * gain_ref[...]
            n1_ref[rows, :] = y.astype(BF16)
            return 0
        lax.fori_loop(0, tm // sub, body, 0)
        ba_ref[...] = jnp.dot(n1_ref[...], wba_ref[...], preferred_element_type=F32)

    out_ref[...] = jnp.dot(n1_ref[...], w_ref[...], preferred_element_type=F32)


def _inproj(x2d, gain, w_main, w_ba, *, tm, tn):
    t, d = x2d.shape
    n = w_main.shape[1]
    return pl.pallas_call(
        functools.partial(_inproj_kernel, sub=min(tm, 256)),
        grid=(t // tm, n // tn),
        in_specs=[
            pl.BlockSpec((tm, d), lambda i, j: (i, 0)),
            _resident((1, d)),
            pl.BlockSpec((d, tn), lambda i, j: (0, j)),
            _resident((d, LANES)),
        ],
        out_specs=[
            pl.BlockSpec((tm, tn), lambda i, j: (i, j)),
            pl.BlockSpec((tm, LANES), lambda i, j: (i, 0)),
        ],
        out_shape=[jax.ShapeDtypeStruct((t, n), F32), jax.ShapeDtypeStruct((t, LANES), F32)],
        scratch_shapes=[pltpu.VMEM((tm, d), BF16)],
        compiler_params=_cparams(("parallel", "arbitrary")),
        name="inproj",
    )(x2d, gain, w_main, w_ba)


def _pool_kernel(u_ref, gp_ref, pw_ref, scale_ref, wup_ref, out_ref, ext_ref):
    ts = u_ref.shape[0]
    gw = pw_ref.shape[1]
    s = pl.program_id(1)

    @pl.when(s == 0)
    def _():
        ext_ref[0:POOL_HALO, :] = jnp.zeros((POOL_HALO, ext_ref.shape[1]), F32)

    @pl.when(s > 0)
    def _():
        ext_ref[0:POOL_HALO, :] = ext_ref[ts:ts + POOL_HALO, :]

    ext_ref[POOL_HALO:POOL_HALO + ts, :] = u_ref[...]

    t1 = (s * ts + 1 + lax.broadcasted_iota(jnp.int32, (ts, 1), 0)).astype(F32)
    acc = jnp.zeros(out_ref.shape, F32)
    for g, w in enumerate(POOL_WINDOWS):
        cols = slice(g * gw, (g + 1) * gw)
        cur = ext_ref[POOL_HALO:POOL_HALO + ts, cols]
        win = cur
        for j in range(1, w):
            win = win + ext_ref[POOL_HALO - j:POOL_HALO - j + ts, cols]
        d = win / jnp.minimum(t1, float(w)) - cur
        y = jnp.dot(d.astype(BF16), pw_ref[g], preferred_element_type=F32) * scale_ref[:, cols]
        acc = acc + jnp.dot(y.astype(BF16), wup_ref[cols, :], preferred_element_type=F32)
    out_ref[...] = jax.nn.sigmoid(gp_ref[...]) * acc


def _pool(proj, pool_w, pool_scale, w_up_pool, *, bsz, seq, ts):
    t = proj.shape[0]
    ns = seq // ts
    d_pool = w_up_pool.shape[0]
    d = w_up_pool.shape[1]
    return pl.pallas_call(
        _pool_kernel,
        grid=(bsz, ns),
        in_specs=[
            pl.BlockSpec((ts, d_pool), lambda b, s: (b * ns + s, COL_POOL // d_pool)),
            pl.BlockSpec((ts, d), lambda b, s: (b * ns + s, COL_GATE_POOL // d)),
            _resident(pool_w.shape),
            _resident((1, d_pool)),
            _resident(w_up_pool.shape),
        ],
        out_specs=pl.BlockSpec((ts, d), lambda b, s: (b * ns + s, 0)),
        out_shape=jax.ShapeDtypeStruct((t, d), F32),
        scratch_shapes=[pltpu.VMEM((POOL_HALO + ts, d_pool), F32)],
        compiler_params=_cparams(("parallel", "arbitrary")),
        name="pool",
    )(proj, proj, pool_w, pool_scale, w_up_pool)


def _prep_kernel(q_ref, k_ref, v_ref, ba_ref, cw_ref, alog_ref, dtb_ref,
                 qn_ref, kn_ref, vv_ref, bx_ref, gx_ref, gt_ref, ext_ref):
    ts = q_ref.shape[0]
    dq = q_ref.shape[1]
    s = pl.program_id(1)

    @pl.when(s == 0)
    def _():
        ext_ref[0:CONV_HALO, :] = jnp.zeros((CONV_HALO, ext_ref.shape[1]), F32)

    @pl.when(s > 0)
    def _():
        ext_ref[0:CONV_HALO, :] = ext_ref[ts:ts + CONV_HALO, :]

    ext_ref[CONV_HALO:CONV_HALO + ts, 0:dq] = q_ref[...]
    ext_ref[CONV_HALO:CONV_HALO + ts, dq:2 * dq] = k_ref[...]
    ext_ref[CONV_HALO:CONV_HALO + ts, 2 * dq:3 * dq] = v_ref[...]

    def conv_silu(c0, width):
        cols = slice(c0, c0 + width)
        first = CONV_HALO - (CONV_K - 1)
        y = ext_ref[first:first + ts, cols] * cw_ref[0:1, cols]
        for tap in range(1, CONV_K):
            y = y + ext_ref[first + tap:first + tap + ts, cols] * cw_ref[tap:tap + 1, cols]
        return y * jax.nn.sigmoid(y)

    for h in range(DN_HEADS):
        hc = slice(h * DN_DK, (h + 1) * DN_DK)
        qh = conv_silu(h * DN_DK, DN_DK)
        qn_ref[:, hc] = qh * (lax.rsqrt(jnp.sum(qh * qh, axis=-1, keepdims=True) + EPS) * (DN_DK ** -0.5))
        kh = conv_silu(dq + h * DN_DK, DN_DK)
        kn_ref[:, hc] = kh * lax.rsqrt(jnp.sum(kh * kh, axis=-1, keepdims=True) + EPS)
        vv_ref[:, hc] = conv_silu(2 * dq + h * DN_DK, DN_DK)

    ba = ba_ref[...]
    beta = jax.nn.sigmoid(ba)
    xs = ba + dtb_ref[...]
    softplus = jnp.maximum(xs, 0.0) + jnp.log1p(jnp.exp(-jnp.abs(xs)))
    g = -jnp.exp(alog_ref[...]) * softplus
    ri = lax.broadcasted_iota(jnp.int32, (ts, ts), 0)
    ci = lax.broadcasted_iota(jnp.int32, (ts, ts), 1)
    tri = jnp.where((ri // CHUNK == ci // CHUNK) & (ci <= ri), 1.0, 0.0).astype(F32)
    gcum = jnp.dot(tri, g, preferred_element_type=F32, precision=lax.Precision.HIGHEST)
    gt_ref[0] = gcum.T[DN_HEADS:2 * DN_HEADS, :]
    for h in range(DN_HEADS):
        hc = slice(h * DN_DK, (h + 1) * DN_DK)
        bx_ref[:, hc] = jnp.broadcast_to(beta[:, h:h + 1], (ts, DN_DK))
        gx_ref[:, hc] = jnp.broadcast_to(gcum[:, DN_HEADS + h:DN_HEADS + h + 1], (ts, DN_DK))


def _prep(proj, ba, conv_w, alog_pad, dtb_pad, *, bsz, seq, ts):
    t = proj.shape[0]
    ns = seq // ts
    dq = DN_HEADS * DN_DK
    row = lambda b, s: b * ns + s
    blk = lambda col: pl.BlockSpec((ts, dq), lambda b, s: (row(b, s), col // dq))
    out_blk = pl.BlockSpec((ts, dq), lambda b, s: (row(b, s), 0))
    return pl.pallas_call(
        _prep_kernel,
        grid=(bsz, ns),
        in_specs=[
            blk(COL_Q), blk(COL_K), blk(COL_V),
            pl.BlockSpec((ts, LANES), lambda b, s: (row(b, s), 0)),
            _resident(conv_w.shape),
            _resident((1, LANES)),
            _resident((1, LANES)),
        ],
        out_specs=[out_blk, out_blk, out_blk, out_blk, out_blk,
                   pl.BlockSpec((1, DN_HEADS, ts), lambda b, s: (row(b, s), 0, 0))],
        out_shape=[jax.ShapeDtypeStruct((t, dq), F32)] * 5
                  + [jax.ShapeDtypeStruct((t // ts, DN_HEADS, ts), F32)],
        scratch_shapes=[pltpu.VMEM((CONV_HALO + ts, 3 * dq), F32)],
        compiler_params=_cparams(("parallel", "arbitrary")),
        name="prep",
    )(proj, proj, proj, ba, conv_w, alog_pad, dtb_pad)


def _dot_nt(a, b):
    return lax.dot_general(a, b, (((1,), (1,)), ((), ())), preferred_element_type=F32)


def _dot_tn(a, b):
    return lax.dot_general(a, b, (((0,), (0,)), ((), ())), preferred_element_type=F32)


GROUP = 256
PAIR = 2 * CHUNK


def _wy_kernel(q_ref, k_ref, v_ref, bx_ref, gx_ref, gt_ref,
               u_ref, w_ref, qg_ref, kd_ref, at_ref, egl_ref):
    ts = q_ref.shape[0]
    ii = lax.broadcasted_iota(jnp.int32, (GROUP, GROUP), 0)
    jj = lax.broadcasted_iota(jnp.int32, (GROUP, GROUP), 1)
    same = (ii // CHUNK) == (jj // CHUNK)
    incl = same & (ii >= jj)
    strict = same & (ii > jj)

    groups = [slice(g * GROUP, (g + 1) * GROUP) for g in range(ts // GROUP)]
    lows, sols = [], []
    for gi, rows in enumerate(groups):
        q = q_ref[rows, :]
        k = k_ref[rows, :]
        beta = bx_ref[rows, :]
        gc = gx_ref[rows, :]
        gr = gt_ref[0, :, rows]
        decay = jnp.where(incl, jnp.exp(jnp.concatenate([gc, gc], axis=1) - gr), 0.0)
        kb = k * beta
        a = _dot_nt(jnp.concatenate([kb, q], axis=0).astype(BF16), k.astype(BF16))
        lows.append(jnp.where(strict, a[:GROUP] * decay, 0.0).astype(BF16))
        attn = a[GROUP:] * decay
        for j in range(GROUP // PAIR):
            pr = slice(j * PAIR, (j + 1) * PAIR)
            at_ref[rows.start + j * PAIR:rows.start + (j + 1) * PAIR, :] = attn[pr, pr].astype(BF16)
        egc = jnp.exp(gc)
        qg_ref[rows, :] = (q * egc).astype(BF16)
        lasts = [gc[(c + 1) * CHUNK - 1:(c + 1) * CHUNK, :] for c in range(GROUP // CHUNK)]
        for c, g_last in enumerate(lasts):
            row = gi * (GROUP // CHUNK) + c
            egl_ref[row:row + 1, :] = jnp.exp(g_last)
        g_last_rows = jnp.concatenate([jnp.broadcast_to(gl, (CHUNK, DN_DK)) for gl in lasts], axis=0)
        kd_ref[rows, :] = (k * jnp.exp(g_last_rows - gc)).astype(BF16)
        sols.append(jnp.concatenate([v_ref[rows, :] * beta, kb * egc], axis=1))

    mm = lambda a16, b: jnp.dot(a16, b.astype(BF16), preferred_element_type=F32)
    sols = [s - mm(l, s) for l, s in zip(lows, sols)]
    pws = lows
    for _ in range(CHUNK.bit_length() - 2):
        pws = [mm(p, p).astype(BF16) for p in pws]
        sols = [s + mm(p, s) for p, s in zip(pws, sols)]
    for rows, s in zip(groups, sols):
        u_ref[rows, :] = s[:, :DN_DK]
        w_ref[rows, :] = s[:, DN_DK:].astype(BF16)


def _wy(qn, kn, vv, bx, gx, gt, *, ts):
    t = qn.shape[0]
    hblk = pl.BlockSpec((ts, DN_DK), lambda r, h: (r, h))
    wide = jax.ShapeDtypeStruct((t, DN_HEADS * DN_DK), BF16)
    return pl.pallas_call(
        _wy_kernel,
        grid=(t // ts, DN_HEADS),
        in_specs=[hblk, hblk, hblk, hblk, hblk,
                  pl.BlockSpec((1, 1, ts), lambda r, h: (r * DN_HEADS + h, 0, 0))],
        out_specs=[hblk, hblk, hblk, hblk, hblk,
                   pl.BlockSpec((ts // CHUNK, DN_DK), lambda r, h: (r, h))],
        out_shape=[jax.ShapeDtypeStruct((t, DN_HEADS * DN_DK), F32), wide, wide, wide, wide,
                   jax.ShapeDtypeStruct((t // CHUNK, DN_HEADS * DN_DK), F32)],
        compiler_params=_cparams(("parallel", "parallel")),
        name="wy",
    )(qn, kn, vv, bx, gx, gt.reshape(-1, 1, ts))


def _scan_kernel(u_ref, w_ref, qg_ref, kd_ref, at_ref, egl_ref, z_ref, dnw_ref, out_ref, state_ref):
    ts = u_ref.shape[0]

    @pl.when(pl.program_id(1) == 0)
    def _():
        state_ref[...] = jnp.zeros(state_ref.shape, F32)

    even_vn = [None] * DN_HEADS
    for c in range(ts // CHUNK):
        rows = slice(c * CHUNK, (c + 1) * CHUNK)
        for h in range(DN_HEADS):
            hc = slice(h * DN_DK, (h + 1) * DN_DK)
            state = state_ref[h]
            wq = jnp.concatenate([w_ref[rows, hc], qg_ref[rows, hc]], axis=0)
            r = jnp.dot(wq, state.astype(BF16), preferred_element_type=F32)
            vn16 = (u_ref[rows, hc] - r[:CHUNK]).astype(BF16)
            if c % 2 == 0:
                even_vn[h] = vn16
                pair = jnp.concatenate([vn16, jnp.zeros_like(vn16)], axis=0)
            else:
                pair = jnp.concatenate([even_vn[h], vn16], axis=0)
            o = r[CHUNK:] + jnp.dot(at_ref[rows, hc], pair, preferred_element_type=F32)
            state_ref[h] = state * egl_ref[c:c + 1, hc] + _dot_tn(kd_ref[rows, hc], vn16)
            on = o * lax.rsqrt(jnp.mean(o * o, axis=-1, keepdims=True) + EPS) * dnw_ref[...]
            z = z_ref[rows, hc]
            out_ref[rows, hc] = (on * (z * jax.nn.sigmoid(z))).astype(out_ref.dtype)


def _scan(u, w, qg, kd, at, egl, proj, dn_norm, *, bsz, seq, ts):
    t, dv = u.shape
    ns = seq // ts
    blk = pl.BlockSpec((ts, dv), lambda b, s: (b * ns + s, 0))
    return pl.pallas_call(
        _scan_kernel,
        grid=(bsz, ns),
        in_specs=[blk, blk, blk, blk, blk,
                  pl.BlockSpec((ts // CHUNK, dv), lambda b, s: (b * ns + s, 0)),
                  pl.BlockSpec((ts, dv), lambda b, s: (b * ns + s, COL_Z // dv)),
                  _resident((1, DN_DK))],
        out_specs=blk,
        out_shape=jax.ShapeDtypeStruct((t, dv), BF16),
        scratch_shapes=[pltpu.VMEM((DN_HEADS, DN_DK, DN_DK), F32)],
        compiler_params=_cparams(("parallel", "arbitrary")),
        name="scan",
    )(u, w, qg, kd, at, egl, proj, dn_norm)


ROUTE_E1, ROUTE_E2, ROUTE_W1, ROUTE_W2, ROUTE_R1, ROUTE_R2 = range(6)
ROUTER_LANE0 = N_GROUPS
NEG_BIG = -1e30


def _merge_kernel(ydn_ref, gd_ref, mp_ref, x_ref, wupdn_ref, wout_ref, gain_ref, wr_ref, br_ref,
                  h1_ref, n2_ref, route_ref, cnt_ref, carry_ref):
    tm = x_ref.shape[0]

    @pl.when(pl.program_id(0) == 0)
    def _():
        carry_ref[...] = jnp.zeros(carry_ref.shape, F32)

    md = jax.nn.sigmoid(gd_ref[...]) * jnp.dot(ydn_ref[...], wupdn_ref[...], preferred_element_type=F32)
    merged = mp_ref[...] + md
    h1 = x_ref[...] + jnp.dot(merged.astype(BF16), wout_ref[...], preferred_element_type=F32)
    h1_ref[...] = h1
    n2 = h1 * lax.rsqrt(jnp.mean(h1 * h1, axis=-1, keepdims=True) + EPS) * gain_ref[...]
    n2_ref[...] = n2

    hi = n2.astype(BF16)
    lo = (n2 - hi.astype(F32)).astype(BF16)
    r_hi = jnp.dot(hi, wr_ref[...], preferred_element_type=F32)
    r_lo = jnp.dot(lo, wr_ref[:, :LANES], preferred_element_type=F32)
    logits = r_hi[:, :LANES] + r_hi[:, LANES:] + r_lo + br_ref[...]

    lane = lax.broadcasted_iota(jnp.int32, (tm, LANES), 1)
    lane_f = lane.astype(F32)
    first_max = lambda v, m: jnp.min(jnp.where(v == m, lane_f, float(LANES)), axis=-1, keepdims=True)

    is_grp = lane < N_GROUPS
    lg = jnp.where(is_grp, logits, NEG_BIG)
    gmax = jnp.max(lg, axis=-1, keepdims=True)
    gi = first_max(lg, gmax)
    p_sel = 1.0 / jnp.sum(jnp.where(is_grp, jnp.exp(lg - gmax), 0.0), axis=-1, keepdims=True)

    e_lane = lane - ROUTER_LANE0
    in_grp = (e_lane >= 0) & (e_lane < N_EXPERTS) & ((e_lane // EXPERTS_PER_GROUP).astype(F32) == gi)
    le = jnp.where(in_grp, logits, NEG_BIG)
    v1 = jnp.max(le, axis=-1, keepdims=True)
    i1 = first_max(le, v1)
    le2 = jnp.where(lane_f == i1, NEG_BIG, le)
    v2 = jnp.max(le2, axis=-1, keepdims=True)
    i2 = first_max(le2, v2)
    s = jnp.exp(v2 - v1)
    w1 = p_sel / (1.0 + s)
    w2 = p_sel * s / (1.0 + s)
    e1 = i1 - float(ROUTER_LANE0)
    e2 = i2 - float(ROUTER_LANE0)

    oh1 = jnp.where(lane_f == e1, 1.0, 0.0)
    oh2 = jnp.where(lane_f == e2, 1.0, 0.0)
    both = oh1 + oh2
    ri = lax.broadcasted_iota(jnp.int32, (tm, tm), 0)
    ci = lax.broadcasted_iota(jnp.int32, (tm, tm), 1)
    tri = jnp.where(ri > ci, 1.0, 0.0).astype(BF16)
    prior = jnp.dot(tri, both.astype(BF16), preferred_element_type=F32) + carry_ref[...]
    r1 = jnp.sum(prior * oh1, axis=-1, keepdims=True)
    r2 = jnp.sum(prior * oh2, axis=-1, keepdims=True)
    carry = carry_ref[...] + jnp.sum(both, axis=0, keepdims=True)
    carry_ref[...] = carry
    cnt_ref[...] = carry

    route = jnp.zeros((tm, LANES), F32)
    for ln, val in ((ROUTE_E1, e1), (ROUTE_E2, e2), (ROUTE_W1, w1), (ROUTE_W2, w2),
                    (ROUTE_R1, r1), (ROUTE_R2, r2)):
        route = jnp.where(lane == ln, val, route)
    route_ref[...] = route


def _merge(y_dn, proj, mp, x2d, w_up_dn, w_out, gain, w_router, b_router, *, tm):
    t, d = x2d.shape
    dv = y_dn.shape[1]
    rows = lambda blk: pl.BlockSpec(blk, lambda i: (i, 0))
    return pl.pallas_call(
        _merge_kernel,
        grid=(t // tm,),
        in_specs=[
            rows((tm, dv)),
            pl.BlockSpec((tm, d), lambda i: (i, COL_GATE_DN // d)),
            rows((tm, d)), rows((tm, d)),
            _resident(w_up_dn.shape), _resident(w_out.shape), _resident((1, d)),
            _resident(w_router.shape), _resident((1, LANES)),
        ],
        out_specs=[rows((tm, d)), rows((tm, d)), rows((tm, LANES)),
                   pl.BlockSpec((1, LANES), lambda i: (0, 0))],
        out_shape=[jax.ShapeDtypeStruct((t, d), F32), jax.ShapeDtypeStruct((t, d), F32),
                   jax.ShapeDtypeStruct((t, LANES), F32), jax.ShapeDtypeStruct((1, LANES), F32)],
        scratch_shapes=[pltpu.VMEM((1, LANES), F32)],
        compiler_params=_cparams(("arbitrary",)),
        name="merge",
    )(y_dn, proj, mp, x2d, w_up_dn, w_out, gain, w_router, b_router)


def _dispatch_kernel(pos_ref, n2_ref, init_ref, xs_ref, sem, *, tb):
    del init_ref
    base = pl.program_id(0) * tb

    def body(j, _):
        for k in range(2):
            dst = pos_ref[2 * (base + j) + k]
            pltpu.make_async_copy(n2_ref.at[pl.ds(j, 1)], xs_ref.at[pl.ds(dst, 1)], sem).start()
        return 0
    lax.fori_loop(0, tb, body, 0)
    for _ in range(2):
        pltpu.make_async_copy(n2_ref, xs_ref.at[pl.ds(0, tb)], sem).wait()


def _dispatch(pos, n2, cap, *, tb):
    t, d = n2.shape
    return pl.pallas_call(
        functools.partial(_dispatch_kernel, tb=tb),
        grid_spec=pltpu.PrefetchScalarGridSpec(
            num_scalar_prefetch=1,
            grid=(t // tb,),
            in_specs=[pl.BlockSpec((tb, d), lambda i, pos: (i, 0)), pl.BlockSpec(memory_space=pl.ANY)],
            out_specs=pl.BlockSpec(memory_space=pl.ANY),
            scratch_shapes=[pltpu.SemaphoreType.DMA],
        ),
        out_shape=jax.ShapeDtypeStruct((cap, d), F32),
        input_output_aliases={2: 0},
        compiler_params=_cparams(("arbitrary",)),
        name="dispatch",
    )(pos, n2, jnp.zeros((cap, d), F32))


def _experts_kernel(be_ref, nu_ref, x_ref, wg_ref, wu_ref, wd_ref, y_ref):
    del be_ref

    @pl.when(pl.program_id(0) < nu_ref[0])
    def _():
        xb = x_ref[...].astype(BF16)
        g = jnp.dot(xb, wg_ref[0], preferred_element_type=F32)
        u = jnp.dot(xb, wu_ref[0], preferred_element_type=F32)
        hmid = (g * jax.nn.sigmoid(g)) * u
        y_ref[...] = jnp.dot(hmid.astype(BF16), wd_ref[0], preferred_element_type=F32)

    @pl.when(pl.program_id(0) >= nu_ref[0])
    def _():
        y_ref[...] = jnp.zeros(y_ref.shape, F32)


def _experts(blk_e, n_used, xs, w_gate, w_up, w_down, *, bm):
    cap, d = xs.shape
    de = w_gate.shape[2]
    live = lambda i, be, nu: (jnp.minimum(i, nu[0] - 1), 0)
    return pl.pallas_call(
        _experts_kernel,
        grid_spec=pltpu.PrefetchScalarGridSpec(
            num_scalar_prefetch=2,
            grid=(cap // bm,),
            in_specs=[
                pl.BlockSpec((bm, d), live),
                pl.BlockSpec((1, d, de), lambda i, be, nu: (be[i], 0, 0)),
                pl.BlockSpec((1, d, de), lambda i, be, nu: (be[i], 0, 0)),
                pl.BlockSpec((1, de, d), lambda i, be, nu: (be[i], 0, 0)),
            ],
            out_specs=pl.BlockSpec((bm, d), lambda i, be, nu: (i, 0)),
        ),
        out_shape=jax.ShapeDtypeStruct((cap, d), F32),
        compiler_params=_cparams(("arbitrary",)),
        name="experts",
    )(blk_e, n_used, xs, w_gate, w_up, w_down)


def _combine_kernel(pos_ref, h1_ref, route_ref, p_ref, ys_ref, wpg_ref, wpp_ref, gple_ref, gfin_ref,
                    out_ref, ybuf, sem, *, final_norm):
    tm = h1_ref.shape[0]
    i = pl.program_id(0)
    n = pl.num_programs(0)
    slot = i % 2

    def issue(blk, sl):
        def body(j, _):
            tok = blk * tm + j
            for k in range(2):
                pltpu.make_async_copy(ys_ref.at[pl.ds(pos_ref[2 * tok + k], 1)],
                                      ybuf.at[sl, pl.ds(k * tm + j, 1)], sem.at[sl]).start()
            return 0
        lax.fori_loop(0, tm, body, 0)

    @pl.when(i == 0)
    def _():
        issue(0, 0)

    @pl.when(i + 1 < n)
    def _():
        issue(i + 1, 1 - slot)

    pltpu.make_async_copy(ys_ref.at[pl.ds(0, 2 * tm)], ybuf.at[slot], sem.at[slot]).wait()

    route = route_ref[...]
    y_moe = (ybuf[slot, 0:tm, :] * route[:, ROUTE_W1:ROUTE_W1 + 1]
             + ybuf[slot, tm:2 * tm, :] * route[:, ROUTE_W2:ROUTE_W2 + 1])
    h2 = h1_ref[...] + y_moe
    n3 = h2 * lax.rsqrt(jnp.mean(h2 * h2, axis=-1, keepdims=True) + EPS) * gple_ref[...]
    gate = jax.nn.sigmoid(jnp.dot(n3.astype(BF16), wpg_ref[...], preferred_element_type=F32))
    h3 = h2 + gate * jnp.dot(p_ref[...].astype(BF16), wpp_ref[...], preferred_element_type=F32)
    if final_norm:
        h3 = h3 * lax.rsqrt(jnp.mean(h3 * h3, axis=-1, keepdims=True) + EPS) * gfin_ref[...]
    out_ref[...] = h3


def _combine(pos, h1, route, p2d, ys, w_ple_gate, w_ple_proj, g_ple, g_fin, *, tm, final_norm):
    t, d = h1.shape
    rows = lambda w: pl.BlockSpec((tm, w), lambda i, pos: (i, 0))
    res = lambda shape: pl.BlockSpec(shape, lambda i, pos: (0,) * len(shape), pipeline_mode=pl.Buffered(1))
    return pl.pallas_call(
        functools.partial(_combine_kernel, final_norm=final_norm),
        grid_spec=pltpu.PrefetchScalarGridSpec(
            num_scalar_prefetch=1,
            grid=(t // tm,),
            in_specs=[rows(d), rows(LANES), rows(p2d.shape[1]), pl.BlockSpec(memory_space=pl.ANY),
                      res(w_ple_gate.shape), res(w_ple_proj.shape), res((1, d)), res((1, d))],
            out_specs=rows(d),
            scratch_shapes=[pltpu.VMEM((2, 2 * tm, d), F32), pltpu.SemaphoreType.DMA((2,))],
        ),
        out_shape=jax.ShapeDtypeStruct((t, d), F32),
        compiler_params=_cparams(("arbitrary",)),
        name="combine",
    )(pos, h1, route, p2d, ys, w_ple_gate, w_ple_proj, g_ple, g_fin)


def _pick(n, pref):
    b = min(n, pref)
    while n % b:
        b -= 8
    return b


def _stage_mixers(x, prm, i):
    bsz, seq, d = x.shape
    t = bsz * seq
    x2d = x.reshape(t, d)
    w_in = prm["w_in"][i]
    d_pool = prm["w_up_pool"].shape[1]
    dq = DN_HEADS * DN_DK
    o_qkv = d_pool
    o_z = o_qkv + 3 * dq
    o_b = o_z + dq
    o_gp = o_b + 2 * DN_HEADS
    o_gd = o_gp + d
    w_main = jnp.concatenate(
        [w_in[:, o_gp:o_gp + d], w_in[:, o_gd:o_gd + d], w_in[:, :d_pool],
         w_in[:, o_qkv:o_qkv + 3 * dq], w_in[:, o_z:o_z + dq]], axis=1).astype(BF16)
    assert w_main.shape[1] == N_MAIN
    w_ba = jnp.pad(w_in[:, o_b:o_b + 2 * DN_HEADS], ((0, 0), (0, LANES - 2 * DN_HEADS))).astype(BF16)
    proj, ba = _inproj(x2d, prm["norm_mix"][i][None, :], w_main, w_ba,
                       tm=_pick(t, 1024), tn=1024)

    ts = _pick(seq, 512)
    mp = _pool(proj, prm["pool_w"][i].astype(BF16), prm["pool_scale"][i][None, :],
               prm["w_up_pool"][i].astype(BF16), bsz=bsz, seq=seq, ts=ts)

    lane_pad = lambda a: jnp.pad(a, (DN_HEADS, LANES - 2 * DN_HEADS))[None, :]
    qn, kn, vv, bx, gx, gt = _prep(proj, ba, prm["conv_w"][i], lane_pad(prm["a_log"][i]),
                                   lane_pad(prm["dt_bias"][i]), bsz=bsz, seq=seq, ts=ts)
    u, w, qg, kd, at, egl = _wy(qn, kn, vv, bx, gx, gt, ts=ts)
    y_dn = _scan(u, w, qg, kd, at, egl, proj, prm["dn_norm"][i][None, :], bsz=bsz, seq=seq, ts=ts)
    return dict(proj=proj, ba=ba, mp=mp, qn=qn, kn=kn, vv=vv, bx=bx, gx=gx, gt=gt, y_dn=y_dn)


MOE_BM = 256


def _router_weights(w_rg, b_rg, w_re, b_re):
    d = w_rg.shape[0]
    w = jnp.concatenate([w_rg, w_re, jnp.zeros((d, LANES - N_GROUPS - N_EXPERTS), F32)], axis=1)
    w_hi = w.astype(BF16)
    w_lo = (w - w_hi.astype(F32)).astype(BF16)
    b = jnp.concatenate([b_rg, b_re, jnp.zeros((LANES - N_GROUPS - N_EXPERTS,), F32)])[None, :]
    return jnp.concatenate([w_hi, w_lo], axis=1), b


def _stage_moe_plan(route, cnt, t):
    counts = cnt[0, :N_EXPERTS].astype(jnp.int32)
    padded = ((counts + MOE_BM - 1) // MOE_BM) * MOE_BM
    pends = jnp.cumsum(padded)
    pstarts = pends - padded
    eid = route[:, ROUTE_E1:ROUTE_E2 + 1].astype(jnp.int32)
    rank = route[:, ROUTE_R1:ROUTE_R2 + 1].astype(jnp.int32)
    pos = (pstarts[eid] + rank).reshape(2 * t)
    cap = 2 * t + N_EXPERTS * MOE_BM
    n_blocks = cap // MOE_BM
    n_used = (pends[-1] // MOE_BM).astype(jnp.int32)
    blk = jnp.minimum(jnp.arange(n_blocks, dtype=jnp.int32), n_used - 1)
    blk_e = jnp.sum((pends[None, :] <= (blk * MOE_BM)[:, None]).astype(jnp.int32), axis=1)
    blk_e = jnp.minimum(blk_e, N_EXPERTS - 1)
    return pos, blk_e, n_used.reshape(1), cap


def kernel(x, p, norm_mix, w_in, pool_w, pool_scale, conv_w, a_log, dt_bias, dn_norm, w_up_pool, w_up_dn, w_out, norm_moe, w_router_group, b_router_group, w_router_expert, b_router_expert, w_gate, w_up, w_down, norm_ple, w_ple_gate, w_ple_proj, norm_final):
    bsz, seq, d = x.shape
    t = bsz * seq
    depth = norm_mix.shape[0]
    prm = dict(norm_mix=norm_mix, w_in=w_in, pool_w=pool_w, pool_scale=pool_scale, conv_w=conv_w,
               a_log=a_log, dt_bias=dt_bias, dn_norm=dn_norm, w_up_pool=w_up_pool)
    h = x
    for i in range(depth):
        st = _stage_mixers(h, prm, i)
        w_router, b_router = _router_weights(w_router_group[i], b_router_group[i],
                                             w_router_expert[i], b_router_expert[i])
        tm = _pick(t, 256)
        h1, n2, route, cnt = _merge(st["y_dn"], st["proj"], st["mp"], h.reshape(t, d),
                                    w_up_dn[i].astype(BF16), w_out[i].astype(BF16), norm_moe[i][None, :],
                                    w_router, b_router, tm=tm)
        pos, blk_e, n_used, cap = _stage_moe_plan(route, cnt, t)
        xs = _dispatch(pos, n2, cap, tb=_pick(t, 512))
        ys = _experts(blk_e, n_used, xs, w_gate[i].astype(BF16), w_up[i].astype(BF16),
                      w_down[i].astype(BF16), bm=MOE_BM)
        out = _combine(pos, h1, route, p[i].reshape(t, -1), ys, w_ple_gate[i].astype(BF16),
                       w_ple_proj[i].astype(BF16), norm_ple[i][None, :], norm_final[None, :],
                       tm=tm, final_norm=(i == depth - 1))
        h = out.reshape(bsz, seq, d)
    return h
```

```python
import functools

import jax
import jax.numpy as jnp
from jax import lax
from jax.experimental import pallas as pl
from jax.experimental.pallas import tpu as pltpu

F32 = jnp.float32
BF16 = jnp.bfloat16

EPS = 1e-6
POOL_WINDOWS = (2, 4, 8, 16)
POOL_HALO = 16
CONV_K = 4
CONV_HALO = 8
DN_HEADS = 8
DN_DK = 128
CHUNK = 64
N_GROUPS = 4
EXPERTS_PER_GROUP = 8
N_EXPERTS = N_GROUPS * EXPERTS_PER_GROUP
LANES = 128
VMEM_LIMIT = 56 * 1024 * 1024

COL_GATE_POOL = 0
COL_GATE_DN = 2048
COL_POOL = 4096
COL_Q = 5120
COL_K = 6144
COL_V = 7168
COL_Z = 8192
N_MAIN = 9216


def _cparams(sem):
    return pltpu.CompilerParams(dimension_semantics=sem, vmem_limit_bytes=VMEM_LIMIT)


def _resident(shape):
    nd = len(shape)
    return pl.BlockSpec(shape, lambda *_: (0,) * nd, pipeline_mode=pl.Buffered(1))


def _inproj_kernel(x_ref, gain_ref, w_ref, wba_ref, out_ref, ba_ref, n1_ref, *, sub):
    tm = x_ref.shape[0]

    @pl.when(pl.program_id(1) == 0)
    def _():
        def body(r, _):
            rows = pl.ds(pl.multiple_of(r * sub, sub), sub)
            x = x_ref[rows, :]
            y = x * lax.rsqrt(jnp.mean(x * x, axis=-1, keepdims=True) + EPS) * gain_ref[...]
            n1_ref[rows, :] = y.astype(BF16)
            return 0
        lax.fori_loop(0, tm // sub, body, 0)
        ba_ref[...] = jnp.dot(n1_ref[...], wba_ref[...], preferred_element_type=F32)

    out_ref[...] = jnp.dot(n1_ref[...], w_ref[...], preferred_element_type=F32)


def _inproj(x2d, gain, w_main, w_ba, *, tm, tn):
    t, d = x2d.shape
    n = w_main.shape[1]
    return pl.pallas_call(
        functools.partial(_inproj_kernel, sub=min(tm, 256)),
        grid=(t // tm, n // tn),
        in_specs=[
            pl.BlockSpec((tm, d), lambda i, j: (i, 0)),
            _resident((1, d)),
            pl.BlockSpec((d, tn), lambda i, j: (0, j)),
            _resident((d, LANES)),
        ],
        out_specs=[
            pl.BlockSpec((tm, tn), lambda i, j: (i, j)),
            pl.BlockSpec((tm, LANES), lambda i, j: (i, 0)),
        ],
        out_shape=[jax.ShapeDtypeStruct((t, n), F32), jax.ShapeDtypeStruct((t, LANES), F32)],
        scratch_shapes=[pltpu.VMEM((tm, d), BF16)],
        compiler_params=_cparams(("parallel", "arbitrary")),
        name="inproj",
    )(x2d, gain, w_main, w_ba)


def _pool_kernel(u_ref, gp_ref, pw_ref, scale_ref, wup_ref, out_ref, ext_ref):
    ts = u_ref.shape[0]
    gw = pw_ref.shape[1]
    s = pl.program_id(1)

    @pl.when(s == 0)
    def _():
        ext_ref[0:POOL_HALO, :] = jnp.zeros((POOL_HALO, ext_ref.shape[1]), F32)

    @pl.when(s > 0)
    def _():
        ext_ref[0:POOL_HALO, :] = ext_ref[ts:ts + POOL_HALO, :]

    ext_ref[POOL_HALO:POOL_HALO + ts, :] = u_ref[...]

    t1 = (s * ts + 1 + lax.broadcasted_iota(jnp.int32, (ts, 1), 0)).astype(F32)
    acc = jnp.zeros(out_ref.shape, F32)
    for g, w in enumerate(POOL_WINDOWS):
        cols = slice(g * gw, (g + 1) * gw)
        cur = ext_ref[POOL_HALO:POOL_HALO + ts, cols]
        win = cur
        for j in range(1, w):
            win = win + ext_ref[POOL_HALO - j:POOL_HALO - j + ts, cols]
        d = win / jnp.minimum(t1, float(w)) - cur
        y = jnp.dot(d.astype(BF16), pw_ref[g], preferred_element_type=F32) * scale_ref[:, cols]
        acc = acc + jnp.dot(y.astype(BF16), wup_ref[cols, :], preferred_element_type=F32)
    out_ref[...] = jax.nn.sigmoid(gp_ref[...]) * acc


def _pool(proj, pool_w, pool_scale, w_up_pool, *, bsz, seq, ts):
    t = proj.shape[0]
    ns = seq // ts
    d_pool = w_up_pool.shape[0]
    d = w_up_pool.shape[1]
    return pl.pallas_call(
        _pool_kernel,
        grid=(bsz, ns),
        in_specs=[
            pl.BlockSpec((ts, d_pool), lambda b, s: (b * ns + s, COL_POOL // d_pool)),
            pl.BlockSpec((ts, d), lambda b, s: (b * ns + s, COL_GATE_POOL // d)),
            _resident(pool_w.shape),
            _resident((1, d_pool)),
            _resident(w_up_pool.shape),
        ],
        out_specs=pl.BlockSpec((ts, d), lambda b, s: (b * ns + s, 0)),
        out_shape=jax.ShapeDtypeStruct((t, d), F32),
        scratch_shapes=[pltpu.VMEM((POOL_HALO + ts, d_pool), F32)],
        compiler_params=_cparams(("parallel", "arbitrary")),
        name="pool",
    )(proj, proj, pool_w, pool_scale, w_up_pool)


def _prep_kernel(q_ref, k_ref, v_ref, ba_ref, cw_ref, alog_ref, dtb_ref,
                 qn_ref, kn_ref, vv_ref, bx_ref, gx_ref, gt_ref, ext_ref):
    ts = q_ref.shape[0]
    dq = q_ref.shape[1]
    s = pl.program_id(1)

    @pl.when(s == 0)
    def _():
        ext_ref[0:CONV_HALO, :] = jnp.zeros((CONV_HALO, ext_ref.shape[1]), F32)

    @pl.when(s > 0)
    def _():
        ext_ref[0:CONV_HALO, :] = ext_ref[ts:ts + CONV_HALO, :]

    ext_ref[CONV_HALO:CONV_HALO + ts, 0:dq] = q_ref[...]
    ext_ref[CONV_HALO:CONV_HALO + ts, dq:2 * dq] = k_ref[...]
    ext_ref[CONV_HALO:CONV_HALO + ts, 2 * dq:3 * dq] = v_ref[...]

    def conv_silu(c0, width):
        cols = slice(c0, c0 + width)
        first = CONV_HALO - (CONV_K - 1)
        y = ext_ref[first:first + ts, cols] * cw_ref[0:1, cols]
        for tap in range(1, CONV_K):
            y = y + ext_ref[first + tap:first + tap + ts, cols] * cw_ref[tap:tap + 1, cols]
        return y * jax.nn.sigmoid(y)

    for h in range(DN_HEADS):
        hc = slice(h * DN_DK, (h + 1) * DN_DK)
        qh = conv_silu(h * DN_DK, DN_DK)
        qn_ref[:, hc] = qh * (lax.rsqrt(jnp.sum(qh * qh, axis=-1, keepdims=True) + EPS) * (DN_DK ** -0.5))
        kh = conv_silu(dq + h * DN_DK, DN_DK)
        kn_ref[:, hc] = kh * lax.rsqrt(jnp.sum(kh * kh, axis=-1, keepdims=True) + EPS)
        vv_ref[:, hc] = conv_silu(2 * dq + h * DN_DK, DN_DK)

    ba = ba_ref[...]
    beta = jax.nn.sigmoid(ba)
    xs = ba + dtb_ref[...]
    softplus = jnp.maximum(xs, 0.0) + jnp.log1p(jnp.exp(-jnp.abs(xs)))
    g = -jnp.exp(alog_ref[...]) * softplus
    ri = lax.broadcasted_iota(jnp.int32, (ts, ts), 0)
    ci = lax.broadcasted_iota(jnp.int32, (ts, ts), 1)
    tri = jnp.where((ri // CHUNK == ci // CHUNK) & (ci <= ri), 1.0, 0.0).astype(F32)
    gcum = jnp.dot(tri, g, preferred_element_type=F32, precision=lax.Precision.HIGHEST)
    gt_ref[0] = gcum.T[DN_HEADS:2 * DN_HEADS, :]
    for h in range(DN_HEADS):
        hc = slice(h * DN_DK, (h + 1) * DN_DK)
        bx_ref[:, hc] = jnp.broadcast_to(beta[:, h:h + 1], (ts, DN_DK))
        gx_ref[:, hc] = jnp.broadcast_to(gcum[:, DN_HEADS + h:DN_HEADS + h + 1], (ts, DN_DK))


def _prep(proj, ba, conv_w, alog_pad, dtb_pad, *, bsz, seq, ts):
    t = proj.shape[0]
    ns = seq // ts
    dq = DN_HEADS * DN_DK
    row = lambda b, s: b * ns + s
    blk = lambda col: pl.BlockSpec((ts, dq), lambda b, s: (row(b, s), col // dq))
    out_blk = pl.BlockSpec((ts, dq), lambda b, s: (row(b, s), 0))
    return pl.pallas_call(
        _prep_kernel,
        grid=(bsz, ns),
        in_specs=[
            blk(COL_Q), blk(COL_K), blk(COL_V),
            pl.BlockSpec((ts, LANES), lambda b, s: (row(b, s), 0)),
            _resident(conv_w.shape),
            _resident((1, LANES)),
            _resident((1, LANES)),
        ],
        out_specs=[out_blk, out_blk, out_blk, out_blk, out_blk,
                   pl.BlockSpec((1, DN_HEADS, ts), lambda b, s: (row(b, s), 0, 0))],
        out_shape=[jax.ShapeDtypeStruct((t, dq), F32)] * 5
                  + [jax.ShapeDtypeStruct((t // ts, DN_HEADS, ts), F32)],
        scratch_shapes=[pltpu.VMEM((CONV_HALO + ts, 3 * dq), F32)],
        compiler_params=_cparams(("parallel", "arbitrary")),
        name="prep",
    )(proj, proj, proj, ba, conv_w, alog_pad, dtb_pad)


def _dot_nt(a, b):
    return lax.dot_general(a, b, (((1,), (1,)), ((), ())), preferred_element_type=F32)


def _dot_tn(a, b):
    return lax.dot_general(a, b, (((0,), (0,)), ((), ())), preferred_element_type=F32)


GROUP = 256
PAIR = 2 * CHUNK


def _wy_kernel(q_ref, k_ref, v_ref, bx_ref, gx_ref, gt_ref,
               u_ref, w_ref, qg_ref, kd_ref, at_ref, egl_ref):
    ts = q_ref.shape[0]
    ii = lax.broadcasted_iota(jnp.int32, (GROUP, GROUP), 0)
    jj = lax.broadcasted_iota(jnp.int32, (GROUP, GROUP), 1)
    same = (ii // CHUNK) == (jj // CHUNK)
    incl = same & (ii >= jj)
    strict = same & (ii > jj)

    groups = [slice(g * GROUP, (g + 1) * GROUP) for g in range(ts // GROUP)]
    lows, sols = [], []
    for gi, rows in enumerate(groups):
        q = q_ref[rows, :]
        k = k_ref[rows, :]
        beta = bx_ref[rows, :]
        gc = gx_ref[rows, :]
        gr = gt_ref[0, :, rows]
        decay = jnp.where(incl, jnp.exp(jnp.concatenate([gc, gc], axis=1) - gr), 0.0)
        kb = k * beta
        a = _dot_nt(jnp.concatenate([kb, q], axis=0).astype(BF16), k.astype(BF16))
        lows.append(jnp.where(strict, a[:GROUP] * decay, 0.0).astype(BF16))
        attn = a[GROUP:] * decay
        for j in range(GROUP // PAIR):
            pr = slice(j * PAIR, (j + 1) * PAIR)
            at_ref[rows.start + j * PAIR:rows.start + (j + 1) * PAIR, :] = attn[pr, pr].astype(BF16)
        egc = jnp.exp(gc)
        qg_ref[rows, :] = (q * egc).astype(BF16)
        lasts = [gc[(c + 1) * CHUNK - 1:(c + 1) * CHUNK, :] for c in range(GROUP // CHUNK)]
        for c, g_last in enumerate(lasts):
            row = gi * (GROUP // CHUNK) + c
            egl_ref[row:row + 1, :] = jnp.exp(g_last)
        g_last_rows = jnp.concatenate([jnp.broadcast_to(gl, (CHUNK, DN_DK)) for gl in lasts], axis=0)
        kd_ref[rows, :] = (k * jnp.exp(g_last_rows - gc)).astype(BF16)
        sols.append(jnp.concatenate([v_ref[rows, :] * beta, kb * egc], axis=1))

    mm = lambda a16, b: jnp.dot(a16, b.astype(BF16), preferred_element_type=F32)
    sols = [s - mm(l, s) for l, s in zip(lows, sols)]
    pws = lows
    for _ in range(CHUNK.bit_length() - 2):
        pws = [mm(p, p).astype(BF16) for p in pws]
        sols = [s + mm(p, s) for p, s in zip(pws, sols)]
    for rows, s in zip(groups, sols):
        u_ref[rows, :] = s[:, :DN_DK]
        w_ref[rows, :] = s[:, DN_DK:].astype(BF16)


def _wy(qn, kn, vv, bx, gx, gt, *, ts):
    t = qn.shape[0]
    hblk = pl.BlockSpec((ts, DN_DK), lambda r, h: (r, h))
    wide = jax.ShapeDtypeStruct((t, DN_HEADS * DN_DK), BF16)
    return pl.pallas_call(
        _wy_kernel,
        grid=(t // ts, DN_HEADS),
        in_specs=[hblk, hblk, hblk, hblk, hblk,
                  pl.BlockSpec((1, 1, ts), lambda r, h: (r * DN_HEADS + h, 0, 0))],
        out_specs=[hblk, hblk, hblk, hblk, hblk,
                   pl.BlockSpec((ts // CHUNK, DN_DK), lambda r, h: (r, h))],
        out_shape=[jax.ShapeDtypeStruct((t, DN_HEADS * DN_DK), F32), wide, wide, wide, wide,
                   jax.ShapeDtypeStruct((t // CHUNK, DN_HEADS * DN_DK), F32)],
        compiler_params=_cparams(("parallel", "parallel")),
        name="wy",
    )(qn, kn, vv, bx, gx, gt.reshape(-1, 1, ts))


def _scan_kernel(u_ref, w_ref, qg_ref, kd_ref, at_ref, egl_ref, z_ref, dnw_ref, out_ref, state_ref):
    ts = u_ref.shape[0]

    @pl.when(pl.program_id(1) == 0)
    def _():
        state_ref[...] = jnp.zeros(state_ref.shape, F32)

    even_vn = [None] * DN_HEADS
    for c in range(ts // CHUNK):
        rows = slice(c * CHUNK, (c + 1) * CHUNK)
        for h in range(DN_HEADS):
            hc = slice(h * DN_DK, (h + 1) * DN_DK)
            state = state_ref[h]
            wq = jnp.concatenate([w_ref[rows, hc], qg_ref[rows, hc]], axis=0)
            r = jnp.dot(wq, state.astype(BF16), preferred_element_type=F32)
            vn16 = (u_ref[rows, hc] - r[:CHUNK]).astype(BF16)
            if c % 2 == 0:
                even_vn[h] = vn16
                pair = jnp.concatenate([vn16, jnp.zeros_like(vn16)], axis=0)
            else:
                pair = jnp.concatenate([even_vn[h], vn16], axis=0)
            o = r[CHUNK:] + jnp.dot(at_ref[rows, hc], pair, preferred_element_type=F32)
            state_ref[h] = state * egl_ref[c:c + 1, hc] + _dot_tn(kd_ref[rows, hc], vn16)
            on = o * lax.rsqrt(jnp.mean(o * o, axis=-1, keepdims=True) + EPS) * dnw_ref[...]
            z = z_ref[rows, hc]
            out_ref[rows, hc] = (on * (z * jax.nn.sigmoid(z))).astype(out_ref.dtype)


def _scan(u, w, qg, kd, at, egl, proj, dn_norm, *, bsz, seq, ts):
    t, dv = u.shape
    ns = seq // ts
    blk = pl.BlockSpec((ts, dv), lambda b, s: (b * ns + s, 0))
    return pl.pallas_call(
        _scan_kernel,
        grid=(bsz, ns),
        in_specs=[blk, blk, blk, blk, blk,
                  pl.BlockSpec((ts // CHUNK, dv), lambda b, s: (b * ns + s, 0)),
                  pl.BlockSpec((ts, dv), lambda b, s: (b * ns + s, COL_Z // dv)),
                  _resident((1, DN_DK))],
        out_specs=blk,
        out_shape=jax.ShapeDtypeStruct((t, dv), BF16),
        scratch_shapes=[pltpu.VMEM((DN_HEADS, DN_DK, DN_DK), F32)],
        compiler_params=_cparams(("parallel", "arbitrary")),
        name="scan",
    )(u, w, qg, kd, at, egl, proj, dn_norm)


ROUTE_E1, ROUTE_E2, ROUTE_W1, ROUTE_W2, ROUTE_R1, ROUTE_R2 = range(6)
ROUTER_LANE0 = N_GROUPS
NEG_BIG = -1e30


def _merge_kernel(ydn_ref, gd_ref, mp_ref, x_ref, wupdn_ref, wout_ref, gain_ref, wr_ref, br_ref,
                  h1_ref, n2_ref, route_ref, cnt_ref, carry_ref):
    tm = x_ref.shape[0]

    @pl.when(pl.program_id(0) == 0)
    def _():
        carry_ref[...] = jnp.zeros(carry_ref.shape, F32)

    md = jax.nn.sigmoid(gd_ref[...]) * jnp.dot(ydn_ref[...], wupdn_ref[...], preferred_element_type=F32)
    merged = mp_ref[...] + md
    h1 = x_ref[...] + jnp.dot(merged.astype(BF16), wout_ref[...], preferred_element_type=F32)
    h1_ref[...] = h1
    n2 = h1 * lax.rsqrt(jnp.mean(h1 * h1, axis=-1, keepdims=True) + EPS) * gain_ref[...]
    n2_ref[...] = n2

    hi = n2.astype(BF16)
    lo = (n2 - hi.astype(F32)).astype(BF16)
    r_hi = jnp.dot(hi, wr_ref[...], preferred_element_type=F32)
    r_lo = jnp.dot(lo, wr_ref[:, :LANES], preferred_element_type=F32)
    logits = r_hi[:, :LANES] + r_hi[:, LANES:] + r_lo + br_ref[...]

    lane = lax.broadcasted_iota(jnp.int32, (tm, LANES), 1)
    lane_f = lane.astype(F32)
    first_max = lambda v, m: jnp.min(jnp.where(v == m, lane_f, float(LANES)), axis=-1, keepdims=True)

    is_grp = lane < N_GROUPS
    lg = jnp.where(is_grp, logits, NEG_BIG)
    gmax = jnp.max(lg, axis=-1, keepdims=True)
    gi = first_max(lg, gmax)
    p_sel = 1.0 / jnp.sum(jnp.where(is_grp, jnp.exp(lg - gmax), 0.0), axis=-1, keepdims=True)

    e_lane = lane - ROUTER_LANE0
    in_grp = (e_lane >= 0) & (e_lane < N_EXPERTS) & ((e_lane // EXPERTS_PER_GROUP).astype(F32) == gi)
    le = jnp.where(in_grp, logits, NEG_BIG)
    v1 = jnp.max(le, axis=-1, keepdims=True)
    i1 = first_max(le, v1)
    le2 = jnp.where(lane_f == i1, NEG_BIG, le)
    v2 = jnp.max(le2, axis=-1, keepdims=True)
    i2 = first_max(le2, v2)
    s = jnp.exp(v2 - v1)
    w1 = p_sel / (1.0 + s)
    w2 = p_sel * s / (1.0 + s)
    e1 = i1 - float(ROUTER_LANE0)
    e2 = i2 - float(ROUTER_LANE0)

    oh1 = jnp.where(lane_f == e1, 1.0, 0.0)
    oh2 = jnp.where(lane_f == e2, 1.0, 0.0)
    both = oh1 + oh2
    ri = lax.broadcasted_iota(jnp.int32, (tm, tm), 0)
    ci = lax.broadcasted_iota(jnp.int32, (tm, tm), 1)
    tri = jnp.where(ri > ci, 1.0, 0.0).astype(BF16)
    prior = jnp.dot(tri, both.astype(BF16), preferred_element_type=F32) + carry_ref[...]
    r1 = jnp.sum(prior * oh1, axis=-1, keepdims=True)
    r2 = jnp.sum(prior * oh2, axis=-1, keepdims=True)
    carry = carry_ref[...] + jnp.sum(both, axis=0, keepdims=True)
    carry_ref[...] = carry
    cnt_ref[...] = carry

    route = jnp.zeros((tm, LANES), F32)
    for ln, val in ((ROUTE_E1, e1), (ROUTE_E2, e2), (ROUTE_W1, w1), (ROUTE_W2, w2),
                    (ROUTE_R1, r1), (ROUTE_R2, r2)):
        route = jnp.where(lane == ln, val, route)
    route_ref[...] = route


def _merge(y_dn, proj, mp, x2d, w_up_dn, w_out, gain, w_router, b_router, *, tm):
    t, d = x2d.shape
    dv = y_dn.shape[1]
    rows = lambda blk: pl.BlockSpec(blk, lambda i: (i, 0))
    return pl.pallas_call(
        _merge_kernel,
        grid=(t // tm,),
        in_specs=[
            rows((tm, dv)),
            pl.BlockSpec((tm, d), lambda i: (i, COL_GATE_DN // d)),
            rows((tm, d)), rows((tm, d)),
            _resident(w_up_dn.shape), _resident(w_out.shape), _resident((1, d)),
            _resident(w_router.shape), _resident((1, LANES)),
        ],
        out_specs=[rows((tm, d)), rows((tm, d)), rows((tm, LANES)),
                   pl.BlockSpec((1, LANES), lambda i: (0, 0))],
        out_shape=[jax.ShapeDtypeStruct((t, d), F32), jax.ShapeDtypeStruct((t, d), F32),
                   jax.ShapeDtypeStruct((t, LANES), F32), jax.ShapeDtypeStruct((1, LANES), F32)],
        scratch_shapes=[pltpu.VMEM((1, LANES), F32)],
        compiler_params=_cparams(("arbitrary",)),
        name="merge",
    )(y_dn, proj, mp, x2d, w_up_dn, w_out, gain, w_router, b_router)


def _experts_kernel(be_ref, nu_ref, tok_ref, n2_ref, wg_ref, wu_ref, wd_ref, y_ref, xbuf0, xbuf1, sem):
    del be_ref
    bm = xbuf0.shape[0]
    i = pl.program_id(0)
    nu = nu_ref[0]

    def issue(blk, buf, slot):
        for j in range(bm):
            tok = tok_ref[blk * bm + j]
            pltpu.make_async_copy(n2_ref.at[pl.ds(tok, 1)], buf.at[pl.ds(j, 1)], sem.at[slot]).start()

    def drain(buf, slot):
        pltpu.make_async_copy(n2_ref.at[pl.ds(0, bm)], buf, sem.at[slot]).wait()

    def live_step(cur, nxt, slot):
        drain(cur, slot)
        issue(i + 1, nxt, 1 - slot)
        xb = cur[...].astype(BF16)
        g = jnp.dot(xb, wg_ref[0], preferred_element_type=F32)
        u = jnp.dot(xb, wu_ref[0], preferred_element_type=F32)
        hmid = (g * jax.nn.sigmoid(g)) * u
        y_ref[...] = jnp.dot(hmid.astype(BF16), wd_ref[0], preferred_element_type=F32)

    @pl.when(i == 0)
    def _():
        issue(0, xbuf0, 0)

    @pl.when((i < nu) & (i % 2 == 0))
    def _():
        live_step(xbuf0, xbuf1, 0)

    @pl.when((i < nu) & (i % 2 == 1))
    def _():
        live_step(xbuf1, xbuf0, 1)

    @pl.when(i >= nu)
    def _():
        y_ref[...] = jnp.zeros(y_ref.shape, F32)

    @pl.when((i == nu) & (i % 2 == 0))
    def _():
        drain(xbuf0, 0)

    @pl.when((i == nu) & (i % 2 == 1))
    def _():
        drain(xbuf1, 1)


def _experts(blk_e, n_used, tok_pad, n2, w_gate, w_up, w_down, *, bm):
    d = n2.shape[1]
    cap = tok_pad.shape[0]
    de = w_gate.shape[2]
    wspec = lambda shape: pl.BlockSpec(shape, lambda i, be, nu, tok: (be[i], 0, 0))
    return pl.pallas_call(
        _experts_kernel,
        grid_spec=pltpu.PrefetchScalarGridSpec(
            num_scalar_prefetch=3,
            grid=(cap // bm,),
            in_specs=[pl.BlockSpec(memory_space=pl.ANY),
                      wspec((1, d, de)), wspec((1, d, de)), wspec((1, de, d))],
            out_specs=pl.BlockSpec((bm, d), lambda i, be, nu, tok: (i, 0)),
            scratch_shapes=[pltpu.VMEM((bm, d), F32), pltpu.VMEM((bm, d), F32),
                            pltpu.SemaphoreType.DMA((2,))],
        ),
        out_shape=jax.ShapeDtypeStruct((cap, d), F32),
        compiler_params=_cparams(("arbitrary",)),
        name="experts",
    )(blk_e, n_used, tok_pad, n2, w_gate, w_up, w_down)


def _combine_kernel(pos_ref, h1_ref, route_ref, p_ref, ys_ref, wpg_ref, wpp_ref, gple_ref, gfin_ref,
                    out_ref, ybuf0, ybuf1, sem, *, final_norm, n_steps):
    tm = h1_ref.shape[0]
    i = pl.program_id(0)

    def issue(blk, buf, slot):
        for j in range(tm):
            for k in range(2):
                src = pos_ref[2 * (blk * tm + j) + k]
                pltpu.make_async_copy(ys_ref.at[pl.ds(src, 1)], buf.at[pl.ds(k * tm + j, 1)],
                                      sem.at[slot]).start()

    def step(cur, nxt, slot):
        pltpu.make_async_copy(ys_ref.at[pl.ds(0, 2 * tm)], cur, sem.at[slot]).wait()
        if nxt is not None:
            issue(i + 1, nxt, 1 - slot)
        route = route_ref[...]
        y_moe = (cur[0:tm, :] * route[:, ROUTE_W1:ROUTE_W1 + 1]
                 + cur[tm:2 * tm, :] * route[:, ROUTE_W2:ROUTE_W2 + 1])
        h2 = h1_ref[...] + y_moe
        n3 = h2 * lax.rsqrt(jnp.mean(h2 * h2, axis=-1, keepdims=True) + EPS) * gple_ref[...]
        gate = jax.nn.sigmoid(jnp.dot(n3.astype(BF16), wpg_ref[...], preferred_element_type=F32))
        h3 = h2 + gate * jnp.dot(p_ref[...].astype(BF16), wpp_ref[...], preferred_element_type=F32)
        if final_norm:
            h3 = h3 * lax.rsqrt(jnp.mean(h3 * h3, axis=-1, keepdims=True) + EPS) * gfin_ref[...]
        out_ref[...] = h3

    @pl.when(i == 0)
    def _():
        issue(0, ybuf0, 0)

    last = n_steps - 1
    bufs = (ybuf0, ybuf1)

    @pl.when((i < last) & (i % 2 == 0))
    def _():
        step(ybuf0, ybuf1, 0)

    @pl.when((i < last) & (i % 2 == 1))
    def _():
        step(ybuf1, ybuf0, 1)

    @pl.when(i == last)
    def _():
        step(bufs[last % 2], None, last % 2)


def _combine(pos, h1, route, p2d, ys, w_ple_gate, w_ple_proj, g_ple, g_fin, *, tm, final_norm):
    t, d = h1.shape
    rows = lambda w: pl.BlockSpec((tm, w), lambda i, pos: (i, 0))
    res = lambda shape: pl.BlockSpec(shape, lambda i, pos: (0,) * len(shape), pipeline_mode=pl.Buffered(1))
    return pl.pallas_call(
        functools.partial(_combine_kernel, final_norm=final_norm, n_steps=t // tm),
        grid_spec=pltpu.PrefetchScalarGridSpec(
            num_scalar_prefetch=1,
            grid=(t // tm,),
            in_specs=[rows(d), rows(LANES), rows(p2d.shape[1]), pl.BlockSpec(memory_space=pl.ANY),
                      res(w_ple_gate.shape), res(w_ple_proj.shape), res((1, d)), res((1, d))],
            out_specs=rows(d),
            scratch_shapes=[pltpu.VMEM((2 * tm, d), F32), pltpu.VMEM((2 * tm, d), F32),
                            pltpu.SemaphoreType.DMA((2,))],
        ),
        out_shape=jax.ShapeDtypeStruct((t, d), F32),
        compiler_params=_cparams(("arbitrary",)),
        name="combine",
    )(pos, h1, route, p2d, ys, w_ple_gate, w_ple_proj, g_ple, g_fin)


def _pick(n, pref):
    b = min(n, pref)
    while n % b:
        b -= 8
    return b


def _stage_mixers(x, prm, i):
    bsz, seq, d = x.shape
    t = bsz * seq
    x2d = x.reshape(t, d)
    w_in = prm["w_in"][i]
    d_pool = prm["w_up_pool"].shape[1]
    dq = DN_HEADS * DN_DK
    o_qkv = d_pool
    o_z = o_qkv + 3 * dq
    o_b = o_z + dq
    o_gp = o_b + 2 * DN_HEADS
    o_gd = o_gp + d
    w_main = jnp.concatenate(
        [w_in[:, o_gp:o_gp + d], w_in[:, o_gd:o_gd + d], w_in[:, :d_pool],
         w_in[:, o_qkv:o_qkv + 3 * dq], w_in[:, o_z:o_z + dq]], axis=1).astype(BF16)
    assert w_main.shape[1] == N_MAIN
    w_ba = jnp.pad(w_in[:, o_b:o_b + 2 * DN_HEADS], ((0, 0), (0, LANES - 2 * DN_HEADS))).astype(BF16)
    proj, ba = _inproj(x2d, prm["norm_mix"][i][None, :], w_main, w_ba,
                       tm=_pick(t, 1024), tn=1024)

    ts = _pick(seq, 512)
    mp = _pool(proj, prm["pool_w"][i].astype(BF16), prm["pool_scale"][i][None, :],
               prm["w_up_pool"][i].astype(BF16), bsz=bsz, seq=seq, ts=ts)

    lane_pad = lambda a: jnp.pad(a, (DN_HEADS, LANES - 2 * DN_HEADS))[None, :]
    qn, kn, vv, bx, gx, gt = _prep(proj, ba, prm["conv_w"][i], lane_pad(prm["a_log"][i]),
                                   lane_pad(prm["dt_bias"][i]), bsz=bsz, seq=seq, ts=ts)
    u, w, qg, kd, at, egl = _wy(qn, kn, vv, bx, gx, gt, ts=ts)
    y_dn = _scan(u, w, qg, kd, at, egl, proj, prm["dn_norm"][i][None, :], bsz=bsz, seq=seq, ts=ts)
    return dict(proj=proj, ba=ba, mp=mp, qn=qn, kn=kn, vv=vv, bx=bx, gx=gx, gt=gt, y_dn=y_dn)


MOE_BM = 256


def _router_weights(w_rg, b_rg, w_re, b_re):
    d = w_rg.shape[0]
    w = jnp.concatenate([w_rg, w_re, jnp.zeros((d, LANES - N_GROUPS - N_EXPERTS), F32)], axis=1)
    w_hi = w.astype(BF16)
    w_lo = (w - w_hi.astype(F32)).astype(BF16)
    b = jnp.concatenate([b_rg, b_re, jnp.zeros((LANES - N_GROUPS - N_EXPERTS,), F32)])[None, :]
    return jnp.concatenate([w_hi, w_lo], axis=1), b


def _stage_moe_plan(route, cnt, t):
    counts = cnt[0, :N_EXPERTS].astype(jnp.int32)
    padded = ((counts + MOE_BM - 1) // MOE_BM) * MOE_BM
    pends = jnp.cumsum(padded)
    pstarts = pends - padded
    eid = route[:, ROUTE_E1:ROUTE_E2 + 1].astype(jnp.int32)
    rank = route[:, ROUTE_R1:ROUTE_R2 + 1].astype(jnp.int32)
    pos = (pstarts[eid] + rank).reshape(2 * t)
    cap = 2 * t + N_EXPERTS * MOE_BM
    n_blocks = cap // MOE_BM
    n_used = (pends[-1] // MOE_BM).astype(jnp.int32)
    blk = jnp.minimum(jnp.arange(n_blocks, dtype=jnp.int32), n_used - 1)
    blk_e = jnp.sum((pends[None, :] <= (blk * MOE_BM)[:, None]).astype(jnp.int32), axis=1)
    blk_e = jnp.minimum(blk_e, N_EXPERTS - 1)
    tok_pad = jnp.zeros((cap,), jnp.int32).at[pos].set(jnp.arange(2 * t, dtype=jnp.int32) // 2)
    return pos, blk_e, n_used.reshape(1), tok_pad


def kernel(x, p, norm_mix, w_in, pool_w, pool_scale, conv_w, a_log, dt_bias, dn_norm, w_up_pool, w_up_dn, w_out, norm_moe, w_router_group, b_router_group, w_router_expert, b_router_expert, w_gate, w_up, w_down, norm_ple, w_ple_gate, w_ple_proj, norm_final):
    bsz, seq, d = x.shape
    t = bsz * seq
    depth = norm_mix.shape[0]
    prm = dict(norm_mix=norm_mix, w_in=w_in, pool_w=pool_w, pool_scale=pool_scale, conv_w=conv_w,
               a_log=a_log, dt_bias=dt_bias, dn_norm=dn_norm, w_up_pool=w_up_pool)
    h = x
    for i in range(depth):
        st = _stage_mixers(h, prm, i)
        w_router, b_router = _router_weights(w_router_group[i], b_router_group[i],
                                             w_router_expert[i], b_router_expert[i])
        tm = _pick(t, 256)
        h1, n2, route, cnt = _merge(st["y_dn"], st["proj"], st["mp"], h.reshape(t, d),
                                    w_up_dn[i].astype(BF16), w_out[i].astype(BF16), norm_moe[i][None, :],
                                    w_router, b_router, tm=tm)
        pos, blk_e, n_used, tok_pad = _stage_moe_plan(route, cnt, t)
        ys = _experts(blk_e, n_used, tok_pad, n2, w_gate[i].astype(BF16), w_up[i].astype(BF16),
                      w_down[i].astype(BF16), bm=MOE_BM)
        out = _combine(pos, h1, route, p[i].reshape(t, -1), ys, w_ple_gate[i].astype(BF16),
                       w_ple_proj[i].astype(BF16), norm_ple[i][None, :], norm_final[None, :],
                       tm=tm, final_norm=(i == depth - 1))
        h = out.reshape(bsz, seq, d)
    return h
```

```python
import functools

import jax
import jax.numpy as jnp
from jax import lax
from jax.experimental import pallas as pl
from jax.experimental.pallas import tpu as pltpu

F32 = jnp.float32
BF16 = jnp.bfloat16

EPS = 1e-6
POOL_WINDOWS = (2, 4, 8, 16)
POOL_HALO = 16
CONV_K = 4
CONV_HALO = 8
DN_HEADS = 8
DN_DK = 128
CHUNK = 64
N_GROUPS = 4
EXPERTS_PER_GROUP = 8
N_EXPERTS = N_GROUPS * EXPERTS_PER_GROUP
LANES = 128
VMEM_LIMIT = 56 * 1024 * 1024

COL_GATE_POOL = 0
COL_GATE_DN = 2048
COL_POOL = 4096
COL_Q = 5120
COL_K = 6144
COL_V = 7168
COL_Z = 8192
N_MAIN = 9216


def _cparams(sem):
    return pltpu.CompilerParams(dimension_semantics=sem, vmem_limit_bytes=VMEM_LIMIT)


def _resident(shape):
    nd = len(shape)
    return pl.BlockSpec(shape, lambda *_: (0,) * nd, pipeline_mode=pl.Buffered(1))


def _inproj_kernel(x_ref, gain_ref, w_ref, wba_ref, out_ref, ba_ref, n1_ref, *, sub):
    tm = x_ref.shape[0]

    @pl.when(pl.program_id(1) == 0)
    def _():
        def body(r, _):
            rows = pl.ds(pl.multiple_of(r * sub, sub), sub)
            x = x_ref[rows, :]
            y = x * lax.rsqrt(jnp.mean(x * x, axis=-1, keepdims=True) + EPS) * gain_ref[...]
            n1_ref[rows, :] = y.astype(BF16)
            return 0
        lax.fori_loop(0, tm // sub, body, 0)
        ba_ref[...] = jnp.dot(n1_ref[...], wba_ref[...], preferred_element_type=F32)

    out_ref[...] = jnp.dot(n1_ref[...], w_ref[...], preferred_element_type=F32)


def _inproj(x2d, gain, w_main, w_ba, *, tm, tn):
    t, d = x2d.shape
    n = w_main.shape[1]
    return pl.pallas_call(
        functools.partial(_inproj_kernel, sub=min(tm, 256)),
        grid=(t // tm, n // tn),
        in_specs=[
            pl.BlockSpec((tm, d), lambda i, j: (i, 0)),
            _resident((1, d)),
            pl.BlockSpec((d, tn), lambda i, j: (0, j)),
            _resident((d, LANES)),
        ],
        out_specs=[
            pl.BlockSpec((tm, tn), lambda i, j: (i, j)),
            pl.BlockSpec((tm, LANES), lambda i, j: (i, 0)),
        ],
        out_shape=[jax.ShapeDtypeStruct((t, n), F32), jax.ShapeDtypeStruct((t, LANES), F32)],
        scratch_shapes=[pltpu.VMEM((tm, d), BF16)],
        compiler_params=_cparams(("parallel", "arbitrary")),
        name="inproj",
    )(x2d, gain, w_main, w_ba)


def _pool_kernel(u_ref, gp_ref, pw_ref, scale_ref, wup_ref, out_ref, ext_ref):
    ts = u_ref.shape[0]
    gw = pw_ref.shape[1]
    s = pl.program_id(1)

    @pl.when(s == 0)
    def _():
        ext_ref[0:POOL_HALO, :] = jnp.zeros((POOL_HALO, ext_ref.shape[1]), F32)

    @pl.when(s > 0)
    def _():
        ext_ref[0:POOL_HALO, :] = ext_ref[ts:ts + POOL_HALO, :]

    ext_ref[POOL_HALO:POOL_HALO + ts, :] = u_ref[...]

    t1 = (s * ts + 1 + lax.broadcasted_iota(jnp.int32, (ts, 1), 0)).astype(F32)
    acc = jnp.zeros(out_ref.shape, F32)
    for g, w in enumerate(POOL_WINDOWS):
        cols = slice(g * gw, (g + 1) * gw)
        cur = ext_ref[POOL_HALO:POOL_HALO + ts, cols]
        win = cur
        for j in range(1, w):
            win = win + ext_ref[POOL_HALO - j:POOL_HALO - j + ts, cols]
        d = win / jnp.minimum(t1, float(w)) - cur
        y = jnp.dot(d.astype(BF16), pw_ref[g], preferred_element_type=F32) * scale_ref[:, cols]
        acc = acc + jnp.dot(y.astype(BF16), wup_ref[cols, :], preferred_element_type=F32)
    out_ref[...] = jax.nn.sigmoid(gp_ref[...]) * acc


def _pool(proj, pool_w, pool_scale, w_up_pool, *, bsz, seq, ts):
    t = proj.shape[0]
    ns = seq // ts
    d_pool = w_up_pool.shape[0]
    d = w_up_pool.shape[1]
    return pl.pallas_call(
        _pool_kernel,
        grid=(bsz, ns),
        in_specs=[
            pl.BlockSpec((ts, d_pool), lambda b, s: (b * ns + s, COL_POOL // d_pool)),
            pl.BlockSpec((ts, d), lambda b, s: (b * ns + s, COL_GATE_POOL // d)),
            _resident(pool_w.shape),
            _resident((1, d_pool)),
            _resident(w_up_pool.shape),
        ],
        out_specs=pl.BlockSpec((ts, d), lambda b, s: (b * ns + s, 0)),
        out_shape=jax.ShapeDtypeStruct((t, d), F32),
        scratch_shapes=[pltpu.VMEM((POOL_HALO + ts, d_pool), F32)],
        compiler_params=_cparams(("parallel", "arbitrary")),
        name="pool",
    )(proj, proj, pool_w, pool_scale, w_up_pool)


def _prep_kernel(q_ref, k_ref, v_ref, ba_ref, cw_ref, alog_ref, dtb_ref,
                 qn_ref, kn_ref, vv_ref, bx_ref, gx_ref, gt_ref, ext_ref):
    ts = q_ref.shape[0]
    dq = q_ref.shape[1]
    s = pl.program_id(1)

    @pl.when(s == 0)
    def _():
        ext_ref[0:CONV_HALO, :] = jnp.zeros((CONV_HALO, ext_ref.shape[1]), F32)

    @pl.when(s > 0)
    def _():
        ext_ref[0:CONV_HALO, :] = ext_ref[ts:ts + CONV_HALO, :]

    ext_ref[CONV_HALO:CONV_HALO + ts, 0:dq] = q_ref[...]
    ext_ref[CONV_HALO:CONV_HALO + ts, dq:2 * dq] = k_ref[...]
    ext_ref[CONV_HALO:CONV_HALO + ts, 2 * dq:3 * dq] = v_ref[...]

    def conv_silu(c0, width):
        cols = slice(c0, c0 + width)
        first = CONV_HALO - (CONV_K - 1)
        y = ext_ref[first:first + ts, cols] * cw_ref[0:1, cols]
        for tap in range(1, CONV_K):
            y = y + ext_ref[first + tap:first + tap + ts, cols] * cw_ref[tap:tap + 1, cols]
        return y * jax.nn.sigmoid(y)

    for h in range(DN_HEADS):
        hc = slice(h * DN_DK, (h + 1) * DN_DK)
        qh = conv_silu(h * DN_DK, DN_DK)
        qn_ref[:, hc] = qh * (lax.rsqrt(jnp.sum(qh * qh, axis=-1, keepdims=True) + EPS) * (DN_DK ** -0.5))
        kh = conv_silu(dq + h * DN_DK, DN_DK)
        kn_ref[:, hc] = kh * lax.rsqrt(jnp.sum(kh * kh, axis=-1, keepdims=True) + EPS)
        vv_ref[:, hc] = conv_silu(2 * dq + h * DN_DK, DN_DK)

    ba = ba_ref[...]
    beta = jax.nn.sigmoid(ba)
    xs = ba + dtb_ref[...]
    softplus = jnp.maximum(xs, 0.0) + jnp.log1p(jnp.exp(-jnp.abs(xs)))
    g = -jnp.exp(alog_ref[...]) * softplus
    ri = lax.broadcasted_iota(jnp.int32, (ts, ts), 0)
    ci = lax.broadcasted_iota(jnp.int32, (ts, ts), 1)
    tri = jnp.where((ri // CHUNK == ci // CHUNK) & (ci <= ri), 1.0, 0.0).astype(F32)
    gcum = jnp.dot(tri, g, preferred_element_type=F32, precision=lax.Precision.HIGHEST)
    gt_ref[0] = gcum.T[DN_HEADS:2 * DN_HEADS, :]
    for h in range(DN_HEADS):
        hc = slice(h * DN_DK, (h + 1) * DN_DK)
        bx_ref[:, hc] = jnp.broadcast_to(beta[:, h:h + 1], (ts, DN_DK))
        gx_ref[:, hc] = jnp.broadcast_to(gcum[:, DN_HEADS + h:DN_HEADS + h + 1], (ts, DN_DK))


def _prep(proj, ba, conv_w, alog_pad, dtb_pad, *, bsz, seq, ts):
    t = proj.shape[0]
    ns = seq // ts
    dq = DN_HEADS * DN_DK
    row = lambda b, s: b * ns + s
    blk = lambda col: pl.BlockSpec((ts, dq), lambda b, s: (row(b, s), col // dq))
    out_blk = pl.BlockSpec((ts, dq), lambda b, s: (row(b, s), 0))
    return pl.pallas_call(
        _prep_kernel,
        grid=(bsz, ns),
        in_specs=[
            blk(COL_Q), blk(COL_K), blk(COL_V),
            pl.BlockSpec((ts, LANES), lambda b, s: (row(b, s), 0)),
            _resident(conv_w.shape),
            _resident((1, LANES)),
            _resident((1, LANES)),
        ],
        out_specs=[out_blk, out_blk, out_blk, out_blk, out_blk,
                   pl.BlockSpec((1, DN_HEADS, ts), lambda b, s: (row(b, s), 0, 0))],
        out_shape=[jax.ShapeDtypeStruct((t, dq), F32)] * 5
                  + [jax.ShapeDtypeStruct((t // ts, DN_HEADS, ts), F32)],
        scratch_shapes=[pltpu.VMEM((CONV_HALO + ts, 3 * dq), F32)],
        compiler_params=_cparams(("parallel", "arbitrary")),
        name="prep",
    )(proj, proj, proj, ba, conv_w, alog_pad, dtb_pad)


def _dot_nt(a, b):
    return lax.dot_general(a, b, (((1,), (1,)), ((), ())), preferred_element_type=F32)


def _dot_tn(a, b):
    return lax.dot_general(a, b, (((0,), (0,)), ((), ())), preferred_element_type=F32)


GROUP = 256
PAIR = 2 * CHUNK


def _wy_kernel(q_ref, k_ref, v_ref, bx_ref, gx_ref, gt_ref,
               u_ref, w_ref, qg_ref, kd_ref, at_ref, egl_ref):
    ts = q_ref.shape[0]
    ii = lax.broadcasted_iota(jnp.int32, (GROUP, GROUP), 0)
    jj = lax.broadcasted_iota(jnp.int32, (GROUP, GROUP), 1)
    same = (ii // CHUNK) == (jj // CHUNK)
    incl = same & (ii >= jj)
    strict = same & (ii > jj)

    items = [(slice(g * GROUP, (g + 1) * GROUP), hh) for hh in range(q_ref.shape[1] // DN_DK)
             for g in range(ts // GROUP)]
    lows, sols = [], []
    for rows, hh in items:
        hc = slice(hh * DN_DK, (hh + 1) * DN_DK)
        q = q_ref[rows, hc]
        k = k_ref[rows, hc]
        beta = bx_ref[rows, hc]
        gc = gx_ref[rows, hc]
        gr = gt_ref[0, hh:hh + 1, rows]
        decay = jnp.where(incl, jnp.exp(jnp.concatenate([gc, gc], axis=1) - gr), 0.0)
        kb = k * beta
        a = _dot_nt(jnp.concatenate([kb, q], axis=0).astype(BF16), k.astype(BF16))
        lows.append(jnp.where(strict, a[:GROUP] * decay, 0.0).astype(BF16))
        attn = a[GROUP:] * decay
        for j in range(GROUP // PAIR):
            pr = slice(j * PAIR, (j + 1) * PAIR)
            at_ref[rows.start + j * PAIR:rows.start + (j + 1) * PAIR, hc] = attn[pr, pr].astype(BF16)
        egc = jnp.exp(gc)
        qg_ref[rows, hc] = (q * egc).astype(BF16)
        lasts = [gc[(c + 1) * CHUNK - 1:(c + 1) * CHUNK, :] for c in range(GROUP // CHUNK)]
        for c, g_last in enumerate(lasts):
            row = rows.start // CHUNK + c
            egl_ref[row:row + 1, hc] = jnp.exp(g_last)
        g_last_rows = jnp.concatenate([jnp.broadcast_to(gl, (CHUNK, DN_DK)) for gl in lasts], axis=0)
        kd_ref[rows, hc] = (k * jnp.exp(g_last_rows - gc)).astype(BF16)
        sols.append(jnp.concatenate([v_ref[rows, hc] * beta, kb * egc], axis=1))

    mm = lambda a16, b: jnp.dot(a16, b.astype(BF16), preferred_element_type=F32)
    sols = [s - mm(l, s) for l, s in zip(lows, sols)]
    pws = lows
    for _ in range(CHUNK.bit_length() - 2):
        pws = [mm(p, p).astype(BF16) for p in pws]
        sols = [s + mm(p, s) for p, s in zip(pws, sols)]
    for (rows, hh), s in zip(items, sols):
        hc = slice(hh * DN_DK, (hh + 1) * DN_DK)
        u_ref[rows, hc] = s[:, :DN_DK]
        w_ref[rows, hc] = s[:, DN_DK:].astype(BF16)


WY_HEADS = 2


def _wy(qn, kn, vv, bx, gx, gt, *, ts):
    t = qn.shape[0]
    hw = WY_HEADS * DN_DK
    hblk = pl.BlockSpec((ts, hw), lambda r, h: (r, h))
    wide = jax.ShapeDtypeStruct((t, DN_HEADS * DN_DK), BF16)
    return pl.pallas_call(
        _wy_kernel,
        grid=(t // ts, DN_HEADS // WY_HEADS),
        in_specs=[hblk, hblk, hblk, hblk, hblk,
                  pl.BlockSpec((1, WY_HEADS, ts), lambda r, h: (r * (DN_HEADS // WY_HEADS) + h, 0, 0))],
        out_specs=[hblk, hblk, hblk, hblk, hblk,
                   pl.BlockSpec((ts // CHUNK, hw), lambda r, h: (r, h))],
        out_shape=[jax.ShapeDtypeStruct((t, DN_HEADS * DN_DK), F32), wide, wide, wide, wide,
                   jax.ShapeDtypeStruct((t // CHUNK, DN_HEADS * DN_DK), F32)],
        compiler_params=_cparams(("parallel", "parallel")),
        name="wy",
    )(qn, kn, vv, bx, gx, gt.reshape(-1, WY_HEADS, ts))


def _scan_kernel(u_ref, w_ref, qg_ref, kd_ref, at_ref, egl_ref, z_ref, dnw_ref, out_ref, state_ref):
    ts = u_ref.shape[0]

    @pl.when(pl.program_id(1) == 0)
    def _():
        state_ref[...] = jnp.zeros(state_ref.shape, F32)

    even_vn = [None] * DN_HEADS
    for c in range(ts // CHUNK):
        rows = slice(c * CHUNK, (c + 1) * CHUNK)
        for h in range(DN_HEADS):
            hc = slice(h * DN_DK, (h + 1) * DN_DK)
            state = state_ref[h]
            wq = jnp.concatenate([w_ref[rows, hc], qg_ref[rows, hc]], axis=0)
            r = jnp.dot(wq, state.astype(BF16), preferred_element_type=F32)
            vn16 = (u_ref[rows, hc] - r[:CHUNK]).astype(BF16)
            if c % 2 == 0:
                even_vn[h] = vn16
                pair = jnp.concatenate([vn16, jnp.zeros_like(vn16)], axis=0)
            else:
                pair = jnp.concatenate([even_vn[h], vn16], axis=0)
            o = r[CHUNK:] + jnp.dot(at_ref[rows, hc], pair, preferred_element_type=F32)
            state_ref[h] = state * egl_ref[c:c + 1, hc] + _dot_tn(kd_ref[rows, hc], vn16)
            on = o * lax.rsqrt(jnp.mean(o * o, axis=-1, keepdims=True) + EPS) * dnw_ref[...]
            z = z_ref[rows, hc]
            out_ref[rows, hc] = (on * (z * jax.nn.sigmoid(z))).astype(out_ref.dtype)


def _scan(u, w, qg, kd, at, egl, proj, dn_norm, *, bsz, seq, ts):
    t, dv = u.shape
    ns = seq // ts
    blk = pl.BlockSpec((ts, dv), lambda b, s: (b * ns + s, 0))
    return pl.pallas_call(
        _scan_kernel,
        grid=(bsz, ns),
        in_specs=[blk, blk, blk, blk, blk,
                  pl.BlockSpec((ts // CHUNK, dv), lambda b, s: (b * ns + s, 0)),
                  pl.BlockSpec((ts, dv), lambda b, s: (b * ns + s, COL_Z // dv)),
                  _resident((1, DN_DK))],
        out_specs=blk,
        out_shape=jax.ShapeDtypeStruct((t, dv), BF16),
        scratch_shapes=[pltpu.VMEM((DN_HEADS, DN_DK, DN_DK), F32)],
        compiler_params=_cparams(("parallel", "arbitrary")),
        name="scan",
    )(u, w, qg, kd, at, egl, proj, dn_norm)


ROUTE_E1, ROUTE_E2, ROUTE_W1, ROUTE_W2, ROUTE_R1, ROUTE_R2 = range(6)
ROUTER_LANE0 = N_GROUPS
NEG_BIG = -1e30


def _merge_kernel(ydn_ref, gd_ref, mp_ref, x_ref, wupdn_ref, wout_ref, gain_ref, wr_ref, br_ref,
                  h1_ref, n2_ref, route_ref, cnt_ref, carry_ref):
    tm = x_ref.shape[0]

    @pl.when(pl.program_id(0) == 0)
    def _():
        carry_ref[...] = jnp.zeros(carry_ref.shape, F32)

    md = jax.nn.sigmoid(gd_ref[...]) * jnp.dot(ydn_ref[...], wupdn_ref[...], preferred_element_type=F32)
    merged = mp_ref[...] + md
    h1 = x_ref[...] + jnp.dot(merged.astype(BF16), wout_ref[...], preferred_element_type=F32)
    h1_ref[...] = h1
    n2 = h1 * lax.rsqrt(jnp.mean(h1 * h1, axis=-1, keepdims=True) + EPS) * gain_ref[...]
    n2_ref[...] = n2

    hi = n2.astype(BF16)
    lo = (n2 - hi.astype(F32)).astype(BF16)
    r_hi = jnp.dot(hi, wr_ref[...], preferred_element_type=F32)
    r_lo = jnp.dot(lo, wr_ref[:, :LANES], preferred_element_type=F32)
    logits = r_hi[:, :LANES] + r_hi[:, LANES:] + r_lo + br_ref[...]

    lane = lax.broadcasted_iota(jnp.int32, (tm, LANES), 1)
    lane_f = lane.astype(F32)
    first_max = lambda v, m: jnp.min(jnp.where(v == m, lane_f, float(LANES)), axis=-1, keepdims=True)

    is_grp = lane < N_GROUPS
    lg = jnp.where(is_grp, logits, NEG_BIG)
    gmax = jnp.max(lg, axis=-1, keepdims=True)
    gi = first_max(lg, gmax)
    p_sel = 1.0 / jnp.sum(jnp.where(is_grp, jnp.exp(lg - gmax), 0.0), axis=-1, keepdims=True)

    e_lane = lane - ROUTER_LANE0
    in_grp = (e_lane >= 0) & (e_lane < N_EXPERTS) & ((e_lane // EXPERTS_PER_GROUP).astype(F32) == gi)
    le = jnp.where(in_grp, logits, NEG_BIG)
    v1 = jnp.max(le, axis=-1, keepdims=True)
    i1 = first_max(le, v1)
    le2 = jnp.where(lane_f == i1, NEG_BIG, le)
    v2 = jnp.max(le2, axis=-1, keepdims=True)
    i2 = first_max(le2, v2)
    s = jnp.exp(v2 - v1)
    w1 = p_sel / (1.0 + s)
    w2 = p_sel * s / (1.0 + s)
    e1 = i1 - float(ROUTER_LANE0)
    e2 = i2 - float(ROUTER_LANE0)

    oh1 = jnp.where(lane_f == e1, 1.0, 0.0)
    oh2 = jnp.where(lane_f == e2, 1.0, 0.0)
    both = oh1 + oh2
    ri = lax.broadcasted_iota(jnp.int32, (tm, tm), 0)
    ci = lax.broadcasted_iota(jnp.int32, (tm, tm), 1)
    tri = jnp.where(ri > ci, 1.0, 0.0).astype(BF16)
    prior = jnp.dot(tri, both.astype(BF16), preferred_element_type=F32) + carry_ref[...]
    r1 = jnp.sum(prior * oh1, axis=-1, keepdims=True)
    r2 = jnp.sum(prior * oh2, axis=-1, keepdims=True)
    carry = carry_ref[...] + jnp.sum(both, axis=0, keepdims=True)
    carry_ref[...] = carry
    cnt_ref[...] = carry

    route = jnp.zeros((tm, LANES), F32)
    for ln, val in ((ROUTE_E1, e1), (ROUTE_E2, e2), (ROUTE_W1, w1), (ROUTE_W2, w2),
                    (ROUTE_R1, r1), (ROUTE_R2, r2)):
        route = jnp.where(lane == ln, val, route)
    route_ref[...] = route


def _merge(y_dn, proj, mp, x2d, w_up_dn, w_out, gain, w_router, b_router, *, tm):
    t, d = x2d.shape
    dv = y_dn.shape[1]
    rows = lambda blk: pl.BlockSpec(blk, lambda i: (i, 0))
    return pl.pallas_call(
        _merge_kernel,
        grid=(t // tm,),
        in_specs=[
            rows((tm, dv)),
            pl.BlockSpec((tm, d), lambda i: (i, COL_GATE_DN // d)),
            rows((tm, d)), rows((tm, d)),
            _resident(w_up_dn.shape), _resident(w_out.shape), _resident((1, d)),
            _resident(w_router.shape), _resident((1, LANES)),
        ],
        out_specs=[rows((tm, d)), rows((tm, d)), rows((tm, LANES)),
                   pl.BlockSpec((1, LANES), lambda i: (0, 0))],
        out_shape=[jax.ShapeDtypeStruct((t, d), F32), jax.ShapeDtypeStruct((t, d), F32),
                   jax.ShapeDtypeStruct((t, LANES), F32), jax.ShapeDtypeStruct((1, LANES), F32)],
        scratch_shapes=[pltpu.VMEM((1, LANES), F32)],
        compiler_params=_cparams(("arbitrary",)),
        name="merge",
    )(y_dn, proj, mp, x2d, w_up_dn, w_out, gain, w_router, b_router)


def _experts_kernel(be_ref, nu_ref, tok_ref, n2_ref, wg_ref, wu_ref, wd_ref, y_ref, xbuf0, xbuf1, sem):
    del be_ref
    bm = xbuf0.shape[0]
    i = pl.program_id(0)
    nu = nu_ref[0]

    def issue(blk, buf, slot):
        for j in range(bm):
            tok = tok_ref[blk * bm + j]
            pltpu.make_async_copy(n2_ref.at[pl.ds(tok, 1)], buf.at[pl.ds(j, 1)],
                                  sem.at[slot]).start(priority=j % 2)

    def drain(buf, slot):
        pltpu.make_async_copy(n2_ref.at[pl.ds(0, bm)], buf, sem.at[slot]).wait()

    def live_step(cur, nxt, slot):
        drain(cur, slot)
        issue(i + 1, nxt, 1 - slot)
        xb = cur[...].astype(BF16)
        g = jnp.dot(xb, wg_ref[0], preferred_element_type=F32)
        u = jnp.dot(xb, wu_ref[0], preferred_element_type=F32)
        hmid = (g * jax.nn.sigmoid(g)) * u
        y_ref[...] = jnp.dot(hmid.astype(BF16), wd_ref[0], preferred_element_type=F32)

    @pl.when(i == 0)
    def _():
        issue(0, xbuf0, 0)

    @pl.when((i < nu) & (i % 2 == 0))
    def _():
        live_step(xbuf0, xbuf1, 0)

    @pl.when((i < nu) & (i % 2 == 1))
    def _():
        live_step(xbuf1, xbuf0, 1)

    @pl.when(i >= nu)
    def _():
        y_ref[...] = jnp.zeros(y_ref.shape, F32)

    @pl.when((i == nu) & (i % 2 == 0))
    def _():
        drain(xbuf0, 0)

    @pl.when((i == nu) & (i % 2 == 1))
    def _():
        drain(xbuf1, 1)


def _experts(blk_e, n_used, tok_pad, n2, w_gate, w_up, w_down, *, bm):
    d = n2.shape[1]
    cap = tok_pad.shape[0]
    de = w_gate.shape[2]
    wspec = lambda shape: pl.BlockSpec(shape, lambda i, be, nu, tok: (be[i], 0, 0))
    return pl.pallas_call(
        _experts_kernel,
        grid_spec=pltpu.PrefetchScalarGridSpec(
            num_scalar_prefetch=3,
            grid=(cap // bm,),
            in_specs=[pl.BlockSpec(memory_space=pl.ANY),
                      wspec((1, d, de)), wspec((1, d, de)), wspec((1, de, d))],
            out_specs=pl.BlockSpec((bm, d), lambda i, be, nu, tok: (i, 0)),
            scratch_shapes=[pltpu.VMEM((bm, d), F32), pltpu.VMEM((bm, d), F32),
                            pltpu.SemaphoreType.DMA((2,))],
        ),
        out_shape=jax.ShapeDtypeStruct((cap, d), F32),
        compiler_params=_cparams(("arbitrary",)),
        name="experts",
    )(blk_e, n_used, tok_pad, n2, w_gate, w_up, w_down)


def _combine_kernel(pos_ref, h1_ref, route_ref, p_ref, ys_ref, wpg_ref, wpp_ref, gple_ref, gfin_ref,
                    out_ref, ybuf0, ybuf1, sem, *, final_norm, n_steps):
    tm = h1_ref.shape[0]
    i = pl.program_id(0)

    def issue(blk, buf, slot):
        for j in range(tm):
            for k in range(2):
                src = pos_ref[2 * (blk * tm + j) + k]
                pltpu.make_async_copy(ys_ref.at[pl.ds(src, 1)], buf.at[pl.ds(k * tm + j, 1)],
                                      sem.at[slot]).start(priority=k)

    def step(cur, nxt, slot):
        pltpu.make_async_copy(ys_ref.at[pl.ds(0, 2 * tm)], cur, sem.at[slot]).wait()
        if nxt is not None:
            issue(i + 1, nxt, 1 - slot)
        route = route_ref[...]
        y_moe = (cur[0:tm, :] * route[:, ROUTE_W1:ROUTE_W1 + 1]
                 + cur[tm:2 * tm, :] * route[:, ROUTE_W2:ROUTE_W2 + 1])
        h2 = h1_ref[...] + y_moe
        n3 = h2 * lax.rsqrt(jnp.mean(h2 * h2, axis=-1, keepdims=True) + EPS) * gple_ref[...]
        gate = jax.nn.sigmoid(jnp.dot(n3.astype(BF16), wpg_ref[...], preferred_element_type=F32))
        h3 = h2 + gate * jnp.dot(p_ref[...].astype(BF16), wpp_ref[...], preferred_element_type=F32)
        if final_norm:
            h3 = h3 * lax.rsqrt(jnp.mean(h3 * h3, axis=-1, keepdims=True) + EPS) * gfin_ref[...]
        out_ref[...] = h3

    @pl.when(i == 0)
    def _():
        issue(0, ybuf0, 0)

    last = n_steps - 1
    bufs = (ybuf0, ybuf1)

    @pl.when((i < last) & (i % 2 == 0))
    def _():
        step(ybuf0, ybuf1, 0)

    @pl.when((i < last) & (i % 2 == 1))
    def _():
        step(ybuf1, ybuf0, 1)

    @pl.when(i == last)
    def _():
        step(bufs[last % 2], None, last % 2)


def _combine(pos, h1, route, p2d, ys, w_ple_gate, w_ple_proj, g_ple, g_fin, *, tm, final_norm):
    t, d = h1.shape
    rows = lambda w: pl.BlockSpec((tm, w), lambda i, pos: (i, 0))
    res = lambda shape: pl.BlockSpec(shape, lambda i, pos: (0,) * len(shape), pipeline_mode=pl.Buffered(1))
    return pl.pallas_call(
        functools.partial(_combine_kernel, final_norm=final_norm, n_steps=t // tm),
        grid_spec=pltpu.PrefetchScalarGridSpec(
            num_scalar_prefetch=1,
            grid=(t // tm,),
            in_specs=[rows(d), rows(LANES), rows(p2d.shape[1]), pl.BlockSpec(memory_space=pl.ANY),
                      res(w_ple_gate.shape), res(w_ple_proj.shape), res((1, d)), res((1, d))],
            out_specs=rows(d),
            scratch_shapes=[pltpu.VMEM((2 * tm, d), F32), pltpu.VMEM((2 * tm, d), F32),
                            pltpu.SemaphoreType.DMA((2,))],
        ),
        out_shape=jax.ShapeDtypeStruct((t, d), F32),
        compiler_params=_cparams(("arbitrary",)),
        name="combine",
    )(pos, h1, route, p2d, ys, w_ple_gate, w_ple_proj, g_ple, g_fin)


def _pick(n, pref):
    b = min(n, pref)
    while n % b:
        b -= 8
    return b


def _stage_mixers(x, prm, i):
    bsz, seq, d = x.shape
    t = bsz * seq
    x2d = x.reshape(t, d)
    w_in = prm["w_in"][i]
    d_pool = prm["w_up_pool"].shape[1]
    dq = DN_HEADS * DN_DK
    o_qkv = d_pool
    o_z = o_qkv + 3 * dq
    o_b = o_z + dq
    o_gp = o_b + 2 * DN_HEADS
    o_gd = o_gp + d
    w_main = jnp.concatenate(
        [w_in[:, o_gp:o_gp + d], w_in[:, o_gd:o_gd + d], w_in[:, :d_pool],
         w_in[:, o_qkv:o_qkv + 3 * dq], w_in[:, o_z:o_z + dq]], axis=1).astype(BF16)
    assert w_main.shape[1] == N_MAIN
    w_ba = jnp.pad(w_in[:, o_b:o_b + 2 * DN_HEADS], ((0, 0), (0, LANES - 2 * DN_HEADS))).astype(BF16)
    proj, ba = _inproj(x2d, prm["norm_mix"][i][None, :], w_main, w_ba,
                       tm=_pick(t, 1024), tn=1024)

    ts = _pick(seq, 512)
    mp = _pool(proj, prm["pool_w"][i].astype(BF16), prm["pool_scale"][i][None, :],
               prm["w_up_pool"][i].astype(BF16), bsz=bsz, seq=seq, ts=ts)

    lane_pad = lambda a: jnp.pad(a, (DN_HEADS, LANES - 2 * DN_HEADS))[None, :]
    qn, kn, vv, bx, gx, gt = _prep(proj, ba, prm["conv_w"][i], lane_pad(prm["a_log"][i]),
                                   lane_pad(prm["dt_bias"][i]), bsz=bsz, seq=seq, ts=ts)
    u, w, qg, kd, at, egl = _wy(qn, kn, vv, bx, gx, gt, ts=ts)
    y_dn = _scan(u, w, qg, kd, at, egl, proj, prm["dn_norm"][i][None, :], bsz=bsz, seq=seq, ts=ts)
    return dict(proj=proj, ba=ba, mp=mp, qn=qn, kn=kn, vv=vv, bx=bx, gx=gx, gt=gt, y_dn=y_dn)


MOE_BM = 256


def _router_weights(w_rg, b_rg, w_re, b_re):
    d = w_rg.shape[0]
    w = jnp.concatenate([w_rg, w_re, jnp.zeros((d, LANES - N_GROUPS - N_EXPERTS), F32)], axis=1)
    w_hi = w.astype(BF16)
    w_lo = (w - w_hi.astype(F32)).astype(BF16)
    b = jnp.concatenate([b_rg, b_re, jnp.zeros((LANES - N_GROUPS - N_EXPERTS,), F32)])[None, :]
    return jnp.concatenate([w_hi, w_lo], axis=1), b


def _stage_moe_plan(route, cnt, t):
    counts = cnt[0, :N_EXPERTS].astype(jnp.int32)
    padded = ((counts + MOE_BM - 1) // MOE_BM) * MOE_BM
    pends = jnp.cumsum(padded)
    pstarts = pends - padded
    eid = route[:, ROUTE_E1:ROUTE_E2 + 1].astype(jnp.int32)
    rank = route[:, ROUTE_R1:ROUTE_R2 + 1].astype(jnp.int32)
    pos = (pstarts[eid] + rank).reshape(2 * t)
    cap = 2 * t + N_EXPERTS * MOE_BM
    n_blocks = cap // MOE_BM
    n_used = (pends[-1] // MOE_BM).astype(jnp.int32)
    blk = jnp.minimum(jnp.arange(n_blocks, dtype=jnp.int32), n_used - 1)
    blk_e = jnp.sum((pends[None, :] <= (blk * MOE_BM)[:, None]).astype(jnp.int32), axis=1)
    blk_e = jnp.minimum(blk_e, N_EXPERTS - 1)
    tok_pad = jnp.zeros((cap,), jnp.int32).at[pos].set(
        jnp.arange(2 * t, dtype=jnp.int32) // 2, unique_indices=True, mode="promise_in_bounds")
    return pos, blk_e, n_used.reshape(1), tok_pad


def kernel(x, p, norm_mix, w_in, pool_w, pool_scale, conv_w, a_log, dt_bias, dn_norm, w_up_pool, w_up_dn, w_out, norm_moe, w_router_group, b_router_group, w_router_expert, b_router_expert, w_gate, w_up, w_down, norm_ple, w_ple_gate, w_ple_proj, norm_final):
    bsz, seq, d = x.shape
    t = bsz * seq
    depth = norm_mix.shape[0]
    prm = dict(norm_mix=norm_mix, w_in=w_in, pool_w=pool_w, pool_scale=pool_scale, conv_w=conv_w,
               a_log=a_log, dt_bias=dt_bias, dn_norm=dn_norm, w_up_pool=w_up_pool)
    h = x
    for i in range(depth):
        st = _stage_mixers(h, prm, i)
        w_router, b_router = _router_weights(w_router_group[i], b_router_group[i],
                                             w_router_expert[i], b_router_expert[i])
        tm = _pick(t, 256)
        h1, n2, route, cnt = _merge(st["y_dn"], st["proj"], st["mp"], h.reshape(t, d),
                                    w_up_dn[i].astype(BF16), w_out[i].astype(BF16), norm_moe[i][None, :],
                                    w_router, b_router, tm=tm)
        pos, blk_e, n_used, tok_pad = _stage_moe_plan(route, cnt, t)
        ys = _experts(blk_e, n_used, tok_pad, n2, w_gate[i].astype(BF16), w_up[i].astype(BF16),
                      w_down[i].astype(BF16), bm=MOE_BM)
        out = _combine(pos, h1, route, p[i].reshape(t, -1), ys, w_ple_gate[i].astype(BF16),
                       w_ple_proj[i].astype(BF16), norm_ple[i][None, :], norm_final[None, :],
                       tm=tm, final_norm=(i == depth - 1))
        h = out.reshape(bsz, seq, d)
    return h
```

```python
import functools

import jax
import jax.numpy as jnp
from jax import lax
from jax.experimental import pallas as pl
from jax.experimental.pallas import tpu as pltpu

F32 = jnp.float32
BF16 = jnp.bfloat16

EPS = 1e-6
POOL_WINDOWS = (2, 4, 8, 16)
POOL_HALO = 16
CONV_K = 4
CONV_HALO = 8
DN_HEADS = 8
DN_DK = 128
CHUNK = 64
N_GROUPS = 4
EXPERTS_PER_GROUP = 8
N_EXPERTS = N_GROUPS * EXPERTS_PER_GROUP
LANES = 128
VMEM_LIMIT = 56 * 1024 * 1024

COL_GATE_POOL = 0
COL_GATE_DN = 2048
COL_POOL = 4096
COL_Q = 5120
COL_K = 6144
COL_V = 7168
COL_Z = 8192
N_MAIN = 9216


def _cparams(sem):
    return pltpu.CompilerParams(dimension_semantics=sem, vmem_limit_bytes=VMEM_LIMIT)


def _resident(shape):
    nd = len(shape)
    return pl.BlockSpec(shape, lambda *_: (0,) * nd, pipeline_mode=pl.Buffered(1))


def _inproj_kernel(x_ref, gain_ref, w_ref, wba_ref, out_ref, ba_ref, n1_ref, *, sub):
    tm = x_ref.shape[0]

    @pl.when(pl.program_id(1) == 0)
    def _():
        def body(r, _):
            rows = pl.ds(pl.multiple_of(r * sub, sub), sub)
            x = x_ref[rows, :]
            y = x * lax.rsqrt(jnp.mean(x * x, axis=-1, keepdims=True) + EPS) * gain_ref[...]
            n1_ref[rows, :] = y.astype(BF16)
            return 0
        lax.fori_loop(0, tm // sub, body, 0)
        ba_ref[...] = jnp.dot(n1_ref[...], wba_ref[...], preferred_element_type=F32)

    out_ref[...] = jnp.dot(n1_ref[...], w_ref[...], preferred_element_type=F32)


def _inproj(x2d, gain, w_main, w_ba, *, tm, tn):
    t, d = x2d.shape
    n = w_main.shape[1]
    return pl.pallas_call(
        functools.partial(_inproj_kernel, sub=min(tm, 256)),
        grid=(t // tm, n // tn),
        in_specs=[
            pl.BlockSpec((tm, d), lambda i, j: (i, 0)),
            _resident((1, d)),
            pl.BlockSpec((d, tn), lambda i, j: (0, j)),
            _resident((d, LANES)),
        ],
        out_specs=[
            pl.BlockSpec((tm, tn), lambda i, j: (i, j)),
            pl.BlockSpec((tm, LANES), lambda i, j: (i, 0)),
        ],
        out_shape=[jax.ShapeDtypeStruct((t, n), F32), jax.ShapeDtypeStruct((t, LANES), F32)],
        scratch_shapes=[pltpu.VMEM((tm, d), BF16)],
        compiler_params=_cparams(("parallel", "arbitrary")),
        name="inproj",
    )(x2d, gain, w_main, w_ba)


def _pool_kernel(u_ref, gp_ref, pw_ref, scale_ref, wup_ref, out_ref, ext_ref):
    ts = u_ref.shape[0]
    gw = pw_ref.shape[1]
    s = pl.program_id(1)

    @pl.when(s == 0)
    def _():
        ext_ref[0:POOL_HALO, :] = jnp.zeros((POOL_HALO, ext_ref.shape[1]), F32)

    @pl.when(s > 0)
    def _():
        ext_ref[0:POOL_HALO, :] = ext_ref[ts:ts + POOL_HALO, :]

    ext_ref[POOL_HALO:POOL_HALO + ts, :] = u_ref[...]

    t1 = (s * ts + 1 + lax.broadcasted_iota(jnp.int32, (ts, 1), 0)).astype(F32)
    acc = jnp.zeros(out_ref.shape, F32)
    for g, w in enumerate(POOL_WINDOWS):
        cols = slice(g * gw, (g + 1) * gw)
        cur = ext_ref[POOL_HALO:POOL_HALO + ts, cols]
        win = cur
        for j in range(1, w):
            win = win + ext_ref[POOL_HALO - j:POOL_HALO - j + ts, cols]
        d = win / jnp.minimum(t1, float(w)) - cur
        y = jnp.dot(d.astype(BF16), pw_ref[g], preferred_element_type=F32) * scale_ref[:, cols]
        acc = acc + jnp.dot(y.astype(BF16), wup_ref[cols, :], preferred_element_type=F32)
    out_ref[...] = jax.nn.sigmoid(gp_ref[...]) * acc


def _pool(proj, pool_w, pool_scale, w_up_pool, *, bsz, seq, ts):
    t = proj.shape[0]
    ns = seq // ts
    d_pool = w_up_pool.shape[0]
    d = w_up_pool.shape[1]
    return pl.pallas_call(
        _pool_kernel,
        grid=(bsz, ns),
        in_specs=[
            pl.BlockSpec((ts, d_pool), lambda b, s: (b * ns + s, COL_POOL // d_pool)),
            pl.BlockSpec((ts, d), lambda b, s: (b * ns + s, COL_GATE_POOL // d)),
            _resident(pool_w.shape),
            _resident((1, d_pool)),
            _resident(w_up_pool.shape),
        ],
        out_specs=pl.BlockSpec((ts, d), lambda b, s: (b * ns + s, 0)),
        out_shape=jax.ShapeDtypeStruct((t, d), F32),
        scratch_shapes=[pltpu.VMEM((POOL_HALO + ts, d_pool), F32)],
        compiler_params=_cparams(("parallel", "arbitrary")),
        name="pool",
    )(proj, proj, pool_w, pool_scale, w_up_pool)


def _prep_kernel(q_ref, k_ref, v_ref, ba_ref, cw_ref, alog_ref, dtb_ref,
                 qn_ref, kn_ref, vv_ref, bx_ref, gx_ref, gt_ref, ext_ref):
    ts = q_ref.shape[0]
    dq = q_ref.shape[1]
    s = pl.program_id(1)

    @pl.when(s == 0)
    def _():
        ext_ref[0:CONV_HALO, :] = jnp.zeros((CONV_HALO, ext_ref.shape[1]), F32)

    @pl.when(s > 0)
    def _():
        ext_ref[0:CONV_HALO, :] = ext_ref[ts:ts + CONV_HALO, :]

    ext_ref[CONV_HALO:CONV_HALO + ts, 0:dq] = q_ref[...]
    ext_ref[CONV_HALO:CONV_HALO + ts, dq:2 * dq] = k_ref[...]
    ext_ref[CONV_HALO:CONV_HALO + ts, 2 * dq:3 * dq] = v_ref[...]

    def conv_silu(c0, width):
        cols = slice(c0, c0 + width)
        first = CONV_HALO - (CONV_K - 1)
        y = ext_ref[first:first + ts, cols] * cw_ref[0:1, cols]
        for tap in range(1, CONV_K):
            y = y + ext_ref[first + tap:first + tap + ts, cols] * cw_ref[tap:tap + 1, cols]
        return y * jax.nn.sigmoid(y)

    for h in range(DN_HEADS):
        hc = slice(h * DN_DK, (h + 1) * DN_DK)
        qh = conv_silu(h * DN_DK, DN_DK)
        qn_ref[:, hc] = qh * (lax.rsqrt(jnp.sum(qh * qh, axis=-1, keepdims=True) + EPS) * (DN_DK ** -0.5))
        kh = conv_silu(dq + h * DN_DK, DN_DK)
        kn_ref[:, hc] = kh * lax.rsqrt(jnp.sum(kh * kh, axis=-1, keepdims=True) + EPS)
        vv_ref[:, hc] = conv_silu(2 * dq + h * DN_DK, DN_DK)

    ba = ba_ref[...]
    beta = jax.nn.sigmoid(ba)
    xs = ba + dtb_ref[...]
    softplus = jnp.maximum(xs, 0.0) + jnp.log1p(jnp.exp(-jnp.abs(xs)))
    g = -jnp.exp(alog_ref[...]) * softplus
    ri = lax.broadcasted_iota(jnp.int32, (ts, ts), 0)
    ci = lax.broadcasted_iota(jnp.int32, (ts, ts), 1)
    tri = jnp.where((ri // CHUNK == ci // CHUNK) & (ci <= ri), 1.0, 0.0).astype(F32)
    gcum = jnp.dot(tri, g, preferred_element_type=F32, precision=lax.Precision.HIGHEST)
    gt_ref[0] = gcum.T[DN_HEADS:2 * DN_HEADS, :]
    for h in range(DN_HEADS):
        hc = slice(h * DN_DK, (h + 1) * DN_DK)
        bx_ref[:, hc] = jnp.broadcast_to(beta[:, h:h + 1], (ts, DN_DK))
        gx_ref[:, hc] = jnp.broadcast_to(gcum[:, DN_HEADS + h:DN_HEADS + h + 1], (ts, DN_DK))


def _prep(proj, ba, conv_w, alog_pad, dtb_pad, *, bsz, seq, ts):
    t = proj.shape[0]
    ns = seq // ts
    dq = DN_HEADS * DN_DK
    row = lambda b, s: b * ns + s
    blk = lambda col: pl.BlockSpec((ts, dq), lambda b, s: (row(b, s), col // dq))
    out_blk = pl.BlockSpec((ts, dq), lambda b, s: (row(b, s), 0))
    return pl.pallas_call(
        _prep_kernel,
        grid=(bsz, ns),
        in_specs=[
            blk(COL_Q), blk(COL_K), blk(COL_V),
            pl.BlockSpec((ts, LANES), lambda b, s: (row(b, s), 0)),
            _resident(conv_w.shape),
            _resident((1, LANES)),
            _resident((1, LANES)),
        ],
        out_specs=[out_blk, out_blk, out_blk, out_blk, out_blk,
                   pl.BlockSpec((1, DN_HEADS, ts), lambda b, s: (row(b, s), 0, 0))],
        out_shape=[jax.ShapeDtypeStruct((t, dq), F32)] * 5
                  + [jax.ShapeDtypeStruct((t // ts, DN_HEADS, ts), F32)],
        scratch_shapes=[pltpu.VMEM((CONV_HALO + ts, 3 * dq), F32)],
        compiler_params=_cparams(("parallel", "arbitrary")),
        name="prep",
    )(proj, proj, proj, ba, conv_w, alog_pad, dtb_pad)


def _dot_nt(a, b):
    return lax.dot_general(a, b, (((1,), (1,)), ((), ())), preferred_element_type=F32)


def _dot_tn(a, b):
    return lax.dot_general(a, b, (((0,), (0,)), ((), ())), preferred_element_type=F32)


GROUP = 256
PAIR = 2 * CHUNK


def _wy_kernel(q_ref, k_ref, v_ref, bx_ref, gx_ref, gt_ref,
               u_ref, w_ref, qg_ref, kd_ref, at_ref, egl_ref):
    ts = q_ref.shape[0]
    ii = lax.broadcasted_iota(jnp.int32, (GROUP, GROUP), 0)
    jj = lax.broadcasted_iota(jnp.int32, (GROUP, GROUP), 1)
    same = (ii // CHUNK) == (jj // CHUNK)
    incl = same & (ii >= jj)
    strict = same & (ii > jj)

    items = [(slice(g * GROUP, (g + 1) * GROUP), hh) for hh in range(q_ref.shape[1] // DN_DK)
             for g in range(ts // GROUP)]
    lows, sols = [], []
    for rows, hh in items:
        hc = slice(hh * DN_DK, (hh + 1) * DN_DK)
        q = q_ref[rows, hc]
        k = k_ref[rows, hc]
        beta = bx_ref[rows, hc]
        gc = gx_ref[rows, hc]
        gr = gt_ref[0, hh:hh + 1, rows]
        decay = jnp.where(incl, jnp.exp(jnp.concatenate([gc, gc], axis=1) - gr), 0.0)
        kb = k * beta
        a = _dot_nt(jnp.concatenate([kb, q], axis=0).astype(BF16), k.astype(BF16))
        lows.append(jnp.where(strict, a[:GROUP] * decay, 0.0).astype(BF16))
        attn = a[GROUP:] * decay
        for j in range(GROUP // PAIR):
            pr = slice(j * PAIR, (j + 1) * PAIR)
            at_ref[rows.start + j * PAIR:rows.start + (j + 1) * PAIR, hc] = attn[pr, pr].astype(BF16)
        egc = jnp.exp(gc)
        qg_ref[rows, hc] = (q * egc).astype(BF16)
        lasts = [gc[(c + 1) * CHUNK - 1:(c + 1) * CHUNK, :] for c in range(GROUP // CHUNK)]
        for c, g_last in enumerate(lasts):
            row = rows.start // CHUNK + c
            egl_ref[row:row + 1, hc] = jnp.exp(g_last)
        g_last_rows = jnp.concatenate([jnp.broadcast_to(gl, (CHUNK, DN_DK)) for gl in lasts], axis=0)
        kd_ref[rows, hc] = (k * jnp.exp(g_last_rows - gc)).astype(BF16)
        sols.append(jnp.concatenate([v_ref[rows, hc] * beta, kb * egc], axis=1))

    mm = lambda a16, b: jnp.dot(a16, b.astype(BF16), preferred_element_type=F32)
    sols = [s - mm(l, s) for l, s in zip(lows, sols)]
    pws = lows
    for _ in range(CHUNK.bit_length() - 2):
        pws = [mm(p, p).astype(BF16) for p in pws]
        sols = [s + mm(p, s) for p, s in zip(pws, sols)]
    for (rows, hh), s in zip(items, sols):
        hc = slice(hh * DN_DK, (hh + 1) * DN_DK)
        u_ref[rows, hc] = s[:, :DN_DK]
        w_ref[rows, hc] = s[:, DN_DK:].astype(BF16)


WY_HEADS = 2


def _wy(qn, kn, vv, bx, gx, gt, *, ts):
    t = qn.shape[0]
    hw = WY_HEADS * DN_DK
    hblk = pl.BlockSpec((ts, hw), lambda r, h: (r, h))
    wide = jax.ShapeDtypeStruct((t, DN_HEADS * DN_DK), BF16)
    return pl.pallas_call(
        _wy_kernel,
        grid=(t // ts, DN_HEADS // WY_HEADS),
        in_specs=[hblk, hblk, hblk, hblk, hblk,
                  pl.BlockSpec((1, WY_HEADS, ts), lambda r, h: (r * (DN_HEADS // WY_HEADS) + h, 0, 0))],
        out_specs=[hblk, hblk, hblk, hblk, hblk,
                   pl.BlockSpec((ts // CHUNK, hw), lambda r, h: (r, h))],
        out_shape=[jax.ShapeDtypeStruct((t, DN_HEADS * DN_DK), F32), wide, wide, wide, wide,
                   jax.ShapeDtypeStruct((t // CHUNK, DN_HEADS * DN_DK), F32)],
        compiler_params=_cparams(("parallel", "parallel")),
        name="wy",
    )(qn, kn, vv, bx, gx, gt.reshape(-1, WY_HEADS, ts))


def _scan_kernel(u_ref, w_ref, qg_ref, kd_ref, at_ref, egl_ref, z_ref, dnw_ref, out_ref, state_ref):
    ts = u_ref.shape[0]

    @pl.when(pl.program_id(1) == 0)
    def _():
        state_ref[...] = jnp.zeros(state_ref.shape, F32)

    even_vn = [None] * DN_HEADS
    for c in range(ts // CHUNK):
        rows = slice(c * CHUNK, (c + 1) * CHUNK)
        for h in range(DN_HEADS):
            hc = slice(h * DN_DK, (h + 1) * DN_DK)
            state = state_ref[h]
            wq = jnp.concatenate([w_ref[rows, hc], qg_ref[rows, hc]], axis=0)
            r = jnp.dot(wq, state.astype(BF16), preferred_element_type=F32)
            vn16 = (u_ref[rows, hc] - r[:CHUNK]).astype(BF16)
            if c % 2 == 0:
                even_vn[h] = vn16
                pair = jnp.concatenate([vn16, jnp.zeros_like(vn16)], axis=0)
            else:
                pair = jnp.concatenate([even_vn[h], vn16], axis=0)
            o = r[CHUNK:] + jnp.dot(at_ref[rows, hc], pair, preferred_element_type=F32)
            state_ref[h] = state * egl_ref[c:c + 1, hc] + _dot_tn(kd_ref[rows, hc], vn16)
            on = o * lax.rsqrt(jnp.mean(o * o, axis=-1, keepdims=True) + EPS) * dnw_ref[...]
            z = z_ref[rows, hc]
            out_ref[rows, hc] = (on * (z * jax.nn.sigmoid(z))).astype(out_ref.dtype)


def _scan(u, w, qg, kd, at, egl, proj, dn_norm, *, bsz, seq, ts):
    t, dv = u.shape
    ns = seq // ts
    blk = pl.BlockSpec((ts, dv), lambda b, s: (b * ns + s, 0))
    return pl.pallas_call(
        _scan_kernel,
        grid=(bsz, ns),
        in_specs=[blk, blk, blk, blk, blk,
                  pl.BlockSpec((ts // CHUNK, dv), lambda b, s: (b * ns + s, 0)),
                  pl.BlockSpec((ts, dv), lambda b, s: (b * ns + s, COL_Z // dv)),
                  _resident((1, DN_DK))],
        out_specs=blk,
        out_shape=jax.ShapeDtypeStruct((t, dv), BF16),
        scratch_shapes=[pltpu.VMEM((DN_HEADS, DN_DK, DN_DK), F32)],
        compiler_params=_cparams(("parallel", "arbitrary")),
        name="scan",
    )(u, w, qg, kd, at, egl, proj, dn_norm)


ROUTE_E1, ROUTE_E2, ROUTE_W1, ROUTE_W2, ROUTE_R1, ROUTE_R2 = range(6)
ROUTER_LANE0 = N_GROUPS
NEG_BIG = -1e30


def _pack_rows(x):
    n = x.shape[1] // 2
    bits = lambda v: lax.bitcast_convert_type(v.astype(BF16).astype(F32), jnp.uint32)
    return bits(x[:, n:]) | (bits(x[:, :n]) >> 16)


def _unpack_rows(p):
    lo = lax.bitcast_convert_type(p << 16, F32).astype(BF16)
    hi = lax.bitcast_convert_type(p & jnp.uint32(0xFFFF0000), F32).astype(BF16)
    return jnp.concatenate([lo, hi], axis=1)


def _merge_kernel(ydn_ref, gd_ref, mp_ref, x_ref, wupdn_ref, wout_ref, gain_ref, wr_ref, br_ref,
                  h1_ref, n2_ref, route_ref, cnt_ref, carry_ref):
    tm = x_ref.shape[0]

    @pl.when(pl.program_id(0) == 0)
    def _():
        carry_ref[...] = jnp.zeros(carry_ref.shape, F32)

    md = jax.nn.sigmoid(gd_ref[...]) * jnp.dot(ydn_ref[...], wupdn_ref[...], preferred_element_type=F32)
    merged = mp_ref[...] + md
    h1 = x_ref[...] + jnp.dot(merged.astype(BF16), wout_ref[...], preferred_element_type=F32)
    h1_ref[...] = h1
    n2 = h1 * lax.rsqrt(jnp.mean(h1 * h1, axis=-1, keepdims=True) + EPS) * gain_ref[...]
    n2_ref[...] = _pack_rows(n2)

    hi = n2.astype(BF16)
    lo = (n2 - hi.astype(F32)).astype(BF16)
    r_hi = jnp.dot(hi, wr_ref[...], preferred_element_type=F32)
    r_lo = jnp.dot(lo, wr_ref[:, :LANES], preferred_element_type=F32)
    logits = r_hi[:, :LANES] + r_hi[:, LANES:] + r_lo + br_ref[...]

    lane = lax.broadcasted_iota(jnp.int32, (tm, LANES), 1)
    lane_f = lane.astype(F32)
    first_max = lambda v, m: jnp.min(jnp.where(v == m, lane_f, float(LANES)), axis=-1, keepdims=True)

    is_grp = lane < N_GROUPS
    lg = jnp.where(is_grp, logits, NEG_BIG)
    gmax = jnp.max(lg, axis=-1, keepdims=True)
    gi = first_max(lg, gmax)
    p_sel = 1.0 / jnp.sum(jnp.where(is_grp, jnp.exp(lg - gmax), 0.0), axis=-1, keepdims=True)

    e_lane = lane - ROUTER_LANE0
    in_grp = (e_lane >= 0) & (e_lane < N_EXPERTS) & ((e_lane // EXPERTS_PER_GROUP).astype(F32) == gi)
    le = jnp.where(in_grp, logits, NEG_BIG)
    v1 = jnp.max(le, axis=-1, keepdims=True)
    i1 = first_max(le, v1)
    le2 = jnp.where(lane_f == i1, NEG_BIG, le)
    v2 = jnp.max(le2, axis=-1, keepdims=True)
    i2 = first_max(le2, v2)
    s = jnp.exp(v2 - v1)
    w1 = p_sel / (1.0 + s)
    w2 = p_sel * s / (1.0 + s)
    e1 = i1 - float(ROUTER_LANE0)
    e2 = i2 - float(ROUTER_LANE0)

    oh1 = jnp.where(lane_f == e1, 1.0, 0.0)
    oh2 = jnp.where(lane_f == e2, 1.0, 0.0)
    both = oh1 + oh2
    ri = lax.broadcasted_iota(jnp.int32, (tm, tm), 0)
    ci = lax.broadcasted_iota(jnp.int32, (tm, tm), 1)
    tri = jnp.where(ri > ci, 1.0, 0.0).astype(BF16)
    prior = jnp.dot(tri, both.astype(BF16), preferred_element_type=F32) + carry_ref[...]
    r1 = jnp.sum(prior * oh1, axis=-1, keepdims=True)
    r2 = jnp.sum(prior * oh2, axis=-1, keepdims=True)
    carry = carry_ref[...] + jnp.sum(both, axis=0, keepdims=True)
    carry_ref[...] = carry
    cnt_ref[...] = carry

    route = jnp.zeros((tm, LANES), F32)
    for ln, val in ((ROUTE_E1, e1), (ROUTE_E2, e2), (ROUTE_W1, w1), (ROUTE_W2, w2),
                    (ROUTE_R1, r1), (ROUTE_R2, r2)):
        route = jnp.where(lane == ln, val, route)
    route_ref[...] = route


def _merge(y_dn, proj, mp, x2d, w_up_dn, w_out, gain, w_router, b_router, *, tm):
    t, d = x2d.shape
    dv = y_dn.shape[1]
    rows = lambda blk: pl.BlockSpec(blk, lambda i: (i, 0))
    return pl.pallas_call(
        _merge_kernel,
        grid=(t // tm,),
        in_specs=[
            rows((tm, dv)),
            pl.BlockSpec((tm, d), lambda i: (i, COL_GATE_DN // d)),
            rows((tm, d)), rows((tm, d)),
            _resident(w_up_dn.shape), _resident(w_out.shape), _resident((1, d)),
            _resident(w_router.shape), _resident((1, LANES)),
        ],
        out_specs=[rows((tm, d)), rows((tm, d // 2)), rows((tm, LANES)),
                   pl.BlockSpec((1, LANES), lambda i: (0, 0))],
        out_shape=[jax.ShapeDtypeStruct((t, d), F32), jax.ShapeDtypeStruct((t, d // 2), jnp.uint32),
                   jax.ShapeDtypeStruct((t, LANES), F32), jax.ShapeDtypeStruct((1, LANES), F32)],
        scratch_shapes=[pltpu.VMEM((1, LANES), F32)],
        compiler_params=_cparams(("arbitrary",)),
        name="merge",
    )(y_dn, proj, mp, x2d, w_up_dn, w_out, gain, w_router, b_router)


def _experts_kernel(be_ref, nu_ref, pos_ref, n2_ref, wg_ref, wu_ref, wd_ref, y_ref,
                    xbuf0, xbuf1, tok_ref, sem):
    del be_ref
    bm = xbuf0.shape[0]
    i = pl.program_id(0)
    nu = nu_ref[0]

    @pl.when(i == 0)
    def _():
        def clear(r, _):
            tok_ref[r] = 0
            return 0
        lax.fori_loop(0, tok_ref.shape[0], clear, 0, unroll=8)

        def fill(a, _):
            tok_ref[pos_ref[a]] = a // 2
            return 0
        lax.fori_loop(0, pos_ref.shape[0], fill, 0, unroll=8)

    def issue(blk, buf, slot):
        for j in range(bm):
            tok = tok_ref[blk * bm + j]
            pltpu.make_async_copy(n2_ref.at[pl.ds(tok, 1)], buf.at[pl.ds(j, 1)],
                                  sem.at[slot]).start(priority=j % 2)

    def drain(buf, slot):
        pltpu.make_async_copy(n2_ref.at[pl.ds(0, bm)], buf, sem.at[slot]).wait()

    def live_step(cur, nxt, slot):
        drain(cur, slot)
        issue(i + 1, nxt, 1 - slot)
        xb = _unpack_rows(cur[...])
        g = jnp.dot(xb, wg_ref[0], preferred_element_type=F32)
        u = jnp.dot(xb, wu_ref[0], preferred_element_type=F32)
        hmid = (g * jax.nn.sigmoid(g)) * u
        y_ref[...] = _pack_rows(jnp.dot(hmid.astype(BF16), wd_ref[0], preferred_element_type=F32))

    @pl.when(i == 0)
    def _():
        issue(0, xbuf0, 0)

    @pl.when((i < nu) & (i % 2 == 0))
    def _():
        live_step(xbuf0, xbuf1, 0)

    @pl.when((i < nu) & (i % 2 == 1))
    def _():
        live_step(xbuf1, xbuf0, 1)

    @pl.when(i >= nu)
    def _():
        y_ref[...] = jnp.zeros(y_ref.shape, y_ref.dtype)

    @pl.when((i == nu) & (i % 2 == 0))
    def _():
        drain(xbuf0, 0)

    @pl.when((i == nu) & (i % 2 == 1))
    def _():
        drain(xbuf1, 1)


def _experts(blk_e, n_used, pos, n2, w_gate, w_up, w_down, *, bm, cap):
    dp = n2.shape[1]
    d, de = w_gate.shape[1], w_gate.shape[2]
    wspec = lambda shape: pl.BlockSpec(shape, lambda i, be, nu, pos: (be[i], 0, 0))
    return pl.pallas_call(
        _experts_kernel,
        grid_spec=pltpu.PrefetchScalarGridSpec(
            num_scalar_prefetch=3,
            grid=(cap // bm,),
            in_specs=[pl.BlockSpec(memory_space=pl.ANY),
                      wspec((1, d, de)), wspec((1, d, de)), wspec((1, de, d))],
            out_specs=pl.BlockSpec((bm, dp), lambda i, be, nu, pos: (i, 0)),
            scratch_shapes=[pltpu.VMEM((bm, dp), jnp.uint32), pltpu.VMEM((bm, dp), jnp.uint32),
                            pltpu.SMEM((cap,), jnp.int32), pltpu.SemaphoreType.DMA((2,))],
        ),
        out_shape=jax.ShapeDtypeStruct((cap, dp), jnp.uint32),
        compiler_params=_cparams(("arbitrary",)),
        name="experts",
    )(blk_e, n_used, pos, n2, w_gate, w_up, w_down)


def _combine_kernel(pos_ref, h1_ref, route_ref, p_ref, ys_ref, wpg_ref, wpp_ref, gple_ref, gfin_ref,
                    out_ref, ybuf0, ybuf1, sem, *, final_norm, n_steps):
    tm = h1_ref.shape[0]
    i = pl.program_id(0)

    def issue(blk, buf, slot):
        for j in range(tm):
            for k in range(2):
                src = pos_ref[2 * (blk * tm + j) + k]
                pltpu.make_async_copy(ys_ref.at[pl.ds(src, 1)], buf.at[pl.ds(k * tm + j, 1)],
                                      sem.at[slot]).start(priority=k)

    def step(cur, nxt, slot):
        pltpu.make_async_copy(ys_ref.at[pl.ds(0, 2 * tm)], cur, sem.at[slot]).wait()
        if nxt is not None:
            issue(i + 1, nxt, 1 - slot)
        route = route_ref[...]
        y_moe = (_unpack_rows(cur[0:tm, :]).astype(F32) * route[:, ROUTE_W1:ROUTE_W1 + 1]
                 + _unpack_rows(cur[tm:2 * tm, :]).astype(F32) * route[:, ROUTE_W2:ROUTE_W2 + 1])
        h2 = h1_ref[...] + y_moe
        n3 = h2 * lax.rsqrt(jnp.mean(h2 * h2, axis=-1, keepdims=True) + EPS) * gple_ref[...]
        gate = jax.nn.sigmoid(jnp.dot(n3.astype(BF16), wpg_ref[...], preferred_element_type=F32))
        h3 = h2 + gate * jnp.dot(p_ref[...].astype(BF16), wpp_ref[...], preferred_element_type=F32)
        if final_norm:
            h3 = h3 * lax.rsqrt(jnp.mean(h3 * h3, axis=-1, keepdims=True) + EPS) * gfin_ref[...]
        out_ref[...] = h3

    @pl.when(i == 0)
    def _():
        issue(0, ybuf0, 0)

    last = n_steps - 1
    bufs = (ybuf0, ybuf1)

    @pl.when((i < last) & (i % 2 == 0))
    def _():
        step(ybuf0, ybuf1, 0)

    @pl.when((i < last) & (i % 2 == 1))
    def _():
        step(ybuf1, ybuf0, 1)

    @pl.when(i == last)
    def _():
        step(bufs[last % 2], None, last % 2)


def _combine(pos, h1, route, p2d, ys, w_ple_gate, w_ple_proj, g_ple, g_fin, *, tm, final_norm):
    t, d = h1.shape
    rows = lambda w: pl.BlockSpec((tm, w), lambda i, pos: (i, 0))
    res = lambda shape: pl.BlockSpec(shape, lambda i, pos: (0,) * len(shape), pipeline_mode=pl.Buffered(1))
    return pl.pallas_call(
        functools.partial(_combine_kernel, final_norm=final_norm, n_steps=t // tm),
        grid_spec=pltpu.PrefetchScalarGridSpec(
            num_scalar_prefetch=1,
            grid=(t // tm,),
            in_specs=[rows(d), rows(LANES), rows(p2d.shape[1]), pl.BlockSpec(memory_space=pl.ANY),
                      res(w_ple_gate.shape), res(w_ple_proj.shape), res((1, d)), res((1, d))],
            out_specs=rows(d),
            scratch_shapes=[pltpu.VMEM((2 * tm, ys.shape[1]), ys.dtype),
                            pltpu.VMEM((2 * tm, ys.shape[1]), ys.dtype),
                            pltpu.SemaphoreType.DMA((2,))],
        ),
        out_shape=jax.ShapeDtypeStruct((t, d), F32),
        compiler_params=_cparams(("arbitrary",)),
        name="combine",
    )(pos, h1, route, p2d, ys, w_ple_gate, w_ple_proj, g_ple, g_fin)


def _pick(n, pref):
    b = min(n, pref)
    while n % b:
        b -= 8
    return b


def _stage_mixers(x, prm, i):
    bsz, seq, d = x.shape
    t = bsz * seq
    x2d = x.reshape(t, d)
    w_in = prm["w_in"][i]
    d_pool = prm["w_up_pool"].shape[1]
    dq = DN_HEADS * DN_DK
    o_qkv = d_pool
    o_z = o_qkv + 3 * dq
    o_b = o_z + dq
    o_gp = o_b + 2 * DN_HEADS
    o_gd = o_gp + d
    w_main = jnp.concatenate(
        [w_in[:, o_gp:o_gp + d], w_in[:, o_gd:o_gd + d], w_in[:, :d_pool],
         w_in[:, o_qkv:o_qkv + 3 * dq], w_in[:, o_z:o_z + dq]], axis=1).astype(BF16)
    assert w_main.shape[1] == N_MAIN
    w_ba = jnp.pad(w_in[:, o_b:o_b + 2 * DN_HEADS], ((0, 0), (0, LANES - 2 * DN_HEADS))).astype(BF16)
    proj, ba = _inproj(x2d, prm["norm_mix"][i][None, :], w_main, w_ba,
                       tm=_pick(t, 1024), tn=1024)

    ts = _pick(seq, 512)
    mp = _pool(proj, prm["pool_w"][i].astype(BF16), prm["pool_scale"][i][None, :],
               prm["w_up_pool"][i].astype(BF16), bsz=bsz, seq=seq, ts=ts)

    lane_pad = lambda a: jnp.pad(a, (DN_HEADS, LANES - 2 * DN_HEADS))[None, :]
    qn, kn, vv, bx, gx, gt = _prep(proj, ba, prm["conv_w"][i], lane_pad(prm["a_log"][i]),
                                   lane_pad(prm["dt_bias"][i]), bsz=bsz, seq=seq, ts=ts)
    u, w, qg, kd, at, egl = _wy(qn, kn, vv, bx, gx, gt, ts=ts)
    y_dn = _scan(u, w, qg, kd, at, egl, proj, prm["dn_norm"][i][None, :], bsz=bsz, seq=seq, ts=ts)
    return dict(proj=proj, ba=ba, mp=mp, qn=qn, kn=kn, vv=vv, bx=bx, gx=gx, gt=gt, y_dn=y_dn)


MOE_BM = 256


def _router_weights(w_rg, b_rg, w_re, b_re):
    d = w_rg.shape[0]
    w = jnp.concatenate([w_rg, w_re, jnp.zeros((d, LANES - N_GROUPS - N_EXPERTS), F32)], axis=1)
    w_hi = w.astype(BF16)
    w_lo = (w - w_hi.astype(F32)).astype(BF16)
    b = jnp.concatenate([b_rg, b_re, jnp.zeros((LANES - N_GROUPS - N_EXPERTS,), F32)])[None, :]
    return jnp.concatenate([w_hi, w_lo], axis=1), b


def _stage_moe_plan(route, cnt, t):
    counts = cnt[0, :N_EXPERTS].astype(jnp.int32)
    padded = ((counts + MOE_BM - 1) // MOE_BM) * MOE_BM
    pends = jnp.cumsum(padded)
    pstarts = pends - padded
    eid = route[:, ROUTE_E1:ROUTE_E2 + 1].astype(jnp.int32)
    rank = route[:, ROUTE_R1:ROUTE_R2 + 1].astype(jnp.int32)
    pos = (pstarts[eid] + rank).reshape(2 * t)
    cap = 2 * t + N_EXPERTS * MOE_BM
    n_blocks = cap // MOE_BM
    n_used = (pends[-1] // MOE_BM).astype(jnp.int32)
    blk = jnp.minimum(jnp.arange(n_blocks, dtype=jnp.int32), n_used - 1)
    blk_e = jnp.sum((pends[None, :] <= (blk * MOE_BM)[:, None]).astype(jnp.int32), axis=1)
    blk_e = jnp.minimum(blk_e, N_EXPERTS - 1)
    return pos, blk_e, n_used.reshape(1), cap


def kernel(x, p, norm_mix, w_in, pool_w, pool_scale, conv_w, a_log, dt_bias, dn_norm, w_up_pool, w_up_dn, w_out, norm_moe, w_router_group, b_router_group, w_router_expert, b_router_expert, w_gate, w_up, w_down, norm_ple, w_ple_gate, w_ple_proj, norm_final):
    bsz, seq, d = x.shape
    t = bsz * seq
    depth = norm_mix.shape[0]
    prm = dict(norm_mix=norm_mix, w_in=w_in, pool_w=pool_w, pool_scale=pool_scale, conv_w=conv_w,
               a_log=a_log, dt_bias=dt_bias, dn_norm=dn_norm, w_up_pool=w_up_pool)
    h = x
    for i in range(depth):
        st = _stage_mixers(h, prm, i)
        w_router, b_router = _router_weights(w_router_group[i], b_router_group[i],
                                             w_router_expert[i], b_router_expert[i])
        tm = _pick(t, 256)
        h1, n2, route, cnt = _merge(st["y_dn"], st["proj"], st["mp"], h.reshape(t, d),
                                    w_up_dn[i].astype(BF16), w_out[i].astype(BF16), norm_moe[i][None, :],
                                    w_router, b_router, tm=tm)
        pos, blk_e, n_used, cap = _stage_moe_plan(route, cnt, t)
        ys = _experts(blk_e, n_used, pos, n2, w_gate[i].astype(BF16), w_up[i].astype(BF16),
                      w_down[i].astype(BF16), bm=MOE_BM, cap=cap)
        out = _combine(pos, h1, route, p[i].reshape(t, -1), ys, w_ple_gate[i].astype(BF16),
                       w_ple_proj[i].astype(BF16), norm_ple[i][None, :], norm_final[None, :],
                       tm=tm, final_norm=(i == depth - 1))
        h = out.reshape(bsz, seq, d)
    return h
```

```python
import functools

import jax
import jax.numpy as jnp
from jax import lax
from jax.experimental import pallas as pl
from jax.experimental.pallas import tpu as pltpu

F32 = jnp.float32
BF16 = jnp.bfloat16

EPS = 1e-6
POOL_WINDOWS = (2, 4, 8, 16)
POOL_HALO = 16
CONV_K = 4
CONV_HALO = 8
DN_HEADS = 8
DN_DK = 128
CHUNK = 64
N_GROUPS = 4
EXPERTS_PER_GROUP = 8
N_EXPERTS = N_GROUPS * EXPERTS_PER_GROUP
LANES = 128
VMEM_LIMIT = 56 * 1024 * 1024

COL_GATE_POOL = 0
COL_GATE_DN = 2048
COL_POOL = 4096
COL_Q = 5120
COL_K = 6144
COL_V = 7168
COL_Z = 8192
N_MAIN = 9216


def _cparams(sem):
    return pltpu.CompilerParams(dimension_semantics=sem, vmem_limit_bytes=VMEM_LIMIT)


def _resident(shape):
    nd = len(shape)
    return pl.BlockSpec(shape, lambda *_: (0,) * nd, pipeline_mode=pl.Buffered(1))


def _inproj_kernel(x_ref, gain_ref, w_ref, wba_ref, out_ref, ba_ref, n1_ref, *, sub):
    tm = x_ref.shape[0]

    @pl.when(pl.program_id(1) == 0)
    def _():
        def body(r, _):
            rows = pl.ds(pl.multiple_of(r * sub, sub), sub)
            x = x_ref[rows, :]
            y = x * lax.rsqrt(jnp.mean(x * x, axis=-1, keepdims=True) + EPS) * gain_ref[...]
            n1_ref[rows, :] = y.astype(BF16)
            return 0
        lax.fori_loop(0, tm // sub, body, 0)
        ba_ref[...] = jnp.dot(n1_ref[...], wba_ref[...], preferred_element_type=F32)

    out_ref[...] = jnp.dot(n1_ref[...], w_ref[...], preferred_element_type=F32)


def _inproj(x2d, gain, w_main, w_ba, *, tm, tn):
    t, d = x2d.shape
    n = w_main.shape[1]
    return pl.pallas_call(
        functools.partial(_inproj_kernel, sub=min(tm, 256)),
        grid=(t // tm, n // tn),
        in_specs=[
            pl.BlockSpec((tm, d), lambda i, j: (i, 0)),
            _resident((1, d)),
            pl.BlockSpec((d, tn), lambda i, j: (0, j)),
            _resident((d, LANES)),
        ],
        out_specs=[
            pl.BlockSpec((tm, tn), lambda i, j: (i, j)),
            pl.BlockSpec((tm, LANES), lambda i, j: (i, 0)),
        ],
        out_shape=[jax.ShapeDtypeStruct((t, n), F32), jax.ShapeDtypeStruct((t, LANES), F32)],
        scratch_shapes=[pltpu.VMEM((tm, d), BF16)],
        compiler_params=_cparams(("parallel", "arbitrary")),
        name="inproj",
    )(x2d, gain, w_main, w_ba)


def _pool_kernel(u_ref, gp_ref, pw_ref, scale_ref, wup_ref, out_ref, ext_ref):
    ts = u_ref.shape[0]
    gw = pw_ref.shape[1]
    s = pl.program_id(1)

    @pl.when(s == 0)
    def _():
        ext_ref[0:POOL_HALO, :] = jnp.zeros((POOL_HALO, ext_ref.shape[1]), F32)

    @pl.when(s > 0)
    def _():
        ext_ref[0:POOL_HALO, :] = ext_ref[ts:ts + POOL_HALO, :]

    ext_ref[POOL_HALO:POOL_HALO + ts, :] = u_ref[...]

    t1 = (s * ts + 1 + lax.broadcasted_iota(jnp.int32, (ts, 1), 0)).astype(F32)
    acc = jnp.zeros(out_ref.shape, F32)
    for g, w in enumerate(POOL_WINDOWS):
        cols = slice(g * gw, (g + 1) * gw)
        cur = ext_ref[POOL_HALO:POOL_HALO + ts, cols]
        win = cur
        for j in range(1, w):
            win = win + ext_ref[POOL_HALO - j:POOL_HALO - j + ts, cols]
        d = win / jnp.minimum(t1, float(w)) - cur
        y = jnp.dot(d.astype(BF16), pw_ref[g], preferred_element_type=F32) * scale_ref[:, cols]
        acc = acc + jnp.dot(y.astype(BF16), wup_ref[cols, :], preferred_element_type=F32)
    out_ref[...] = jax.nn.sigmoid(gp_ref[...]) * acc


def _pool(proj, pool_w, pool_scale, w_up_pool, *, bsz, seq, ts):
    t = proj.shape[0]
    ns = seq // ts
    d_pool = w_up_pool.shape[0]
    d = w_up_pool.shape[1]
    return pl.pallas_call(
        _pool_kernel,
        grid=(bsz, ns),
        in_specs=[
            pl.BlockSpec((ts, d_pool), lambda b, s: (b * ns + s, COL_POOL // d_pool)),
            pl.BlockSpec((ts, d), lambda b, s: (b * ns + s, COL_GATE_POOL // d)),
            _resident(pool_w.shape),
            _resident((1, d_pool)),
            _resident(w_up_pool.shape),
        ],
        out_specs=pl.BlockSpec((ts, d), lambda b, s: (b * ns + s, 0)),
        out_shape=jax.ShapeDtypeStruct((t, d), F32),
        scratch_shapes=[pltpu.VMEM((POOL_HALO + ts, d_pool), F32)],
        compiler_params=_cparams(("parallel", "arbitrary")),
        name="pool",
    )(proj, proj, pool_w, pool_scale, w_up_pool)


def _prep_kernel(q_ref, k_ref, v_ref, ba_ref, cw_ref, alog_ref, dtb_ref,
                 qn_ref, kn_ref, vv_ref, bx_ref, gx_ref, gt_ref, ext_ref):
    ts = q_ref.shape[0]
    dq = q_ref.shape[1]
    s = pl.program_id(1)

    @pl.when(s == 0)
    def _():
        ext_ref[0:CONV_HALO, :] = jnp.zeros((CONV_HALO, ext_ref.shape[1]), F32)

    @pl.when(s > 0)
    def _():
        ext_ref[0:CONV_HALO, :] = ext_ref[ts:ts + CONV_HALO, :]

    ext_ref[CONV_HALO:CONV_HALO + ts, 0:dq] = q_ref[...]
    ext_ref[CONV_HALO:CONV_HALO + ts, dq:2 * dq] = k_ref[...]
    ext_ref[CONV_HALO:CONV_HALO + ts, 2 * dq:3 * dq] = v_ref[...]

    def conv_silu(c0, width):
        cols = slice(c0, c0 + width)
        first = CONV_HALO - (CONV_K - 1)
        y = ext_ref[first:first + ts, cols] * cw_ref[0:1, cols]
        for tap in range(1, CONV_K):
            y = y + ext_ref[first + tap:first + tap + ts, cols] * cw_ref[tap:tap + 1, cols]
        return y * jax.nn.sigmoid(y)

    for h in range(DN_HEADS):
        hc = slice(h * DN_DK, (h + 1) * DN_DK)
        qh = conv_silu(h * DN_DK, DN_DK)
        qn_ref[:, hc] = qh * (lax.rsqrt(jnp.sum(qh * qh, axis=-1, keepdims=True) + EPS) * (DN_DK ** -0.5))
        kh = conv_silu(dq + h * DN_DK, DN_DK)
        kn_ref[:, hc] = kh * lax.rsqrt(jnp.sum(kh * kh, axis=-1, keepdims=True) + EPS)
        vv_ref[:, hc] = conv_silu(2 * dq + h * DN_DK, DN_DK)

    ba = ba_ref[...]
    beta = jax.nn.sigmoid(ba)
    xs = ba + dtb_ref[...]
    softplus = jnp.maximum(xs, 0.0) + jnp.log1p(jnp.exp(-jnp.abs(xs)))
    g = -jnp.exp(alog_ref[...]) * softplus
    ri = lax.broadcasted_iota(jnp.int32, (ts, ts), 0)
    ci = lax.broadcasted_iota(jnp.int32, (ts, ts), 1)
    tri = jnp.where((ri // CHUNK == ci // CHUNK) & (ci <= ri), 1.0, 0.0).astype(F32)
    gcum = jnp.dot(tri, g, preferred_element_type=F32, precision=lax.Precision.HIGHEST)
    gt_ref[0] = gcum.T[DN_HEADS:2 * DN_HEADS, :]
    for h in range(DN_HEADS):
        hc = slice(h * DN_DK, (h + 1) * DN_DK)
        bx_ref[:, hc] = jnp.broadcast_to(beta[:, h:h + 1], (ts, DN_DK))
        gx_ref[:, hc] = jnp.broadcast_to(gcum[:, DN_HEADS + h:DN_HEADS + h + 1], (ts, DN_DK))


def _prep(proj, ba, conv_w, alog_pad, dtb_pad, *, bsz, seq, ts):
    t = proj.shape[0]
    ns = seq // ts
    dq = DN_HEADS * DN_DK
    row = lambda b, s: b * ns + s
    blk = lambda col: pl.BlockSpec((ts, dq), lambda b, s: (row(b, s), col // dq))
    out_blk = pl.BlockSpec((ts, dq), lambda b, s: (row(b, s), 0))
    return pl.pallas_call(
        _prep_kernel,
        grid=(bsz, ns),
        in_specs=[
            blk(COL_Q), blk(COL_K), blk(COL_V),
            pl.BlockSpec((ts, LANES), lambda b, s: (row(b, s), 0)),
            _resident(conv_w.shape),
            _resident((1, LANES)),
            _resident((1, LANES)),
        ],
        out_specs=[out_blk, out_blk, out_blk, out_blk, out_blk,
                   pl.BlockSpec((1, DN_HEADS, ts), lambda b, s: (row(b, s), 0, 0))],
        out_shape=[jax.ShapeDtypeStruct((t, dq), F32)] * 5
                  + [jax.ShapeDtypeStruct((t // ts, DN_HEADS, ts), F32)],
        scratch_shapes=[pltpu.VMEM((CONV_HALO + ts, 3 * dq), F32)],
        compiler_params=_cparams(("parallel", "arbitrary")),
        name="prep",
    )(proj, proj, proj, ba, conv_w, alog_pad, dtb_pad)


def _dot_nt(a, b):
    return lax.dot_general(a, b, (((1,), (1,)), ((), ())), preferred_element_type=F32)


def _dot_tn(a, b):
    return lax.dot_general(a, b, (((0,), (0,)), ((), ())), preferred_element_type=F32)


GROUP = 256
PAIR = 2 * CHUNK


def _wy_kernel(q_ref, k_ref, v_ref, bx_ref, gx_ref, gt_ref,
               u_ref, w_ref, qg_ref, kd_ref, at_ref, egl_ref):
    ts = q_ref.shape[0]
    ii = lax.broadcasted_iota(jnp.int32, (GROUP, GROUP), 0)
    jj = lax.broadcasted_iota(jnp.int32, (GROUP, GROUP), 1)
    same = (ii // CHUNK) == (jj // CHUNK)
    incl = same & (ii >= jj)
    strict = same & (ii > jj)

    items = [(slice(g * GROUP, (g + 1) * GROUP), hh) for hh in range(q_ref.shape[1] // DN_DK)
             for g in range(ts // GROUP)]
    lows, sols = [], []
    for rows, hh in items:
        hc = slice(hh * DN_DK, (hh + 1) * DN_DK)
        q = q_ref[rows, hc]
        k = k_ref[rows, hc]
        beta = bx_ref[rows, hc]
        gc = gx_ref[rows, hc]
        gr = gt_ref[0, hh:hh + 1, rows]
        decay = jnp.where(incl, jnp.exp(jnp.concatenate([gc, gc], axis=1) - gr), 0.0)
        kb = k * beta
        a = _dot_nt(jnp.concatenate([kb, q], axis=0).astype(BF16), k.astype(BF16))
        lows.append(jnp.where(strict, a[:GROUP] * decay, 0.0).astype(BF16))
        attn = a[GROUP:] * decay
        for j in range(GROUP // PAIR):
            pr = slice(j * PAIR, (j + 1) * PAIR)
            at_ref[rows.start + j * PAIR:rows.start + (j + 1) * PAIR, hc] = attn[pr, pr].astype(BF16)
        egc = jnp.exp(gc)
        qg_ref[rows, hc] = (q * egc).astype(BF16)
        lasts = [gc[(c + 1) * CHUNK - 1:(c + 1) * CHUNK, :] for c in range(GROUP // CHUNK)]
        for c, g_last in enumerate(lasts):
            row = rows.start // CHUNK + c
            egl_ref[row:row + 1, hc] = jnp.exp(g_last)
        g_last_rows = jnp.concatenate([jnp.broadcast_to(gl, (CHUNK, DN_DK)) for gl in lasts], axis=0)
        kd_ref[rows, hc] = (k * jnp.exp(g_last_rows - gc)).astype(BF16)
        sols.append(jnp.concatenate([v_ref[rows, hc] * beta, kb * egc], axis=1))

    mm = lambda a16, b: jnp.dot(a16, b.astype(BF16), preferred_element_type=F32)
    sols = [s - mm(l, s) for l, s in zip(lows, sols)]
    pws = lows
    for _ in range(CHUNK.bit_length() - 2):
        pws = [mm(p, p).astype(BF16) for p in pws]
        sols = [s + mm(p, s) for p, s in zip(pws, sols)]
    for (rows, hh), s in zip(items, sols):
        hc = slice(hh * DN_DK, (hh + 1) * DN_DK)
        u_ref[rows, hc] = s[:, :DN_DK]
        w_ref[rows, hc] = s[:, DN_DK:].astype(BF16)


WY_HEADS = 2


def _wy(qn, kn, vv, bx, gx, gt, *, ts):
    t = qn.shape[0]
    hw = WY_HEADS * DN_DK
    hblk = pl.BlockSpec((ts, hw), lambda r, h: (r, h))
    wide = jax.ShapeDtypeStruct((t, DN_HEADS * DN_DK), BF16)
    return pl.pallas_call(
        _wy_kernel,
        grid=(t // ts, DN_HEADS // WY_HEADS),
        in_specs=[hblk, hblk, hblk, hblk, hblk,
                  pl.BlockSpec((1, WY_HEADS, ts), lambda r, h: (r * (DN_HEADS // WY_HEADS) + h, 0, 0))],
        out_specs=[hblk, hblk, hblk, hblk, hblk,
                   pl.BlockSpec((ts // CHUNK, hw), lambda r, h: (r, h))],
        out_shape=[jax.ShapeDtypeStruct((t, DN_HEADS * DN_DK), F32), wide, wide, wide, wide,
                   jax.ShapeDtypeStruct((t // CHUNK, DN_HEADS * DN_DK), F32)],
        compiler_params=_cparams(("parallel", "parallel")),
        name="wy",
    )(qn, kn, vv, bx, gx, gt.reshape(-1, WY_HEADS, ts))


def _scan_kernel(u_ref, w_ref, qg_ref, kd_ref, at_ref, egl_ref, z_ref, dnw_ref, out_ref, state_ref):
    ts = u_ref.shape[0]

    @pl.when(pl.program_id(1) == 0)
    def _():
        state_ref[...] = jnp.zeros(state_ref.shape, F32)

    even_vn = [None] * DN_HEADS
    for c in range(ts // CHUNK):
        rows = slice(c * CHUNK, (c + 1) * CHUNK)
        for h in range(DN_HEADS):
            hc = slice(h * DN_DK, (h + 1) * DN_DK)
            state = state_ref[h]
            wq = jnp.concatenate([w_ref[rows, hc], qg_ref[rows, hc]], axis=0)
            r = jnp.dot(wq, state.astype(BF16), preferred_element_type=F32)
            vn16 = (u_ref[rows, hc] - r[:CHUNK]).astype(BF16)
            if c % 2 == 0:
                even_vn[h] = vn16
                pair = jnp.concatenate([vn16, jnp.zeros_like(vn16)], axis=0)
            else:
                pair = jnp.concatenate([even_vn[h], vn16], axis=0)
            o = r[CHUNK:] + jnp.dot(at_ref[rows, hc], pair, preferred_element_type=F32)
            state_ref[h] = state * egl_ref[c:c + 1, hc] + _dot_tn(kd_ref[rows, hc], vn16)
            on = o * lax.rsqrt(jnp.mean(o * o, axis=-1, keepdims=True) + EPS) * dnw_ref[...]
            z = z_ref[rows, hc]
            out_ref[rows, hc] = (on * (z * jax.nn.sigmoid(z))).astype(out_ref.dtype)


def _scan(u, w, qg, kd, at, egl, proj, dn_norm, *, bsz, seq, ts):
    t, dv = u.shape
    ns = seq // ts
    blk = pl.BlockSpec((ts, dv), lambda b, s: (b * ns + s, 0))
    return pl.pallas_call(
        _scan_kernel,
        grid=(bsz, ns),
        in_specs=[blk, blk, blk, blk, blk,
                  pl.BlockSpec((ts // CHUNK, dv), lambda b, s: (b * ns + s, 0)),
                  pl.BlockSpec((ts, dv), lambda b, s: (b * ns + s, COL_Z // dv)),
                  _resident((1, DN_DK))],
        out_specs=blk,
        out_shape=jax.ShapeDtypeStruct((t, dv), BF16),
        scratch_shapes=[pltpu.VMEM((DN_HEADS, DN_DK, DN_DK), F32)],
        compiler_params=_cparams(("parallel", "arbitrary")),
        name="scan",
    )(u, w, qg, kd, at, egl, proj, dn_norm)


ROUTE_E1, ROUTE_E2, ROUTE_W1, ROUTE_W2, ROUTE_R1, ROUTE_R2 = range(6)
ROUTER_LANE0 = N_GROUPS
NEG_BIG = -1e30


def _pack_rows(x):
    n = x.shape[1] // 2
    bits = lambda v: lax.bitcast_convert_type(v.astype(BF16).astype(F32), jnp.uint32)
    return bits(x[:, n:]) | (bits(x[:, :n]) >> 16)


def _unpack_rows(p):
    lo = lax.bitcast_convert_type(p << 16, F32).astype(BF16)
    hi = lax.bitcast_convert_type(p & jnp.uint32(0xFFFF0000), F32).astype(BF16)
    return jnp.concatenate([lo, hi], axis=1)


def _merge_kernel(ydn_ref, gd_ref, mp_ref, x_ref, wupdn_ref, wout_ref, gain_ref, wr_ref, br_ref,
                  h1_ref, n2_ref, route_ref, cnt_ref, carry_ref):
    tm = x_ref.shape[0]

    @pl.when(pl.program_id(0) == 0)
    def _():
        carry_ref[...] = jnp.zeros(carry_ref.shape, F32)

    md = jax.nn.sigmoid(gd_ref[...]) * jnp.dot(ydn_ref[...], wupdn_ref[...], preferred_element_type=F32)
    merged = mp_ref[...] + md
    h1 = x_ref[...] + jnp.dot(merged.astype(BF16), wout_ref[...], preferred_element_type=F32)
    h1_ref[...] = h1
    n2 = h1 * lax.rsqrt(jnp.mean(h1 * h1, axis=-1, keepdims=True) + EPS) * gain_ref[...]
    n2_ref[...] = _pack_rows(n2)

    hi = n2.astype(BF16)
    lo = (n2 - hi.astype(F32)).astype(BF16)
    r_hi = jnp.dot(hi, wr_ref[...], preferred_element_type=F32)
    r_lo = jnp.dot(lo, wr_ref[:, :LANES], preferred_element_type=F32)
    logits = r_hi[:, :LANES] + r_hi[:, LANES:] + r_lo + br_ref[...]

    lane = lax.broadcasted_iota(jnp.int32, (tm, LANES), 1)
    lane_f = lane.astype(F32)
    first_max = lambda v, m: jnp.min(jnp.where(v == m, lane_f, float(LANES)), axis=-1, keepdims=True)

    is_grp = lane < N_GROUPS
    lg = jnp.where(is_grp, logits, NEG_BIG)
    gmax = jnp.max(lg, axis=-1, keepdims=True)
    gi = first_max(lg, gmax)
    p_sel = 1.0 / jnp.sum(jnp.where(is_grp, jnp.exp(lg - gmax), 0.0), axis=-1, keepdims=True)

    e_lane = lane - ROUTER_LANE0
    in_grp = (e_lane >= 0) & (e_lane < N_EXPERTS) & ((e_lane // EXPERTS_PER_GROUP).astype(F32) == gi)
    le = jnp.where(in_grp, logits, NEG_BIG)
    v1 = jnp.max(le, axis=-1, keepdims=True)
    i1 = first_max(le, v1)
    le2 = jnp.where(lane_f == i1, NEG_BIG, le)
    v2 = jnp.max(le2, axis=-1, keepdims=True)
    i2 = first_max(le2, v2)
    s = jnp.exp(v2 - v1)
    w1 = p_sel / (1.0 + s)
    w2 = p_sel * s / (1.0 + s)
    e1 = i1 - float(ROUTER_LANE0)
    e2 = i2 - float(ROUTER_LANE0)

    oh1 = jnp.where(lane_f == e1, 1.0, 0.0)
    oh2 = jnp.where(lane_f == e2, 1.0, 0.0)
    both = oh1 + oh2
    ri = lax.broadcasted_iota(jnp.int32, (tm, tm), 0)
    ci = lax.broadcasted_iota(jnp.int32, (tm, tm), 1)
    tri = jnp.where(ri > ci, 1.0, 0.0).astype(BF16)
    prior = jnp.dot(tri, both.astype(BF16), preferred_element_type=F32) + carry_ref[...]
    r1 = jnp.sum(prior * oh1, axis=-1, keepdims=True)
    r2 = jnp.sum(prior * oh2, axis=-1, keepdims=True)
    carry = carry_ref[...] + jnp.sum(both, axis=0, keepdims=True)
    carry_ref[...] = carry
    cnt_ref[...] = carry

    route = jnp.zeros((tm, LANES), F32)
    for ln, val in ((ROUTE_E1, e1), (ROUTE_E2, e2), (ROUTE_W1, w1), (ROUTE_W2, w2),
                    (ROUTE_R1, r1), (ROUTE_R2, r2)):
        route = jnp.where(lane == ln, val, route)
    route_ref[...] = route


def _merge(y_dn, proj, mp, x2d, w_up_dn, w_out, gain, w_router, b_router, *, tm):
    t, d = x2d.shape
    dv = y_dn.shape[1]
    rows = lambda blk: pl.BlockSpec(blk, lambda i: (i, 0))
    return pl.pallas_call(
        _merge_kernel,
        grid=(t // tm,),
        in_specs=[
            rows((tm, dv)),
            pl.BlockSpec((tm, d), lambda i: (i, COL_GATE_DN // d)),
            rows((tm, d)), rows((tm, d)),
            _resident(w_up_dn.shape), _resident(w_out.shape), _resident((1, d)),
            _resident(w_router.shape), _resident((1, LANES)),
        ],
        out_specs=[rows((tm, d)), rows((tm, d // 2)), rows((tm, LANES)),
                   pl.BlockSpec((1, LANES), lambda i: (0, 0))],
        out_shape=[jax.ShapeDtypeStruct((t, d), F32), jax.ShapeDtypeStruct((t, d // 2), jnp.uint32),
                   jax.ShapeDtypeStruct((t, LANES), F32), jax.ShapeDtypeStruct((1, LANES), F32)],
        scratch_shapes=[pltpu.VMEM((1, LANES), F32)],
        compiler_params=_cparams(("arbitrary",)),
        name="merge",
    )(y_dn, proj, mp, x2d, w_up_dn, w_out, gain, w_router, b_router)


N_GATHER_BUF = 3


def _experts_kernel(be_ref, nu_ref, tok_ref, n2_ref, wg_ref, wu_ref, wd_ref, y_ref,
                    xbuf0, xbuf1, xbuf2, wg16, wu16, wd16, sem):
    bufs = (xbuf0, xbuf1, xbuf2)
    bm = xbuf0.shape[0]
    i = pl.program_id(0)
    nu = nu_ref[0]

    @pl.when((i < nu) & ((i == 0) | (be_ref[i] != be_ref[jnp.maximum(i - 1, 0)])))
    def _():
        wg16[...] = wg_ref[0].astype(BF16)
        wu16[...] = wu_ref[0].astype(BF16)
        wd16[...] = wd_ref[0].astype(BF16)

    def issue(blk, slot):
        for j in range(bm):
            tok = tok_ref[blk * bm + j]
            pltpu.make_async_copy(n2_ref.at[pl.ds(tok, 1)], bufs[slot].at[pl.ds(j, 1)],
                                  sem.at[slot]).start(priority=j % 2)

    def drain(slot):
        pltpu.make_async_copy(n2_ref.at[pl.ds(0, bm)], bufs[slot], sem.at[slot]).wait()

    @pl.when(i == 0)
    def _():
        issue(0, 0)
        issue(1, 1)

    for r in range(N_GATHER_BUF):
        @pl.when((i < nu) & (i % N_GATHER_BUF == r))
        def _(r=r):
            drain(r)
            issue(i + 2, (r + 2) % N_GATHER_BUF)
            xb = _unpack_rows(bufs[r][...])
            g = jnp.dot(xb, wg16[...], preferred_element_type=F32)
            u = jnp.dot(xb, wu16[...], preferred_element_type=F32)
            hmid = (g * jax.nn.sigmoid(g)) * u
            y_ref[...] = _pack_rows(jnp.dot(hmid.astype(BF16), wd16[...], preferred_element_type=F32))

        @pl.when((i == nu) & (i % N_GATHER_BUF == r))
        def _(r=r):
            drain(r)
            drain((r + 1) % N_GATHER_BUF)

    @pl.when(i >= nu)
    def _():
        y_ref[...] = jnp.zeros(y_ref.shape, y_ref.dtype)


def _experts(blk_e, n_used, tok_pad, n2, w_gate, w_up, w_down, *, bm, cap):
    dp = n2.shape[1]
    d, de = w_gate.shape[1], w_gate.shape[2]
    wspec = lambda shape: pl.BlockSpec(shape, lambda i, be, nu, tok: (be[i], 0, 0))
    xbuf = pltpu.VMEM((bm, dp), jnp.uint32)
    return pl.pallas_call(
        _experts_kernel,
        grid_spec=pltpu.PrefetchScalarGridSpec(
            num_scalar_prefetch=3,
            grid=(cap // bm,),
            in_specs=[pl.BlockSpec(memory_space=pl.ANY),
                      wspec((1, d, de)), wspec((1, d, de)), wspec((1, de, d))],
            out_specs=pl.BlockSpec((bm, dp), lambda i, be, nu, tok: (i, 0)),
            scratch_shapes=[xbuf, xbuf, xbuf, pltpu.VMEM((d, de), BF16), pltpu.VMEM((d, de), BF16),
                            pltpu.VMEM((de, d), BF16), pltpu.SemaphoreType.DMA((N_GATHER_BUF,))],
        ),
        out_shape=jax.ShapeDtypeStruct((cap, dp), jnp.uint32),
        compiler_params=_cparams(("arbitrary",)),
        name="experts",
    )(blk_e, n_used, tok_pad, n2, w_gate, w_up, w_down)


def _combine_kernel(pos_ref, h1_ref, route_ref, p_ref, ys_ref, wpg_ref, wpp_ref, gple_ref, gfin_ref,
                    out_ref, ybuf0, ybuf1, ybuf2, sem, *, final_norm, n_steps):
    bufs = (ybuf0, ybuf1, ybuf2)
    tm = h1_ref.shape[0]
    i = pl.program_id(0)

    def issue(blk, slot):
        for j in range(tm):
            for k in range(2):
                src = pos_ref[2 * (blk * tm + j) + k]
                pltpu.make_async_copy(ys_ref.at[pl.ds(src, 1)], bufs[slot].at[pl.ds(k * tm + j, 1)],
                                      sem.at[slot]).start(priority=k)

    def drain(slot):
        pltpu.make_async_copy(ys_ref.at[pl.ds(0, 2 * tm)], bufs[slot], sem.at[slot]).wait()

    @pl.when(i == 0)
    def _():
        issue(0, 0)
        issue(1, 1)

    for r in range(N_GATHER_BUF):
        @pl.when(i % N_GATHER_BUF == r)
        def _(r=r):
            drain(r)
            issue(i + 2, (r + 2) % N_GATHER_BUF)
            cur = bufs[r]
            route = route_ref[...]
            y_moe = (_unpack_rows(cur[0:tm, :]).astype(F32) * route[:, ROUTE_W1:ROUTE_W1 + 1]
                     + _unpack_rows(cur[tm:2 * tm, :]).astype(F32) * route[:, ROUTE_W2:ROUTE_W2 + 1])
            h2 = h1_ref[...] + y_moe
            n3 = h2 * lax.rsqrt(jnp.mean(h2 * h2, axis=-1, keepdims=True) + EPS) * gple_ref[...]
            gate = jax.nn.sigmoid(jnp.dot(n3.astype(BF16), wpg_ref[...], preferred_element_type=F32))
            h3 = h2 + gate * jnp.dot(p_ref[...].astype(BF16), wpp_ref[...], preferred_element_type=F32)
            if final_norm:
                h3 = h3 * lax.rsqrt(jnp.mean(h3 * h3, axis=-1, keepdims=True) + EPS) * gfin_ref[...]
            out_ref[...] = h3

    last = n_steps - 1

    @pl.when(i == last)
    def _():
        drain((last + 1) % N_GATHER_BUF)
        drain((last + 2) % N_GATHER_BUF)


def _combine(pos, h1, route, p2d, ys, w_ple_gate, w_ple_proj, g_ple, g_fin, *, tm, final_norm):
    t, d = h1.shape
    rows = lambda w: pl.BlockSpec((tm, w), lambda i, pos: (i, 0))
    res = lambda shape: pl.BlockSpec(shape, lambda i, pos: (0,) * len(shape), pipeline_mode=pl.Buffered(1))
    return pl.pallas_call(
        functools.partial(_combine_kernel, final_norm=final_norm, n_steps=t // tm),
        grid_spec=pltpu.PrefetchScalarGridSpec(
            num_scalar_prefetch=1,
            grid=(t // tm,),
            in_specs=[rows(d), rows(LANES), rows(p2d.shape[1]), pl.BlockSpec(memory_space=pl.ANY),
                      res(w_ple_gate.shape), res(w_ple_proj.shape), res((1, d)), res((1, d))],
            out_specs=rows(d),
            scratch_shapes=[pltpu.VMEM((2 * tm, ys.shape[1]), ys.dtype)] * N_GATHER_BUF
                           + [pltpu.SemaphoreType.DMA((N_GATHER_BUF,))],
        ),
        out_shape=jax.ShapeDtypeStruct((t, d), F32),
        compiler_params=_cparams(("arbitrary",)),
        name="combine",
    )(jnp.pad(pos, (0, 2 * 2 * tm)), h1, route, p2d, ys, w_ple_gate, w_ple_proj, g_ple, g_fin)


def _pick(n, pref):
    b = min(n, pref)
    while n % b:
        b -= 8
    return b


def _stage_mixers(x, prm, i):
    bsz, seq, d = x.shape
    t = bsz * seq
    x2d = x.reshape(t, d)
    w_in = prm["w_in"][i]
    d_pool = prm["w_up_pool"].shape[1]
    dq = DN_HEADS * DN_DK
    o_qkv = d_pool
    o_z = o_qkv + 3 * dq
    o_b = o_z + dq
    o_gp = o_b + 2 * DN_HEADS
    o_gd = o_gp + d
    w_main = jnp.concatenate(
        [w_in[:, o_gp:o_gp + d], w_in[:, o_gd:o_gd + d], w_in[:, :d_pool],
         w_in[:, o_qkv:o_qkv + 3 * dq], w_in[:, o_z:o_z + dq]], axis=1).astype(BF16)
    assert w_main.shape[1] == N_MAIN
    w_ba = jnp.pad(w_in[:, o_b:o_b + 2 * DN_HEADS], ((0, 0), (0, LANES - 2 * DN_HEADS))).astype(BF16)
    proj, ba = _inproj(x2d, prm["norm_mix"][i][None, :], w_main, w_ba,
                       tm=_pick(t, 1024), tn=1024)

    ts = _pick(seq, 512)
    mp = _pool(proj, prm["pool_w"][i].astype(BF16), prm["pool_scale"][i][None, :],
               prm["w_up_pool"][i].astype(BF16), bsz=bsz, seq=seq, ts=ts)

    lane_pad = lambda a: jnp.pad(a, (DN_HEADS, LANES - 2 * DN_HEADS))[None, :]
    qn, kn, vv, bx, gx, gt = _prep(proj, ba, prm["conv_w"][i], lane_pad(prm["a_log"][i]),
                                   lane_pad(prm["dt_bias"][i]), bsz=bsz, seq=seq, ts=ts)
    u, w, qg, kd, at, egl = _wy(qn, kn, vv, bx, gx, gt, ts=ts)
    y_dn = _scan(u, w, qg, kd, at, egl, proj, prm["dn_norm"][i][None, :], bsz=bsz, seq=seq, ts=ts)
    return dict(proj=proj, ba=ba, mp=mp, qn=qn, kn=kn, vv=vv, bx=bx, gx=gx, gt=gt, y_dn=y_dn)


MOE_BM = 256


def _router_weights(w_rg, b_rg, w_re, b_re):
    d = w_rg.shape[0]
    w = jnp.concatenate([w_rg, w_re, jnp.zeros((d, LANES - N_GROUPS - N_EXPERTS), F32)], axis=1)
    w_hi = w.astype(BF16)
    w_lo = (w - w_hi.astype(F32)).astype(BF16)
    b = jnp.concatenate([b_rg, b_re, jnp.zeros((LANES - N_GROUPS - N_EXPERTS,), F32)])[None, :]
    return jnp.concatenate([w_hi, w_lo], axis=1), b


def _stage_moe_plan(route, cnt, t):
    counts = cnt[0, :N_EXPERTS].astype(jnp.int32)
    padded = ((counts + MOE_BM - 1) // MOE_BM) * MOE_BM
    pends = jnp.cumsum(padded)
    pstarts = pends - padded
    eid = route[:, ROUTE_E1:ROUTE_E2 + 1].astype(jnp.int32)
    rank = route[:, ROUTE_R1:ROUTE_R2 + 1].astype(jnp.int32)
    pos = (pstarts[eid] + rank).reshape(2 * t)
    cap = 2 * t + N_EXPERTS * MOE_BM
    n_blocks = cap // MOE_BM
    n_used = (pends[-1] // MOE_BM).astype(jnp.int32)
    blk = jnp.minimum(jnp.arange(n_blocks, dtype=jnp.int32), n_used - 1)
    blk_e = jnp.sum((pends[None, :] <= (blk * MOE_BM)[:, None]).astype(jnp.int32), axis=1)
    blk_e = jnp.minimum(blk_e, N_EXPERTS - 1)
    tok_pad = jnp.zeros((cap + MOE_BM,), jnp.int32).at[pos].set(
        jnp.arange(2 * t, dtype=jnp.int32) // 2, unique_indices=True, mode="promise_in_bounds")
    return pos, blk_e, n_used.reshape(1), tok_pad, cap


def kernel(x, p, norm_mix, w_in, pool_w, pool_scale, conv_w, a_log, dt_bias, dn_norm, w_up_pool, w_up_dn, w_out, norm_moe, w_router_group, b_router_group, w_router_expert, b_router_expert, w_gate, w_up, w_down, norm_ple, w_ple_gate, w_ple_proj, norm_final):
    bsz, seq, d = x.shape
    t = bsz * seq
    depth = norm_mix.shape[0]
    prm = dict(norm_mix=norm_mix, w_in=w_in, pool_w=pool_w, pool_scale=pool_scale, conv_w=conv_w,
               a_log=a_log, dt_bias=dt_bias, dn_norm=dn_norm, w_up_pool=w_up_pool)
    h = x
    for i in range(depth):
        st = _stage_mixers(h, prm, i)
        w_router, b_router = _router_weights(w_router_group[i], b_router_group[i],
                                             w_router_expert[i], b_router_expert[i])
        tm = _pick(t, 256)
        h1, n2, route, cnt = _merge(st["y_dn"], st["proj"], st["mp"], h.reshape(t, d),
                                    w_up_dn[i].astype(BF16), w_out[i].astype(BF16), norm_moe[i][None, :],
                                    w_router, b_router, tm=tm)
        pos, blk_e, n_used, tok_pad, cap = _stage_moe_plan(route, cnt, t)
        ys = _experts(blk_e, n_used, tok_pad, n2, w_gate[i], w_up[i], w_down[i], bm=MOE_BM, cap=cap)
        out = _combine(pos, h1, route, p[i].reshape(t, -1), ys, w_ple_gate[i].astype(BF16),
                       w_ple_proj[i].astype(BF16), norm_ple[i][None, :], norm_final[None, :],
                       tm=tm, final_norm=(i == depth - 1))
        h = out.reshape(bsz, seq, d)
    return h
```

```python
import functools

import jax
import jax.numpy as jnp
from jax import lax
from jax.experimental import pallas as pl
from jax.experimental.pallas import tpu as pltpu

F32 = jnp.float32
BF16 = jnp.bfloat16

EPS = 1e-6
POOL_WINDOWS = (2, 4, 8, 16)
POOL_HALO = 16
CONV_K = 4
CONV_HALO = 8
DN_HEADS = 8
DN_DK = 128
CHUNK = 64
N_GROUPS = 4
EXPERTS_PER_GROUP = 8
N_EXPERTS = N_GROUPS * EXPERTS_PER_GROUP
LANES = 128
VMEM_LIMIT = 56 * 1024 * 1024

COL_GATE_POOL = 0
COL_GATE_DN = 2048
COL_POOL = 4096
COL_Q = 5120
COL_K = 6144
COL_V = 7168
COL_Z = 8192
N_MAIN = 9216


def _cparams(sem):
    return pltpu.CompilerParams(dimension_semantics=sem, vmem_limit_bytes=VMEM_LIMIT)


def _resident(shape):
    nd = len(shape)
    return pl.BlockSpec(shape, lambda *_: (0,) * nd, pipeline_mode=pl.Buffered(1))


def _inproj_kernel(x_ref, gain_ref, w_ref, wba_ref, out_ref, ba_ref, n1_ref, *, sub):
    tm = x_ref.shape[0]

    @pl.when(pl.program_id(1) == 0)
    def _():
        def body(r, _):
            rows = pl.ds(pl.multiple_of(r * sub, sub), sub)
            x = x_ref[rows, :]
            y = x * lax.rsqrt(jnp.mean(x * x, axis=-1, keepdims=True) + EPS) * gain_ref[...]
            n1_ref[rows, :] = y.astype(BF16)
            return 0
        lax.fori_loop(0, tm // sub, body, 0)
        ba_ref[...] = jnp.dot(n1_ref[...], wba_ref[...], preferred_element_type=F32)

    out_ref[...] = jnp.dot(n1_ref[...], w_ref[...], preferred_element_type=F32)


def _inproj(x2d, gain, w_main, w_ba, *, tm, tn):
    t, d = x2d.shape
    n = w_main.shape[1]
    return pl.pallas_call(
        functools.partial(_inproj_kernel, sub=min(tm, 256)),
        grid=(t // tm, n // tn),
        in_specs=[
            pl.BlockSpec((tm, d), lambda i, j: (i, 0)),
            _resident((1, d)),
            pl.BlockSpec((d, tn), lambda i, j: (0, j)),
            _resident((d, LANES)),
        ],
        out_specs=[
            pl.BlockSpec((tm, tn), lambda i, j: (i, j)),
            pl.BlockSpec((tm, LANES), lambda i, j: (i, 0)),
        ],
        out_shape=[jax.ShapeDtypeStruct((t, n), F32), jax.ShapeDtypeStruct((t, LANES), F32)],
        scratch_shapes=[pltpu.VMEM((tm, d), BF16)],
        compiler_params=_cparams(("parallel", "arbitrary")),
        name="inproj",
    )(x2d, gain, w_main, w_ba)


def _pool_kernel(u_ref, gp_ref, pw_ref, scale_ref, wup_ref, out_ref, ext_ref):
    ts = u_ref.shape[0]
    gw = pw_ref.shape[1]
    s = pl.program_id(1)

    @pl.when(s == 0)
    def _():
        ext_ref[0:POOL_HALO, :] = jnp.zeros((POOL_HALO, ext_ref.shape[1]), F32)

    @pl.when(s > 0)
    def _():
        ext_ref[0:POOL_HALO, :] = ext_ref[ts:ts + POOL_HALO, :]

    ext_ref[POOL_HALO:POOL_HALO + ts, :] = u_ref[...]

    t1 = (s * ts + 1 + lax.broadcasted_iota(jnp.int32, (ts, 1), 0)).astype(F32)
    acc = jnp.zeros(out_ref.shape, F32)
    for g, w in enumerate(POOL_WINDOWS):
        cols = slice(g * gw, (g + 1) * gw)
        cur = ext_ref[POOL_HALO:POOL_HALO + ts, cols]
        win = cur
        for j in range(1, w):
            win = win + ext_ref[POOL_HALO - j:POOL_HALO - j + ts, cols]
        d = win / jnp.minimum(t1, float(w)) - cur
        y = jnp.dot(d.astype(BF16), pw_ref[g], preferred_element_type=F32) * scale_ref[:, cols]
        acc = acc + jnp.dot(y.astype(BF16), wup_ref[cols, :], preferred_element_type=F32)
    out_ref[...] = jax.nn.sigmoid(gp_ref[...]) * acc


def _pool(proj, pool_w, pool_scale, w_up_pool, *, bsz, seq, ts):
    t = proj.shape[0]
    ns = seq // ts
    d_pool = w_up_pool.shape[0]
    d = w_up_pool.shape[1]
    return pl.pallas_call(
        _pool_kernel,
        grid=(bsz, ns),
        in_specs=[
            pl.BlockSpec((ts, d_pool), lambda b, s: (b * ns + s, COL_POOL // d_pool)),
            pl.BlockSpec((ts, d), lambda b, s: (b * ns + s, COL_GATE_POOL // d)),
            _resident(pool_w.shape),
            _resident((1, d_pool)),
            _resident(w_up_pool.shape),
        ],
        out_specs=pl.BlockSpec((ts, d), lambda b, s: (b * ns + s, 0)),
        out_shape=jax.ShapeDtypeStruct((t, d), F32),
        scratch_shapes=[pltpu.VMEM((POOL_HALO + ts, d_pool), F32)],
        compiler_params=_cparams(("parallel", "arbitrary")),
        name="pool",
    )(proj, proj, pool_w, pool_scale, w_up_pool)


def _dot_nt(a, b):
    return lax.dot_general(a, b, (((1,), (1,)), ((), ())), preferred_element_type=F32)


def _dot_tn(a, b):
    return lax.dot_general(a, b, (((0,), (0,)), ((), ())), preferred_element_type=F32)


GROUP = 256
PAIR = 2 * CHUNK
WY_HEADS = 2


def _wy_kernel(q_ref, k_ref, v_ref, ba_ref, cw_ref, alog_ref, dtb_ref,
               u_ref, w_ref, qg_ref, kd_ref, at_ref, egl_ref, ext, carry_ref, gate_ref, qkv_ref, *, ns):
    ts, hw = q_ref.shape
    hp = pl.program_id(1)
    s = pl.program_id(0) % ns

    @pl.when(s == 0)
    def _():
        ext[0:CONV_HALO, :] = jnp.zeros((CONV_HALO, 3 * hw), F32)

    @pl.when(s > 0)
    def _():
        ext[0:CONV_HALO, :] = carry_ref[hp]

    ext[CONV_HALO:CONV_HALO + ts, 0:hw] = q_ref[...]
    ext[CONV_HALO:CONV_HALO + ts, hw:2 * hw] = k_ref[...]
    ext[CONV_HALO:CONV_HALO + ts, 2 * hw:3 * hw] = v_ref[...]
    carry_ref[hp] = ext[ts:ts + CONV_HALO, :]

    def conv_silu(c0):
        cols = slice(c0, c0 + DN_DK)
        first = CONV_HALO - (CONV_K - 1)
        y = ext[first:first + ts, cols] * cw_ref[0, 0:1, cols]
        for tap in range(1, CONV_K):
            y = y + ext[first + tap:first + tap + ts, cols] * cw_ref[0, tap:tap + 1, cols]
        return y * jax.nn.sigmoid(y)

    @pl.when(hp == 0)
    def _():
        ba = ba_ref[...]
        xs = ba + dtb_ref[...]
        softplus = jnp.maximum(xs, 0.0) + jnp.log1p(jnp.exp(-jnp.abs(xs)))
        g_all = -jnp.exp(alog_ref[...]) * softplus
        ri = lax.broadcasted_iota(jnp.int32, (ts, ts), 0)
        ci = lax.broadcasted_iota(jnp.int32, (ts, ts), 1)
        tri = jnp.where((ri // CHUNK == ci // CHUNK) & (ci <= ri), 1.0, 0.0).astype(F32)
        gate_ref[0] = jax.nn.sigmoid(ba)
        gate_ref[1] = jnp.dot(tri, g_all, preferred_element_type=F32, precision=lax.Precision.HIGHEST)

    beta_all = gate_ref[0]
    gcum = gate_ref[1]
    lane = lax.broadcasted_iota(jnp.int32, (ts, LANES), 1)
    pick = lambda arr, ln: jnp.sum(jnp.where(lane == ln, arr, 0.0), axis=-1, keepdims=True)
    row8 = lax.broadcasted_iota(jnp.int32, (8, LANES), 0)
    lane8 = lax.broadcasted_iota(jnp.int32, (8, LANES), 1)
    pick_rows = jnp.where(lane8 == DN_HEADS + hp * WY_HEADS + row8, 1.0, 0.0).astype(F32)
    gr_all = lax.dot_general(pick_rows, gcum, (((1,), (1,)), ((), ())), preferred_element_type=F32,
                             precision=lax.Precision.HIGHEST)

    betas, gcs = [], []
    for hh in range(WY_HEADS):
        hc = slice(hh * DN_DK, (hh + 1) * DN_DK)
        qh = conv_silu(hh * DN_DK)
        qkv_ref[0, :, hc] = qh * (lax.rsqrt(jnp.sum(qh * qh, axis=-1, keepdims=True) + EPS) * (DN_DK ** -0.5))
        kh = conv_silu(hw + hh * DN_DK)
        qkv_ref[1, :, hc] = kh * lax.rsqrt(jnp.sum(kh * kh, axis=-1, keepdims=True) + EPS)
        qkv_ref[2, :, hc] = conv_silu(2 * hw + hh * DN_DK)
        head = hp * WY_HEADS + hh
        betas.append(jnp.broadcast_to(pick(beta_all, head), (ts, DN_DK)))
        gcs.append(jnp.broadcast_to(pick(gcum, DN_HEADS + head), (ts, DN_DK)))

    ii = lax.broadcasted_iota(jnp.int32, (GROUP, GROUP), 0)
    jj = lax.broadcasted_iota(jnp.int32, (GROUP, GROUP), 1)
    same = (ii // CHUNK) == (jj // CHUNK)
    incl = same & (ii >= jj)
    strict = same & (ii > jj)

    items = [(slice(g * GROUP, (g + 1) * GROUP), hh) for hh in range(WY_HEADS)
             for g in range(ts // GROUP)]
    lows, sols = [], []
    for rows, hh in items:
        hc = slice(hh * DN_DK, (hh + 1) * DN_DK)
        q = qkv_ref[0, rows, hc]
        k = qkv_ref[1, rows, hc]
        beta = betas[hh][rows, :]
        gc = gcs[hh][rows, :]
        gr = gr_all[hh:hh + 1, rows]
        decay = jnp.where(incl, jnp.exp(jnp.concatenate([gc, gc], axis=1) - gr), 0.0)
        kb = k * beta
        a = _dot_nt(jnp.concatenate([kb, q], axis=0).astype(BF16), k.astype(BF16))
        lows.append(jnp.where(strict, a[:GROUP] * decay, 0.0).astype(BF16))
        attn = a[GROUP:] * decay
        for j in range(GROUP // PAIR):
            pr = slice(j * PAIR, (j + 1) * PAIR)
            at_ref[rows.start + j * PAIR:rows.start + (j + 1) * PAIR, hc] = attn[pr, pr].astype(BF16)
        egc = jnp.exp(gc)
        qg_ref[rows, hc] = (q * egc).astype(BF16)
        lasts = [gc[(c + 1) * CHUNK - 1:(c + 1) * CHUNK, :] for c in range(GROUP // CHUNK)]
        for c, g_last in enumerate(lasts):
            row = rows.start // CHUNK + c
            egl_ref[row:row + 1, hc] = jnp.exp(g_last)
        g_last_rows = jnp.concatenate([jnp.broadcast_to(gl, (CHUNK, DN_DK)) for gl in lasts], axis=0)
        kd_ref[rows, hc] = (k * jnp.exp(g_last_rows - gc)).astype(BF16)
        sols.append(jnp.concatenate([qkv_ref[2, rows, hc] * beta, kb * egc], axis=1))

    mm = lambda a16, b: jnp.dot(a16, b.astype(BF16), preferred_element_type=F32)
    sols = [s - mm(l, s) for l, s in zip(lows, sols)]
    pws = lows
    for _ in range(CHUNK.bit_length() - 2):
        pws = [mm(p, p).astype(BF16) for p in pws]
        sols = [s + mm(p, s) for p, s in zip(pws, sols)]
    for (rows, hh), s in zip(items, sols):
        hc = slice(hh * DN_DK, (hh + 1) * DN_DK)
        u_ref[rows, hc] = s[:, :DN_DK]
        w_ref[rows, hc] = s[:, DN_DK:].astype(BF16)


def _wy(proj, ba, conv_w, alog_pad, dtb_pad, *, seq, ts):
    t = proj.shape[0]
    hw = WY_HEADS * DN_DK
    n_hp = DN_HEADS // WY_HEADS
    dq = DN_HEADS * DN_DK
    cw = conv_w.reshape(CONV_K, 3, n_hp, hw).transpose(2, 0, 1, 3).reshape(n_hp, CONV_K, 3 * hw)
    src = lambda col: pl.BlockSpec((ts, hw), lambda r, h: (r, col // hw + h))
    hblk = pl.BlockSpec((ts, hw), lambda r, h: (r, h))
    wide = jax.ShapeDtypeStruct((t, dq), BF16)
    return pl.pallas_call(
        functools.partial(_wy_kernel, ns=seq // ts),
        grid=(t // ts, n_hp),
        in_specs=[src(COL_Q), src(COL_K), src(COL_V),
                  pl.BlockSpec((ts, LANES), lambda r, h: (r, 0)),
                  pl.BlockSpec((1, CONV_K, 3 * hw), lambda r, h: (h, 0, 0)),
                  _resident((1, LANES)), _resident((1, LANES))],
        out_specs=[hblk, hblk, hblk, hblk, hblk,
                   pl.BlockSpec((ts // CHUNK, hw), lambda r, h: (r, h))],
        out_shape=[jax.ShapeDtypeStruct((t, dq), F32), wide, wide, wide, wide,
                   jax.ShapeDtypeStruct((t // CHUNK, dq), F32)],
        scratch_shapes=[pltpu.VMEM((CONV_HALO + ts, 3 * hw), F32),
                        pltpu.VMEM((n_hp, CONV_HALO, 3 * hw), F32),
                        pltpu.VMEM((2, ts, LANES), F32),
                        pltpu.VMEM((3, ts, hw), F32)],
        compiler_params=_cparams(("arbitrary", "arbitrary")),
        name="wy",
    )(proj, proj, proj, ba, cw, alog_pad, dtb_pad)


def _scan_kernel(u_ref, w_ref, qg_ref, kd_ref, at_ref, egl_ref, z_ref, dnw_ref, out_ref, state_ref):
    ts = u_ref.shape[0]

    @pl.when(pl.program_id(1) == 0)
    def _():
        state_ref[...] = jnp.zeros(state_ref.shape, F32)

    heads = range(DN_HEADS)
    hcs = [slice(h * DN_DK, (h + 1) * DN_DK) for h in heads]
    states = [state_ref[h] for h in heads]
    even_vn = [None] * DN_HEADS
    for c in range(ts // CHUNK):
        rows = slice(c * CHUNK, (c + 1) * CHUNK)
        rs = [jnp.dot(jnp.concatenate([w_ref[rows, hcs[h]], qg_ref[rows, hcs[h]]], axis=0),
                      states[h].astype(BF16), preferred_element_type=F32) for h in heads]
        vns = [(u_ref[rows, hcs[h]] - rs[h][:CHUNK]).astype(BF16) for h in heads]
        states = [states[h] * egl_ref[c:c + 1, hcs[h]] + _dot_tn(kd_ref[rows, hcs[h]], vns[h]) for h in heads]
        for h in heads:
            if c % 2 == 0:
                even_vn[h] = vns[h]
                pair = jnp.concatenate([vns[h], jnp.zeros_like(vns[h])], axis=0)
            else:
                pair = jnp.concatenate([even_vn[h], vns[h]], axis=0)
            o = rs[h][CHUNK:] + jnp.dot(at_ref[rows, hcs[h]], pair, preferred_element_type=F32)
            on = o * lax.rsqrt(jnp.mean(o * o, axis=-1, keepdims=True) + EPS) * dnw_ref[...]
            z = z_ref[rows, hcs[h]]
            out_ref[rows, hcs[h]] = (on * (z * jax.nn.sigmoid(z))).astype(out_ref.dtype)
    for h in heads:
        state_ref[h] = states[h]


def _scan(u, w, qg, kd, at, egl, proj, dn_norm, *, bsz, seq, ts):
    t, dv = u.shape
    ns = seq // ts
    blk = pl.BlockSpec((ts, dv), lambda b, s: (b * ns + s, 0))
    return pl.pallas_call(
        _scan_kernel,
        grid=(bsz, ns),
        in_specs=[blk, blk, blk, blk, blk,
                  pl.BlockSpec((ts // CHUNK, dv), lambda b, s: (b * ns + s, 0)),
                  pl.BlockSpec((ts, dv), lambda b, s: (b * ns + s, COL_Z // dv)),
                  _resident((1, DN_DK))],
        out_specs=blk,
        out_shape=jax.ShapeDtypeStruct((t, dv), BF16),
        scratch_shapes=[pltpu.VMEM((DN_HEADS, DN_DK, DN_DK), F32)],
        compiler_params=_cparams(("parallel", "arbitrary")),
        name="scan",
    )(u, w, qg, kd, at, egl, proj, dn_norm)


ROUTE_E1, ROUTE_E2, ROUTE_W1, ROUTE_W2, ROUTE_R1, ROUTE_R2 = range(6)
ROUTER_LANE0 = N_GROUPS
NEG_BIG = -1e30


def _pack_rows(x):
    n = x.shape[1] // 2
    bits = lambda v: lax.bitcast_convert_type(v.astype(BF16).astype(F32), jnp.uint32)
    return bits(x[:, n:]) | (bits(x[:, :n]) >> 16)


def _unpack_rows(p):
    lo = lax.bitcast_convert_type(p << 16, F32).astype(BF16)
    hi = lax.bitcast_convert_type(p & jnp.uint32(0xFFFF0000), F32).astype(BF16)
    return jnp.concatenate([lo, hi], axis=1)


def _merge_kernel(ydn_ref, gd_ref, mp_ref, x_ref, wupdn_ref, wout_ref, gain_ref, wr_ref, br_ref,
                  h1_ref, n2_ref, route_ref, cnt_ref, carry_ref):
    tm = x_ref.shape[0]

    @pl.when(pl.program_id(0) == 0)
    def _():
        carry_ref[...] = jnp.zeros(carry_ref.shape, F32)

    md = jax.nn.sigmoid(gd_ref[...]) * jnp.dot(ydn_ref[...], wupdn_ref[...], preferred_element_type=F32)
    merged = mp_ref[...] + md
    h1 = x_ref[...] + jnp.dot(merged.astype(BF16), wout_ref[...], preferred_element_type=F32)
    h1_ref[...] = h1
    n2 = h1 * lax.rsqrt(jnp.mean(h1 * h1, axis=-1, keepdims=True) + EPS) * gain_ref[...]
    n2_ref[...] = _pack_rows(n2)

    hi = n2.astype(BF16)
    lo = (n2 - hi.astype(F32)).astype(BF16)
    r_hi = jnp.dot(hi, wr_ref[...], preferred_element_type=F32)
    r_lo = jnp.dot(lo, wr_ref[:, :LANES], preferred_element_type=F32)
    logits = r_hi[:, :LANES] + r_hi[:, LANES:] + r_lo + br_ref[...]

    lane = lax.broadcasted_iota(jnp.int32, (tm, LANES), 1)
    lane_f = lane.astype(F32)
    first_max = lambda v, m: jnp.min(jnp.where(v == m, lane_f, float(LANES)), axis=-1, keepdims=True)

    is_grp = lane < N_GROUPS
    lg = jnp.where(is_grp, logits, NEG_BIG)
    gmax = jnp.max(lg, axis=-1, keepdims=True)
    gi = first_max(lg, gmax)
    p_sel = 1.0 / jnp.sum(jnp.where(is_grp, jnp.exp(lg - gmax), 0.0), axis=-1, keepdims=True)

    e_lane = lane - ROUTER_LANE0
    in_grp = (e_lane >= 0) & (e_lane < N_EXPERTS) & ((e_lane // EXPERTS_PER_GROUP).astype(F32) == gi)
    le = jnp.where(in_grp, logits, NEG_BIG)
    v1 = jnp.max(le, axis=-1, keepdims=True)
    i1 = first_max(le, v1)
    le2 = jnp.where(lane_f == i1, NEG_BIG, le)
    v2 = jnp.max(le2, axis=-1, keepdims=True)
    i2 = first_max(le2, v2)
    s = jnp.exp(v2 - v1)
    w1 = p_sel / (1.0 + s)
    w2 = p_sel * s / (1.0 + s)
    e1 = i1 - float(ROUTER_LANE0)
    e2 = i2 - float(ROUTER_LANE0)

    oh1 = jnp.where(lane_f == e1, 1.0, 0.0)
    oh2 = jnp.where(lane_f == e2, 1.0, 0.0)
    both = oh1 + oh2
    ri = lax.broadcasted_iota(jnp.int32, (tm, tm), 0)
    ci = lax.broadcasted_iota(jnp.int32, (tm, tm), 1)
    tri = jnp.where(ri > ci, 1.0, 0.0).astype(BF16)
    prior = jnp.dot(tri, both.astype(BF16), preferred_element_type=F32) + carry_ref[...]
    r1 = jnp.sum(prior * oh1, axis=-1, keepdims=True)
    r2 = jnp.sum(prior * oh2, axis=-1, keepdims=True)
    carry = carry_ref[...] + jnp.sum(both, axis=0, keepdims=True)
    carry_ref[...] = carry
    cnt_ref[...] = carry

    route = jnp.zeros((tm, LANES), F32)
    for ln, val in ((ROUTE_E1, e1), (ROUTE_E2, e2), (ROUTE_W1, w1), (ROUTE_W2, w2),
                    (ROUTE_R1, r1), (ROUTE_R2, r2)):
        route = jnp.where(lane == ln, val, route)
    route_ref[...] = route


def _merge(y_dn, proj, mp, x2d, w_up_dn, w_out, gain, w_router, b_router, *, tm):
    t, d = x2d.shape
    dv = y_dn.shape[1]
    rows = lambda blk: pl.BlockSpec(blk, lambda i: (i, 0))
    return pl.pallas_call(
        _merge_kernel,
        grid=(t // tm,),
        in_specs=[
            rows((tm, dv)),
            pl.BlockSpec((tm, d), lambda i: (i, COL_GATE_DN // d)),
            rows((tm, d)), rows((tm, d)),
            _resident(w_up_dn.shape), _resident(w_out.shape), _resident((1, d)),
            _resident(w_router.shape), _resident((1, LANES)),
        ],
        out_specs=[rows((tm, d)), rows((tm, d // 2)), rows((tm, LANES)),
                   pl.BlockSpec((1, LANES), lambda i: (0, 0))],
        out_shape=[jax.ShapeDtypeStruct((t, d), F32), jax.ShapeDtypeStruct((t, d // 2), jnp.uint32),
                   jax.ShapeDtypeStruct((t, LANES), F32), jax.ShapeDtypeStruct((1, LANES), F32)],
        scratch_shapes=[pltpu.VMEM((1, LANES), F32)],
        compiler_params=_cparams(("arbitrary",)),
        name="merge",
    )(y_dn, proj, mp, x2d, w_up_dn, w_out, gain, w_router, b_router)


N_GATHER_BUF = 3


def _experts_kernel(be_ref, nu_ref, tok_ref, n2_ref, wg_ref, wu_ref, wd_ref, y_ref,
                    xbuf0, xbuf1, xbuf2, wg16, wu16, wd16, sem):
    bufs = (xbuf0, xbuf1, xbuf2)
    bm = xbuf0.shape[0]
    i = pl.program_id(0)
    nu = nu_ref[0]

    @pl.when((i < nu) & ((i == 0) | (be_ref[i] != be_ref[jnp.maximum(i - 1, 0)])))
    def _():
        wg16[...] = wg_ref[0].astype(BF16)
        wu16[...] = wu_ref[0].astype(BF16)
        wd16[...] = wd_ref[0].astype(BF16)

    def issue(blk, slot):
        for j in range(bm):
            tok = tok_ref[blk * bm + j]
            pltpu.make_async_copy(n2_ref.at[pl.ds(tok, 1)], bufs[slot].at[pl.ds(j, 1)],
                                  sem.at[slot]).start(priority=j % 2)

    def drain(slot):
        pltpu.make_async_copy(n2_ref.at[pl.ds(0, bm)], bufs[slot], sem.at[slot]).wait()

    @pl.when(i == 0)
    def _():
        issue(0, 0)
        issue(1, 1)

    for r in range(N_GATHER_BUF):
        @pl.when((i < nu) & (i % N_GATHER_BUF == r))
        def _(r=r):
            drain(r)
            issue(i + 2, (r + 2) % N_GATHER_BUF)
            xb = _unpack_rows(bufs[r][...])
            g = jnp.dot(xb, wg16[...], preferred_element_type=F32)
            u = jnp.dot(xb, wu16[...], preferred_element_type=F32)
            hmid = (g * jax.nn.sigmoid(g)) * u
            y_ref[...] = _pack_rows(jnp.dot(hmid.astype(BF16), wd16[...], preferred_element_type=F32))

        @pl.when((i == nu) & (i % N_GATHER_BUF == r))
        def _(r=r):
            drain(r)
            drain((r + 1) % N_GATHER_BUF)

    @pl.when(i >= nu)
    def _():
        y_ref[...] = jnp.zeros(y_ref.shape, y_ref.dtype)


def _experts(blk_e, n_used, tok_pad, n2, w_gate, w_up, w_down, *, bm, cap):
    dp = n2.shape[1]
    d, de = w_gate.shape[1], w_gate.shape[2]
    wspec = lambda shape: pl.BlockSpec(shape, lambda i, be, nu, tok: (be[i], 0, 0))
    xbuf = pltpu.VMEM((bm, dp), jnp.uint32)
    return pl.pallas_call(
        _experts_kernel,
        grid_spec=pltpu.PrefetchScalarGridSpec(
            num_scalar_prefetch=3,
            grid=(cap // bm,),
            in_specs=[pl.BlockSpec(memory_space=pl.ANY),
                      wspec((1, d, de)), wspec((1, d, de)), wspec((1, de, d))],
            out_specs=pl.BlockSpec((bm, dp), lambda i, be, nu, tok: (i, 0)),
            scratch_shapes=[xbuf, xbuf, xbuf, pltpu.VMEM((d, de), BF16), pltpu.VMEM((d, de), BF16),
                            pltpu.VMEM((de, d), BF16), pltpu.SemaphoreType.DMA((N_GATHER_BUF,))],
        ),
        out_shape=jax.ShapeDtypeStruct((cap, dp), jnp.uint32),
        compiler_params=_cparams(("arbitrary",)),
        name="experts",
    )(blk_e, n_used, tok_pad, n2, w_gate, w_up, w_down)


def _combine_kernel(pos_ref, h1_ref, route_ref, p_ref, ys_ref, wpg_ref, wpp_ref, gple_ref, gfin_ref,
                    out_ref, ybuf0, ybuf1, ybuf2, sem, *, final_norm, n_steps):
    bufs = (ybuf0, ybuf1, ybuf2)
    tm = h1_ref.shape[0]
    i = pl.program_id(0)

    def issue(blk, slot):
        for j in range(tm):
            for k in range(2):
                src = pos_ref[2 * (blk * tm + j) + k]
                pltpu.make_async_copy(ys_ref.at[pl.ds(src, 1)], bufs[slot].at[pl.ds(k * tm + j, 1)],
                                      sem.at[slot]).start(priority=k)

    def drain(slot):
        pltpu.make_async_copy(ys_ref.at[pl.ds(0, 2 * tm)], bufs[slot], sem.at[slot]).wait()

    @pl.when(i == 0)
    def _():
        issue(0, 0)
        issue(1, 1)

    for r in range(N_GATHER_BUF):
        @pl.when(i % N_GATHER_BUF == r)
        def _(r=r):
            drain(r)
            issue(i + 2, (r + 2) % N_GATHER_BUF)
            cur = bufs[r]
            route = route_ref[...]
            y_moe = (_unpack_rows(cur[0:tm, :]).astype(F32) * route[:, ROUTE_W1:ROUTE_W1 + 1]
                     + _unpack_rows(cur[tm:2 * tm, :]).astype(F32) * route[:, ROUTE_W2:ROUTE_W2 + 1])
            h2 = h1_ref[...] + y_moe
            n3 = h2 * lax.rsqrt(jnp.mean(h2 * h2, axis=-1, keepdims=True) + EPS) * gple_ref[...]
            gate = jax.nn.sigmoid(jnp.dot(n3.astype(BF16), wpg_ref[...], preferred_element_type=F32))
            h3 = h2 + gate * jnp.dot(p_ref[...].astype(BF16), wpp_ref[...], preferred_element_type=F32)
            if final_norm:
                h3 = h3 * lax.rsqrt(jnp.mean(h3 * h3, axis=-1, keepdims=True) + EPS) * gfin_ref[...]
            out_ref[...] = h3

    last = n_steps - 1

    @pl.when(i == last)
    def _():
        drain((last + 1) % N_GATHER_BUF)
        drain((last + 2) % N_GATHER_BUF)


def _combine(pos, h1, route, p2d, ys, w_ple_gate, w_ple_proj, g_ple, g_fin, *, tm, final_norm):
    t, d = h1.shape
    rows = lambda w: pl.BlockSpec((tm, w), lambda i, pos: (i, 0))
    res = lambda shape: pl.BlockSpec(shape, lambda i, pos: (0,) * len(shape), pipeline_mode=pl.Buffered(1))
    return pl.pallas_call(
        functools.partial(_combine_kernel, final_norm=final_norm, n_steps=t // tm),
        grid_spec=pltpu.PrefetchScalarGridSpec(
            num_scalar_prefetch=1,
            grid=(t // tm,),
            in_specs=[rows(d), rows(LANES), rows(p2d.shape[1]), pl.BlockSpec(memory_space=pl.ANY),
                      res(w_ple_gate.shape), res(w_ple_proj.shape), res((1, d)), res((1, d))],
            out_specs=rows(d),
            scratch_shapes=[pltpu.VMEM((2 * tm, ys.shape[1]), ys.dtype)] * N_GATHER_BUF
                           + [pltpu.SemaphoreType.DMA((N_GATHER_BUF,))],
        ),
        out_shape=jax.ShapeDtypeStruct((t, d), F32),
        compiler_params=_cparams(("arbitrary",)),
        name="combine",
    )(jnp.pad(pos, (0, 2 * 2 * tm)), h1, route, p2d, ys, w_ple_gate, w_ple_proj, g_ple, g_fin)


def _pick(n, pref):
    b = min(n, pref)
    while n % b:
        b -= 8
    return b


def _stage_mixers(x, prm, i):
    bsz, seq, d = x.shape
    t = bsz * seq
    x2d = x.reshape(t, d)
    w_in = prm["w_in"][i]
    d_pool = prm["w_up_pool"].shape[1]
    dq = DN_HEADS * DN_DK
    o_qkv = d_pool
    o_z = o_qkv + 3 * dq
    o_b = o_z + dq
    o_gp = o_b + 2 * DN_HEADS
    o_gd = o_gp + d
    w_main = jnp.concatenate(
        [w_in[:, o_gp:o_gp + d], w_in[:, o_gd:o_gd + d], w_in[:, :d_pool],
         w_in[:, o_qkv:o_qkv + 3 * dq], w_in[:, o_z:o_z + dq]], axis=1).astype(BF16)
    assert w_main.shape[1] == N_MAIN
    w_ba = jnp.pad(w_in[:, o_b:o_b + 2 * DN_HEADS], ((0, 0), (0, LANES - 2 * DN_HEADS))).astype(BF16)
    proj, ba = _inproj(x2d, prm["norm_mix"][i][None, :], w_main, w_ba,
                       tm=_pick(t, 1024), tn=1024)

    ts = _pick(seq, 512)
    mp = _pool(proj, prm["pool_w"][i].astype(BF16), prm["pool_scale"][i][None, :],
               prm["w_up_pool"][i].astype(BF16), bsz=bsz, seq=seq, ts=ts)

    lane_pad = lambda a: jnp.pad(a, (DN_HEADS, LANES - 2 * DN_HEADS))[None, :]
    u, w, qg, kd, at, egl = _wy(proj, ba, prm["conv_w"][i], lane_pad(prm["a_log"][i]),
                                lane_pad(prm["dt_bias"][i]), seq=seq, ts=ts)
    y_dn = _scan(u, w, qg, kd, at, egl, proj, prm["dn_norm"][i][None, :], bsz=bsz, seq=seq, ts=ts)
    return dict(proj=proj, ba=ba, mp=mp, y_dn=y_dn)


MOE_BM = 256


def _router_weights(w_rg, b_rg, w_re, b_re):
    d = w_rg.shape[0]
    w = jnp.concatenate([w_rg, w_re, jnp.zeros((d, LANES - N_GROUPS - N_EXPERTS), F32)], axis=1)
    w_hi = w.astype(BF16)
    w_lo = (w - w_hi.astype(F32)).astype(BF16)
    b = jnp.concatenate([b_rg, b_re, jnp.zeros((LANES - N_GROUPS - N_EXPERTS,), F32)])[None, :]
    return jnp.concatenate([w_hi, w_lo], axis=1), b


def _stage_moe_plan(route, cnt, t):
    counts = cnt[0, :N_EXPERTS].astype(jnp.int32)
    padded = ((counts + MOE_BM - 1) // MOE_BM) * MOE_BM
    pends = jnp.cumsum(padded)
    pstarts = pends - padded
    eid = route[:, ROUTE_E1:ROUTE_E2 + 1].astype(jnp.int32)
    rank = route[:, ROUTE_R1:ROUTE_R2 + 1].astype(jnp.int32)
    pos = (pstarts[eid] + rank).reshape(2 * t)
    cap = 2 * t + N_EXPERTS * MOE_BM
    n_blocks = cap // MOE_BM
    n_used = (pends[-1] // MOE_BM).astype(jnp.int32)
    blk = jnp.minimum(jnp.arange(n_blocks, dtype=jnp.int32), n_used - 1)
    blk_e = jnp.sum((pends[None, :] <= (blk * MOE_BM)[:, None]).astype(jnp.int32), axis=1)
    blk_e = jnp.minimum(blk_e, N_EXPERTS - 1)
    tok_pad = jnp.zeros((cap + MOE_BM,), jnp.int32).at[pos].set(
        jnp.arange(2 * t, dtype=jnp.int32) // 2, unique_indices=True, mode="promise_in_bounds")
    return pos, blk_e, n_used.reshape(1), tok_pad, cap


def kernel(x, p, norm_mix, w_in, pool_w, pool_scale, conv_w, a_log, dt_bias, dn_norm, w_up_pool, w_up_dn, w_out, norm_moe, w_router_group, b_router_group, w_router_expert, b_router_expert, w_gate, w_up, w_down, norm_ple, w_ple_gate, w_ple_proj, norm_final):
    bsz, seq, d = x.shape
    t = bsz * seq
    depth = norm_mix.shape[0]
    prm = dict(norm_mix=norm_mix, w_in=w_in, pool_w=pool_w, pool_scale=pool_scale, conv_w=conv_w,
               a_log=a_log, dt_bias=dt_bias, dn_norm=dn_norm, w_up_pool=w_up_pool)
    h = x
    for i in range(depth):
        st = _stage_mixers(h, prm, i)
        w_router, b_router = _router_weights(w_router_group[i], b_router_group[i],
                                             w_router_expert[i], b_router_expert[i])
        tm = _pick(t, 256)
        h1, n2, route, cnt = _merge(st["y_dn"], st["proj"], st["mp"], h.reshape(t, d),
                                    w_up_dn[i].astype(BF16), w_out[i].astype(BF16), norm_moe[i][None, :],
                                    w_router, b_router, tm=tm)
        pos, blk_e, n_used, tok_pad, cap = _stage_moe_plan(route, cnt, t)
        ys = _experts(blk_e, n_used, tok_pad, n2, w_gate[i], w_up[i], w_down[i], bm=MOE_BM, cap=cap)
        out = _combine(pos, h1, route, p[i].reshape(t, -1), ys, w_ple_gate[i].astype(BF16),
                       w_ple_proj[i].astype(BF16), norm_ple[i][None, :], norm_final[None, :],
                       tm=tm, final_norm=(i == depth - 1))
        h = out.reshape(bsz, seq, d)
    return h
```

```python
import functools

import jax
import jax.numpy as jnp
from jax import lax
from jax.experimental import pallas as pl
from jax.experimental.pallas import tpu as pltpu

F32 = jnp.float32
BF16 = jnp.bfloat16

EPS = 1e-6
POOL_WINDOWS = (2, 4, 8, 16)
POOL_HALO = 16
CONV_K = 4
CONV_HALO = 8
DN_HEADS = 8
DN_DK = 128
CHUNK = 64
N_GROUPS = 4
EXPERTS_PER_GROUP = 8
N_EXPERTS = N_GROUPS * EXPERTS_PER_GROUP
LANES = 128
VMEM_LIMIT = 56 * 1024 * 1024

COL_GATE_POOL = 0
COL_GATE_DN = 2048
COL_POOL = 4096
COL_Q = 5120
COL_K = 6144
COL_V = 7168
COL_Z = 8192
N_MAIN = 9216


def _cparams(sem):
    return pltpu.CompilerParams(dimension_semantics=sem, vmem_limit_bytes=VMEM_LIMIT)


def _resident(shape):
    nd = len(shape)
    return pl.BlockSpec(shape, lambda *_: (0,) * nd, pipeline_mode=pl.Buffered(1))


def _inproj_kernel(x_ref, gain_ref, w_ref, wba_ref, out_ref, ba_ref, n1_ref, *, sub):
    tm = x_ref.shape[0]

    @pl.when(pl.program_id(1) == 0)
    def _():
        def body(r, _):
            rows = pl.ds(pl.multiple_of(r * sub, sub), sub)
            x = x_ref[rows, :]
            y = x * lax.rsqrt(jnp.mean(x * x, axis=-1, keepdims=True) + EPS) * gain_ref[...]
            n1_ref[rows, :] = y.astype(BF16)
            return 0
        lax.fori_loop(0, tm // sub, body, 0)
        ba_ref[...] = jnp.dot(n1_ref[...], wba_ref[...], preferred_element_type=F32)

    out_ref[...] = jnp.dot(n1_ref[...], w_ref[...], preferred_element_type=F32).astype(out_ref.dtype)


def _inproj(x2d, gain, w_main, w_ba, *, tm, tn):
    t, d = x2d.shape
    n = w_main.shape[1]
    return pl.pallas_call(
        functools.partial(_inproj_kernel, sub=min(tm, 256)),
        grid=(t // tm, n // tn),
        in_specs=[
            pl.BlockSpec((tm, d), lambda i, j: (i, 0)),
            _resident((1, d)),
            pl.BlockSpec((d, tn), lambda i, j: (0, j)),
            _resident((d, LANES)),
        ],
        out_specs=[
            pl.BlockSpec((tm, tn), lambda i, j: (i, j)),
            pl.BlockSpec((tm, LANES), lambda i, j: (i, 0)),
        ],
        out_shape=[jax.ShapeDtypeStruct((t, n), BF16), jax.ShapeDtypeStruct((t, LANES), F32)],
        scratch_shapes=[pltpu.VMEM((tm, d), BF16)],
        compiler_params=_cparams(("parallel", "arbitrary")),
        name="inproj",
    )(x2d, gain, w_main, w_ba)


def _pool_kernel(u_ref, gp_ref, pw_ref, scale_ref, wup_ref, out_ref, ext_ref):
    ts = u_ref.shape[0]
    gw = pw_ref.shape[1]
    s = pl.program_id(1)

    @pl.when(s == 0)
    def _():
        ext_ref[0:POOL_HALO, :] = jnp.zeros((POOL_HALO, ext_ref.shape[1]), F32)

    @pl.when(s > 0)
    def _():
        ext_ref[0:POOL_HALO, :] = ext_ref[ts:ts + POOL_HALO, :]

    ext_ref[POOL_HALO:POOL_HALO + ts, :] = u_ref[...].astype(F32)

    t1 = (s * ts + 1 + lax.broadcasted_iota(jnp.int32, (ts, 1), 0)).astype(F32)
    acc = jnp.zeros(out_ref.shape, F32)
    for g, w in enumerate(POOL_WINDOWS):
        cols = slice(g * gw, (g + 1) * gw)
        cur = ext_ref[POOL_HALO:POOL_HALO + ts, cols]
        win = cur
        for j in range(1, w):
            win = win + ext_ref[POOL_HALO - j:POOL_HALO - j + ts, cols]
        d = win / jnp.minimum(t1, float(w)) - cur
        y = jnp.dot(d.astype(BF16), pw_ref[g], preferred_element_type=F32) * scale_ref[:, cols]
        acc = acc + jnp.dot(y.astype(BF16), wup_ref[cols, :], preferred_element_type=F32)
    out_ref[...] = (jax.nn.sigmoid(gp_ref[...].astype(F32)) * acc).astype(out_ref.dtype)


def _pool(proj, pool_w, pool_scale, w_up_pool, *, bsz, seq, ts):
    t = proj.shape[0]
    ns = seq // ts
    d_pool = w_up_pool.shape[0]
    d = w_up_pool.shape[1]
    return pl.pallas_call(
        _pool_kernel,
        grid=(bsz, ns),
        in_specs=[
            pl.BlockSpec((ts, d_pool), lambda b, s: (b * ns + s, COL_POOL // d_pool)),
            pl.BlockSpec((ts, d), lambda b, s: (b * ns + s, COL_GATE_POOL // d)),
            _resident(pool_w.shape),
            _resident((1, d_pool)),
            _resident(w_up_pool.shape),
        ],
        out_specs=pl.BlockSpec((ts, d), lambda b, s: (b * ns + s, 0)),
        out_shape=jax.ShapeDtypeStruct((t, d), BF16),
        scratch_shapes=[pltpu.VMEM((POOL_HALO + ts, d_pool), F32)],
        compiler_params=_cparams(("parallel", "arbitrary")),
        name="pool",
    )(proj, proj, pool_w, pool_scale, w_up_pool)


def _dot_nt(a, b):
    return lax.dot_general(a, b, (((1,), (1,)), ((), ())), preferred_element_type=F32)


def _dot_tn(a, b):
    return lax.dot_general(a, b, (((0,), (0,)), ((), ())), preferred_element_type=F32)


GROUP = 256
PAIR = 2 * CHUNK
WY_HEADS = 2


def _wy_kernel(q_ref, k_ref, v_ref, ba_ref, cw_ref, alog_ref, dtb_ref,
               u_ref, w_ref, qg_ref, kd_ref, at_ref, egl_ref, ext, carry_ref, gate_ref, qkv_ref, *, ns):
    ts, hw = q_ref.shape
    hp = pl.program_id(1)
    s = pl.program_id(0) % ns

    @pl.when(s == 0)
    def _():
        ext[0:CONV_HALO, :] = jnp.zeros((CONV_HALO, 3 * hw), F32)

    @pl.when(s > 0)
    def _():
        ext[0:CONV_HALO, :] = carry_ref[hp]

    ext[CONV_HALO:CONV_HALO + ts, 0:hw] = q_ref[...].astype(F32)
    ext[CONV_HALO:CONV_HALO + ts, hw:2 * hw] = k_ref[...].astype(F32)
    ext[CONV_HALO:CONV_HALO + ts, 2 * hw:3 * hw] = v_ref[...].astype(F32)
    carry_ref[hp] = ext[ts:ts + CONV_HALO, :]

    def conv_silu(c0):
        cols = slice(c0, c0 + DN_DK)
        first = CONV_HALO - (CONV_K - 1)
        y = ext[first:first + ts, cols] * cw_ref[0, 0:1, cols]
        for tap in range(1, CONV_K):
            y = y + ext[first + tap:first + tap + ts, cols] * cw_ref[0, tap:tap + 1, cols]
        return y * jax.nn.sigmoid(y)

    @pl.when(hp == 0)
    def _():
        ba = ba_ref[...]
        xs = ba + dtb_ref[...]
        softplus = jnp.maximum(xs, 0.0) + jnp.log1p(jnp.exp(-jnp.abs(xs)))
        g_all = -jnp.exp(alog_ref[...]) * softplus
        ri = lax.broadcasted_iota(jnp.int32, (ts, ts), 0)
        ci = lax.broadcasted_iota(jnp.int32, (ts, ts), 1)
        tri = jnp.where((ri // CHUNK == ci // CHUNK) & (ci <= ri), 1.0, 0.0).astype(F32)
        gate_ref[0] = jax.nn.sigmoid(ba)
        gate_ref[1] = jnp.dot(tri, g_all, preferred_element_type=F32, precision=lax.Precision.HIGHEST)

    beta_all = gate_ref[0]
    gcum = gate_ref[1]
    lane = lax.broadcasted_iota(jnp.int32, (ts, LANES), 1)
    pick = lambda arr, ln: jnp.sum(jnp.where(lane == ln, arr, 0.0), axis=-1, keepdims=True)
    row8 = lax.broadcasted_iota(jnp.int32, (8, LANES), 0)
    lane8 = lax.broadcasted_iota(jnp.int32, (8, LANES), 1)
    pick_rows = jnp.where(lane8 == DN_HEADS + hp * WY_HEADS + row8, 1.0, 0.0).astype(F32)
    gr_all = lax.dot_general(pick_rows, gcum, (((1,), (1,)), ((), ())), preferred_element_type=F32,
                             precision=lax.Precision.HIGHEST)

    betas, gcs = [], []
    for hh in range(WY_HEADS):
        hc = slice(hh * DN_DK, (hh + 1) * DN_DK)
        qh = conv_silu(hh * DN_DK)
        qkv_ref[0, :, hc] = qh * (lax.rsqrt(jnp.sum(qh * qh, axis=-1, keepdims=True) + EPS) * (DN_DK ** -0.5))
        kh = conv_silu(hw + hh * DN_DK)
        qkv_ref[1, :, hc] = kh * lax.rsqrt(jnp.sum(kh * kh, axis=-1, keepdims=True) + EPS)
        qkv_ref[2, :, hc] = conv_silu(2 * hw + hh * DN_DK)
        head = hp * WY_HEADS + hh
        betas.append(jnp.broadcast_to(pick(beta_all, head), (ts, DN_DK)))
        gcs.append(jnp.broadcast_to(pick(gcum, DN_HEADS + head), (ts, DN_DK)))

    ii = lax.broadcasted_iota(jnp.int32, (GROUP, GROUP), 0)
    jj = lax.broadcasted_iota(jnp.int32, (GROUP, GROUP), 1)
    same = (ii // CHUNK) == (jj // CHUNK)
    incl = same & (ii >= jj)
    strict = same & (ii > jj)

    items = [(slice(g * GROUP, (g + 1) * GROUP), hh) for hh in range(WY_HEADS)
             for g in range(ts // GROUP)]
    lows, sols = [], []
    for rows, hh in items:
        hc = slice(hh * DN_DK, (hh + 1) * DN_DK)
        q = qkv_ref[0, rows, hc]
        k = qkv_ref[1, rows, hc]
        beta = betas[hh][rows, :]
        gc = gcs[hh][rows, :]
        gr = gr_all[hh:hh + 1, rows]
        decay = jnp.where(incl, jnp.exp(jnp.concatenate([gc, gc], axis=1) - gr), 0.0)
        kb = k * beta
        a = _dot_nt(jnp.concatenate([kb, q], axis=0).astype(BF16), k.astype(BF16))
        lows.append(jnp.where(strict, a[:GROUP] * decay, 0.0).astype(BF16))
        attn = a[GROUP:] * decay
        for j in range(GROUP // PAIR):
            pr = slice(j * PAIR, (j + 1) * PAIR)
            at_ref[rows.start + j * PAIR:rows.start + (j + 1) * PAIR, hc] = attn[pr, pr].astype(BF16)
        egc = jnp.exp(gc)
        qg_ref[rows, hc] = (q * egc).astype(BF16)
        lasts = [gc[(c + 1) * CHUNK - 1:(c + 1) * CHUNK, :] for c in range(GROUP // CHUNK)]
        for c, g_last in enumerate(lasts):
            row = rows.start // CHUNK + c
            egl_ref[row:row + 1, hc] = jnp.exp(g_last)
        g_last_rows = jnp.concatenate([jnp.broadcast_to(gl, (CHUNK, DN_DK)) for gl in lasts], axis=0)
        kd_ref[rows, hc] = (k * jnp.exp(g_last_rows - gc)).astype(BF16)
        sols.append(jnp.concatenate([qkv_ref[2, rows, hc] * beta, kb * egc], axis=1))

    mm = lambda a16, b: jnp.dot(a16, b.astype(BF16), preferred_element_type=F32)
    sols = [s - mm(l, s) for l, s in zip(lows, sols)]
    pws = lows
    for _ in range(CHUNK.bit_length() - 2):
        pws = [mm(p, p).astype(BF16) for p in pws]
        sols = [s + mm(p, s) for p, s in zip(pws, sols)]
    for (rows, hh), s in zip(items, sols):
        hc = slice(hh * DN_DK, (hh + 1) * DN_DK)
        u_ref[rows, hc] = s[:, :DN_DK]
        w_ref[rows, hc] = s[:, DN_DK:].astype(BF16)


def _wy(proj, ba, conv_w, alog_pad, dtb_pad, *, seq, ts):
    t = proj.shape[0]
    hw = WY_HEADS * DN_DK
    n_hp = DN_HEADS // WY_HEADS
    dq = DN_HEADS * DN_DK
    cw = conv_w.reshape(CONV_K, 3, n_hp, hw).transpose(2, 0, 1, 3).reshape(n_hp, CONV_K, 3 * hw)
    src = lambda col: pl.BlockSpec((ts, hw), lambda r, h: (r, col // hw + h))
    hblk = pl.BlockSpec((ts, hw), lambda r, h: (r, h))
    wide = jax.ShapeDtypeStruct((t, dq), BF16)
    return pl.pallas_call(
        functools.partial(_wy_kernel, ns=seq // ts),
        grid=(t // ts, n_hp),
        in_specs=[src(COL_Q), src(COL_K), src(COL_V),
                  pl.BlockSpec((ts, LANES), lambda r, h: (r, 0)),
                  pl.BlockSpec((1, CONV_K, 3 * hw), lambda r, h: (h, 0, 0)),
                  _resident((1, LANES)), _resident((1, LANES))],
        out_specs=[hblk, hblk, hblk, hblk, hblk,
                   pl.BlockSpec((ts // CHUNK, hw), lambda r, h: (r, h))],
        out_shape=[jax.ShapeDtypeStruct((t, dq), F32), wide, wide, wide, wide,
                   jax.ShapeDtypeStruct((t // CHUNK, dq), F32)],
        scratch_shapes=[pltpu.VMEM((CONV_HALO + ts, 3 * hw), F32),
                        pltpu.VMEM((n_hp, CONV_HALO, 3 * hw), F32),
                        pltpu.VMEM((2, ts, LANES), F32),
                        pltpu.VMEM((3, ts, hw), F32)],
        compiler_params=_cparams(("arbitrary", "arbitrary")),
        name="wy",
    )(proj, proj, proj, ba, cw, alog_pad, dtb_pad)


def _scan_kernel(u_ref, w_ref, qg_ref, kd_ref, at_ref, egl_ref, z_ref, dnw_ref, out_ref, state_ref):
    ts = u_ref.shape[0]

    @pl.when(pl.program_id(1) == 0)
    def _():
        state_ref[...] = jnp.zeros(state_ref.shape, F32)

    heads = range(DN_HEADS)
    hcs = [slice(h * DN_DK, (h + 1) * DN_DK) for h in heads]
    states = [state_ref[h] for h in heads]
    even_vn = [None] * DN_HEADS
    for c in range(ts // CHUNK):
        rows = slice(c * CHUNK, (c + 1) * CHUNK)
        rs = [jnp.dot(jnp.concatenate([w_ref[rows, hcs[h]], qg_ref[rows, hcs[h]]], axis=0),
                      states[h].astype(BF16), preferred_element_type=F32) for h in heads]
        vns = [(u_ref[rows, hcs[h]] - rs[h][:CHUNK]).astype(BF16) for h in heads]
        states = [states[h] * egl_ref[c:c + 1, hcs[h]] + _dot_tn(kd_ref[rows, hcs[h]], vns[h]) for h in heads]
        for h in heads:
            if c % 2 == 0:
                even_vn[h] = vns[h]
                pair = jnp.concatenate([vns[h], jnp.zeros_like(vns[h])], axis=0)
            else:
                pair = jnp.concatenate([even_vn[h], vns[h]], axis=0)
            o = rs[h][CHUNK:] + jnp.dot(at_ref[rows, hcs[h]], pair, preferred_element_type=F32)
            on = o * lax.rsqrt(jnp.mean(o * o, axis=-1, keepdims=True) + EPS) * dnw_ref[...]
            z = z_ref[rows, hcs[h]].astype(F32)
            out_ref[rows, hcs[h]] = (on * (z * jax.nn.sigmoid(z))).astype(out_ref.dtype)
    for h in heads:
        state_ref[h] = states[h]


def _scan(u, w, qg, kd, at, egl, proj, dn_norm, *, bsz, seq, ts):
    t, dv = u.shape
    ns = seq // ts
    blk = pl.BlockSpec((ts, dv), lambda b, s: (b * ns + s, 0))
    return pl.pallas_call(
        _scan_kernel,
        grid=(bsz, ns),
        in_specs=[blk, blk, blk, blk, blk,
                  pl.BlockSpec((ts // CHUNK, dv), lambda b, s: (b * ns + s, 0)),
                  pl.BlockSpec((ts, dv), lambda b, s: (b * ns + s, COL_Z // dv)),
                  _resident((1, DN_DK))],
        out_specs=blk,
        out_shape=jax.ShapeDtypeStruct((t, dv), BF16),
        scratch_shapes=[pltpu.VMEM((DN_HEADS, DN_DK, DN_DK), F32)],
        compiler_params=_cparams(("parallel", "arbitrary")),
        name="scan",
    )(u, w, qg, kd, at, egl, proj, dn_norm)


ROUTE_E1, ROUTE_E2, ROUTE_W1, ROUTE_W2, ROUTE_R1, ROUTE_R2 = range(6)
ROUTER_LANE0 = N_GROUPS
NEG_BIG = -1e30


def _pack_rows(x):
    n = x.shape[1] // 2
    bits = lambda v: lax.bitcast_convert_type(v.astype(BF16).astype(F32), jnp.uint32)
    return bits(x[:, n:]) | (bits(x[:, :n]) >> 16)


def _unpack_rows(p):
    lo = lax.bitcast_convert_type(p << 16, F32).astype(BF16)
    hi = lax.bitcast_convert_type(p & jnp.uint32(0xFFFF0000), F32).astype(BF16)
    return jnp.concatenate([lo, hi], axis=1)


def _merge_kernel(ydn_ref, gd_ref, mp_ref, x_ref, wupdn_ref, wout_ref, gain_ref, wr_ref, br_ref,
                  h1_ref, n2_ref, route_ref, cnt_ref, carry_ref):
    tm = x_ref.shape[0]

    @pl.when(pl.program_id(0) == 0)
    def _():
        carry_ref[...] = jnp.zeros(carry_ref.shape, F32)

    md = (jax.nn.sigmoid(gd_ref[...].astype(F32))
          * jnp.dot(ydn_ref[...], wupdn_ref[...], preferred_element_type=F32))
    merged = mp_ref[...].astype(F32) + md
    h1 = x_ref[...] + jnp.dot(merged.astype(BF16), wout_ref[...], preferred_element_type=F32)
    h1_ref[...] = h1
    n2 = h1 * lax.rsqrt(jnp.mean(h1 * h1, axis=-1, keepdims=True) + EPS) * gain_ref[...]
    n2_ref[...] = _pack_rows(n2)

    hi = n2.astype(BF16)
    lo = (n2 - hi.astype(F32)).astype(BF16)
    r_hi = jnp.dot(hi, wr_ref[...], preferred_element_type=F32)
    r_lo = jnp.dot(lo, wr_ref[:, :LANES], preferred_element_type=F32)
    logits = r_hi[:, :LANES] + r_hi[:, LANES:] + r_lo + br_ref[...]

    lane = lax.broadcasted_iota(jnp.int32, (tm, LANES), 1)
    lane_f = lane.astype(F32)
    first_max = lambda v, m: jnp.min(jnp.where(v == m, lane_f, float(LANES)), axis=-1, keepdims=True)

    is_grp = lane < N_GROUPS
    lg = jnp.where(is_grp, logits, NEG_BIG)
    gmax = jnp.max(lg, axis=-1, keepdims=True)
    gi = first_max(lg, gmax)
    p_sel = 1.0 / jnp.sum(jnp.where(is_grp, jnp.exp(lg - gmax), 0.0), axis=-1, keepdims=True)

    e_lane = lane - ROUTER_LANE0
    in_grp = (e_lane >= 0) & (e_lane < N_EXPERTS) & ((e_lane // EXPERTS_PER_GROUP).astype(F32) == gi)
    le = jnp.where(in_grp, logits, NEG_BIG)
    v1 = jnp.max(le, axis=-1, keepdims=True)
    i1 = first_max(le, v1)
    le2 = jnp.where(lane_f == i1, NEG_BIG, le)
    v2 = jnp.max(le2, axis=-1, keepdims=True)
    i2 = first_max(le2, v2)
    s = jnp.exp(v2 - v1)
    w1 = p_sel / (1.0 + s)
    w2 = p_sel * s / (1.0 + s)
    e1 = i1 - float(ROUTER_LANE0)
    e2 = i2 - float(ROUTER_LANE0)

    oh1 = jnp.where(lane_f == e1, 1.0, 0.0)
    oh2 = jnp.where(lane_f == e2, 1.0, 0.0)
    both = oh1 + oh2
    ri = lax.broadcasted_iota(jnp.int32, (tm, tm), 0)
    ci = lax.broadcasted_iota(jnp.int32, (tm, tm), 1)
    tri = jnp.where(ri > ci, 1.0, 0.0).astype(BF16)
    prior = jnp.dot(tri, both.astype(BF16), preferred_element_type=F32) + carry_ref[...]
    r1 = jnp.sum(prior * oh1, axis=-1, keepdims=True)
    r2 = jnp.sum(prior * oh2, axis=-1, keepdims=True)
    carry = carry_ref[...] + jnp.sum(both, axis=0, keepdims=True)
    carry_ref[...] = carry
    cnt_ref[...] = carry

    route = jnp.zeros((tm, LANES), F32)
    for ln, val in ((ROUTE_E1, e1), (ROUTE_E2, e2), (ROUTE_W1, w1), (ROUTE_W2, w2),
                    (ROUTE_R1, r1), (ROUTE_R2, r2)):
        route = jnp.where(lane == ln, val, route)
    route_ref[...] = route


def _merge(y_dn, proj, mp, x2d, w_up_dn, w_out, gain, w_router, b_router, *, tm):
    t, d = x2d.shape
    dv = y_dn.shape[1]
    rows = lambda blk: pl.BlockSpec(blk, lambda i: (i, 0))
    return pl.pallas_call(
        _merge_kernel,
        grid=(t // tm,),
        in_specs=[
            rows((tm, dv)),
            pl.BlockSpec((tm, d), lambda i: (i, COL_GATE_DN // d)),
            rows((tm, d)), rows((tm, d)),
            _resident(w_up_dn.shape), _resident(w_out.shape), _resident((1, d)),
            _resident(w_router.shape), _resident((1, LANES)),
        ],
        out_specs=[rows((tm, d)), rows((tm, d // 2)), rows((tm, LANES)),
                   pl.BlockSpec((1, LANES), lambda i: (0, 0))],
        out_shape=[jax.ShapeDtypeStruct((t, d), F32), jax.ShapeDtypeStruct((t, d // 2), jnp.uint32),
                   jax.ShapeDtypeStruct((t, LANES), F32), jax.ShapeDtypeStruct((1, LANES), F32)],
        scratch_shapes=[pltpu.VMEM((1, LANES), F32)],
        compiler_params=_cparams(("arbitrary",)),
        name="merge",
    )(y_dn, proj, mp, x2d, w_up_dn, w_out, gain, w_router, b_router)


N_GATHER_BUF = 3


def _dispatch_kernel(pos_ref, zrow_ref, nu_ref, n2_ref, xs_ref, zbuf, sem, zsem, *, n_blocks):
    tb = n2_ref.shape[0]
    bm = zbuf.shape[0]
    i = pl.program_id(0)

    @pl.when(i == 0)
    def _():
        zbuf[...] = jnp.zeros(zbuf.shape, zbuf.dtype)
        zero_block = lambda row: pltpu.make_async_copy(
            zbuf, xs_ref.at[pl.ds(pl.multiple_of(row, bm), bm)], zsem)
        for start in (True, False):
            for e in range(N_EXPERTS):
                @pl.when(zrow_ref[e] >= 0)
                def _(e=e, start=start):
                    cp = zero_block(jnp.maximum(zrow_ref[e], 0))
                    cp.start() if start else cp.wait()

            def tail(b, _, start=start):
                cp = zero_block(b * bm)
                cp.start() if start else cp.wait()
                return 0
            lax.fori_loop(nu_ref[0], n_blocks, tail, 0)

    for j in range(tb):
        for k in range(2):
            dst = pos_ref[2 * (i * tb + j) + k]
            pltpu.make_async_copy(n2_ref.at[pl.ds(j, 1)], xs_ref.at[pl.ds(dst, 1)],
                                  sem).start(priority=k)
    for _ in range(2):
        pltpu.make_async_copy(n2_ref, xs_ref.at[pl.ds(0, tb)], sem).wait()


def _dispatch(pos, zrow, n_used, n2, *, cap, bm, tb):
    t, dp = n2.shape
    return pl.pallas_call(
        functools.partial(_dispatch_kernel, n_blocks=cap // bm),
        grid_spec=pltpu.PrefetchScalarGridSpec(
            num_scalar_prefetch=3,
            grid=(t // tb,),
            in_specs=[pl.BlockSpec((tb, dp), lambda i, pos, zrow, nu: (i, 0))],
            out_specs=pl.BlockSpec(memory_space=pl.ANY),
            scratch_shapes=[pltpu.VMEM((bm, dp), n2.dtype), pltpu.SemaphoreType.DMA,
                            pltpu.SemaphoreType.DMA],
        ),
        out_shape=jax.ShapeDtypeStruct((cap, dp), n2.dtype),
        compiler_params=_cparams(("arbitrary",)),
        name="dispatch",
    )(pos, zrow, n_used, n2)


def _experts_kernel(be_ref, nu_ref, x_ref, wg_ref, wu_ref, wd_ref, y_ref, wg16, wu16, wd16):
    i = pl.program_id(0)
    nu = nu_ref[0]

    @pl.when((i < nu) & ((i == 0) | (be_ref[i] != be_ref[jnp.maximum(i - 1, 0)])))
    def _():
        wg16[...] = wg_ref[0].astype(BF16)
        wu16[...] = wu_ref[0].astype(BF16)
        wd16[...] = wd_ref[0].astype(BF16)

    @pl.when(i < nu)
    def _():
        xb = _unpack_rows(x_ref[...])
        g = jnp.dot(xb, wg16[...], preferred_element_type=F32)
        u = jnp.dot(xb, wu16[...], preferred_element_type=F32)
        hmid = (g * jax.nn.sigmoid(g)) * u
        y_ref[...] = _pack_rows(jnp.dot(hmid.astype(BF16), wd16[...], preferred_element_type=F32))

    @pl.when(i >= nu)
    def _():
        y_ref[...] = jnp.zeros(y_ref.shape, y_ref.dtype)


def _experts(blk_e, n_used, xs, w_gate, w_up, w_down, *, bm):
    cap, dp = xs.shape
    d, de = w_gate.shape[1], w_gate.shape[2]
    wspec = lambda shape: pl.BlockSpec(shape, lambda i, be, nu: (be[i], 0, 0))
    return pl.pallas_call(
        _experts_kernel,
        grid_spec=pltpu.PrefetchScalarGridSpec(
            num_scalar_prefetch=2,
            grid=(cap // bm,),
            in_specs=[pl.BlockSpec((bm, dp), lambda i, be, nu: (jnp.minimum(i, nu[0] - 1), 0)),
                      wspec((1, d, de)), wspec((1, d, de)), wspec((1, de, d))],
            out_specs=pl.BlockSpec((bm, dp), lambda i, be, nu: (i, 0)),
            scratch_shapes=[pltpu.VMEM((d, de), BF16), pltpu.VMEM((d, de), BF16), pltpu.VMEM((de, d), BF16)],
        ),
        out_shape=jax.ShapeDtypeStruct((cap, dp), jnp.uint32),
        compiler_params=_cparams(("arbitrary",)),
        name="experts",
    )(blk_e, n_used, xs, w_gate, w_up, w_down)


def _combine_kernel(pos_ref, h1_ref, route_ref, p_ref, ys_ref, wpg_ref, wpp_ref, gple_ref, gfin_ref,
                    out_ref, ybuf0, ybuf1, ybuf2, sem, *, final_norm, n_steps):
    bufs = (ybuf0, ybuf1, ybuf2)
    tm = h1_ref.shape[0]
    i = pl.program_id(0)

    def issue(blk, slot):
        for j in range(tm):
            for k in range(2):
                src = pos_ref[2 * (blk * tm + j) + k]
                pltpu.make_async_copy(ys_ref.at[pl.ds(src, 1)], bufs[slot].at[pl.ds(k * tm + j, 1)],
                                      sem.at[slot]).start(priority=k)

    def drain(slot):
        pltpu.make_async_copy(ys_ref.at[pl.ds(0, 2 * tm)], bufs[slot], sem.at[slot]).wait()

    @pl.when(i == 0)
    def _():
        issue(0, 0)
        issue(1, 1)

    for r in range(N_GATHER_BUF):
        @pl.when(i % N_GATHER_BUF == r)
        def _(r=r):
            drain(r)
            issue(i + 2, (r + 2) % N_GATHER_BUF)
            cur = bufs[r]
            route = route_ref[...]
            y_moe = (_unpack_rows(cur[0:tm, :]).astype(F32) * route[:, ROUTE_W1:ROUTE_W1 + 1]
                     + _unpack_rows(cur[tm:2 * tm, :]).astype(F32) * route[:, ROUTE_W2:ROUTE_W2 + 1])
            h2 = h1_ref[...] + y_moe
            n3 = h2 * lax.rsqrt(jnp.mean(h2 * h2, axis=-1, keepdims=True) + EPS) * gple_ref[...]
            gate = jax.nn.sigmoid(jnp.dot(n3.astype(BF16), wpg_ref[...], preferred_element_type=F32))
            h3 = h2 + gate * jnp.dot(p_ref[...].astype(BF16), wpp_ref[...], preferred_element_type=F32)
            if final_norm:
                h3 = h3 * lax.rsqrt(jnp.mean(h3 * h3, axis=-1, keepdims=True) + EPS) * gfin_ref[...]
            out_ref[...] = h3

    last = n_steps - 1

    @pl.when(i == last)
    def _():
        drain((last + 1) % N_GATHER_BUF)
        drain((last + 2) % N_GATHER_BUF)


def _combine(pos, h1, route, p2d, ys, w_ple_gate, w_ple_proj, g_ple, g_fin, *, tm, final_norm):
    t, d = h1.shape
    rows = lambda w: pl.BlockSpec((tm, w), lambda i, pos: (i, 0))
    res = lambda shape: pl.BlockSpec(shape, lambda i, pos: (0,) * len(shape), pipeline_mode=pl.Buffered(1))
    return pl.pallas_call(
        functools.partial(_combine_kernel, final_norm=final_norm, n_steps=t // tm),
        grid_spec=pltpu.PrefetchScalarGridSpec(
            num_scalar_prefetch=1,
            grid=(t // tm,),
            in_specs=[rows(d), rows(LANES), rows(p2d.shape[1]), pl.BlockSpec(memory_space=pl.ANY),
                      res(w_ple_gate.shape), res(w_ple_proj.shape), res((1, d)), res((1, d))],
            out_specs=rows(d),
            scratch_shapes=[pltpu.VMEM((2 * tm, ys.shape[1]), ys.dtype)] * N_GATHER_BUF
                           + [pltpu.SemaphoreType.DMA((N_GATHER_BUF,))],
        ),
        out_shape=jax.ShapeDtypeStruct((t, d), F32),
        compiler_params=_cparams(("arbitrary",)),
        name="combine",
    )(jnp.pad(pos, (0, 2 * 2 * tm)), h1, route, p2d, ys, w_ple_gate, w_ple_proj, g_ple, g_fin)


def _pick(n, pref):
    b = min(n, pref)
    while n % b:
        b -= 8
    return b


def _stage_mixers(x, prm, i):
    bsz, seq, d = x.shape
    t = bsz * seq
    x2d = x.reshape(t, d)
    w_in = prm["w_in"][i]
    d_pool = prm["w_up_pool"].shape[1]
    dq = DN_HEADS * DN_DK
    o_qkv = d_pool
    o_z = o_qkv + 3 * dq
    o_b = o_z + dq
    o_gp = o_b + 2 * DN_HEADS
    o_gd = o_gp + d
    w_main = jnp.concatenate(
        [w_in[:, o_gp:o_gp + d], w_in[:, o_gd:o_gd + d], w_in[:, :d_pool],
         w_in[:, o_qkv:o_qkv + 3 * dq], w_in[:, o_z:o_z + dq]], axis=1).astype(BF16)
    assert w_main.shape[1] == N_MAIN
    w_ba = jnp.pad(w_in[:, o_b:o_b + 2 * DN_HEADS], ((0, 0), (0, LANES - 2 * DN_HEADS))).astype(BF16)
    proj, ba = _inproj(x2d, prm["norm_mix"][i][None, :], w_main, w_ba,
                       tm=_pick(t, 1024), tn=1024)

    ts = _pick(seq, 512)
    mp = _pool(proj, prm["pool_w"][i].astype(BF16), prm["pool_scale"][i][None, :],
               prm["w_up_pool"][i].astype(BF16), bsz=bsz, seq=seq, ts=ts)

    lane_pad = lambda a: jnp.pad(a, (DN_HEADS, LANES - 2 * DN_HEADS))[None, :]
    u, w, qg, kd, at, egl = _wy(proj, ba, prm["conv_w"][i], lane_pad(prm["a_log"][i]),
                                lane_pad(prm["dt_bias"][i]), seq=seq, ts=ts)
    y_dn = _scan(u, w, qg, kd, at, egl, proj, prm["dn_norm"][i][None, :], bsz=bsz, seq=seq, ts=ts)
    return dict(proj=proj, ba=ba, mp=mp, y_dn=y_dn)


MOE_BM = 256


def _router_weights(w_rg, b_rg, w_re, b_re):
    d = w_rg.shape[0]
    w = jnp.concatenate([w_rg, w_re, jnp.zeros((d, LANES - N_GROUPS - N_EXPERTS), F32)], axis=1)
    w_hi = w.astype(BF16)
    w_lo = (w - w_hi.astype(F32)).astype(BF16)
    b = jnp.concatenate([b_rg, b_re, jnp.zeros((LANES - N_GROUPS - N_EXPERTS,), F32)])[None, :]
    return jnp.concatenate([w_hi, w_lo], axis=1), b


def _stage_moe_plan(route, cnt, t):
    counts = cnt[0, :N_EXPERTS].astype(jnp.int32)
    padded = ((counts + MOE_BM - 1) // MOE_BM) * MOE_BM
    pends = jnp.cumsum(padded)
    pstarts = pends - padded
    eid = route[:, ROUTE_E1:ROUTE_E2 + 1].astype(jnp.int32)
    rank = route[:, ROUTE_R1:ROUTE_R2 + 1].astype(jnp.int32)
    pos = (pstarts[eid] + rank).reshape(2 * t)
    cap = 2 * t + N_EXPERTS * MOE_BM
    n_blocks = cap // MOE_BM
    n_used = (pends[-1] // MOE_BM).astype(jnp.int32)
    blk = jnp.minimum(jnp.arange(n_blocks, dtype=jnp.int32), n_used - 1)
    blk_e = jnp.sum((pends[None, :] <= (blk * MOE_BM)[:, None]).astype(jnp.int32), axis=1)
    blk_e = jnp.minimum(blk_e, N_EXPERTS - 1)
    zrow = jnp.where(padded > 0, pends - MOE_BM, -1).astype(jnp.int32)
    return pos, blk_e, n_used.reshape(1), zrow, cap


def kernel(x, p, norm_mix, w_in, pool_w, pool_scale, conv_w, a_log, dt_bias, dn_norm, w_up_pool, w_up_dn, w_out, norm_moe, w_router_group, b_router_group, w_router_expert, b_router_expert, w_gate, w_up, w_down, norm_ple, w_ple_gate, w_ple_proj, norm_final):
    bsz, seq, d = x.shape
    t = bsz * seq
    depth = norm_mix.shape[0]
    prm = dict(norm_mix=norm_mix, w_in=w_in, pool_w=pool_w, pool_scale=pool_scale, conv_w=conv_w,
               a_log=a_log, dt_bias=dt_bias, dn_norm=dn_norm, w_up_pool=w_up_pool)
    h = x
    for i in range(depth):
        st = _stage_mixers(h, prm, i)
        w_router, b_router = _router_weights(w_router_group[i], b_router_group[i],
                                             w_router_expert[i], b_router_expert[i])
        tm = _pick(t, 256)
        h1, n2, route, cnt = _merge(st["y_dn"], st["proj"], st["mp"], h.reshape(t, d),
                                    w_up_dn[i].astype(BF16), w_out[i].astype(BF16), norm_moe[i][None, :],
                                    w_router, b_router, tm=tm)
        pos, blk_e, n_used, zrow, cap = _stage_moe_plan(route, cnt, t)
        xs = _dispatch(pos, zrow, n_used, n2, cap=cap, bm=MOE_BM, tb=tm)
        ys = _experts(blk_e, n_used, xs, w_gate[i], w_up[i], w_down[i], bm=MOE_BM)
        out = _combine(pos, h1, route, p[i].reshape(t, -1), ys, w_ple_gate[i].astype(BF16),
                       w_ple_proj[i].astype(BF16), norm_ple[i][None, :], norm_final[None, :],
                       tm=tm, final_norm=(i == depth - 1))
        h = out.reshape(bsz, seq, d)
    return h
```

```python
import functools

import jax
import jax.numpy as jnp
from jax import lax
from jax.experimental import pallas as pl
from jax.experimental.pallas import tpu as pltpu

F32 = jnp.float32
BF16 = jnp.bfloat16

EPS = 1e-6
POOL_WINDOWS = (2, 4, 8, 16)
POOL_HALO = 32
assert all(w == 2 ** (g + 1) for g, w in enumerate(POOL_WINDOWS)) and POOL_HALO == 8 * len(POOL_WINDOWS)
CONV_K = 4
CONV_HALO = 8
DN_HEADS = 8
DN_DK = 128
CHUNK = 64
N_GROUPS = 4
EXPERTS_PER_GROUP = 8
N_EXPERTS = N_GROUPS * EXPERTS_PER_GROUP
LANES = 128
VMEM_LIMIT = 56 * 1024 * 1024

COL_GATE_POOL = 0
COL_GATE_DN = 2048
COL_POOL = 4096
COL_Q = 5120
COL_K = 6144
COL_V = 7168
COL_Z = 8192
N_MAIN = 9216


def _cparams(sem):
    return pltpu.CompilerParams(dimension_semantics=sem, vmem_limit_bytes=VMEM_LIMIT)


def _resident(shape):
    nd = len(shape)
    return pl.BlockSpec(shape, lambda *_: (0,) * nd, pipeline_mode=pl.Buffered(1))


def _inproj_kernel(x_ref, gain_ref, w_ref, wba_ref, out_ref, ba_ref, n1_ref, *, sub):
    tm = x_ref.shape[0]

    @pl.when(pl.program_id(1) == 0)
    def _():
        def body(r, _):
            rows = pl.ds(pl.multiple_of(r * sub, sub), sub)
            x = x_ref[rows, :]
            y = x * lax.rsqrt(jnp.mean(x * x, axis=-1, keepdims=True) + EPS) * gain_ref[...]
            n1_ref[rows, :] = y.astype(BF16)
            return 0
        lax.fori_loop(0, tm // sub, body, 0)
        ba_ref[...] = jnp.dot(n1_ref[...], wba_ref[...], preferred_element_type=F32)

    out_ref[...] = jnp.dot(n1_ref[...], w_ref[...], preferred_element_type=F32).astype(out_ref.dtype)


def _inproj(x2d, gain, w_main, w_ba, *, tm, tn):
    t, d = x2d.shape
    n = w_main.shape[1]
    return pl.pallas_call(
        functools.partial(_inproj_kernel, sub=min(tm, 256)),
        grid=(t // tm, n // tn),
        in_specs=[
            pl.BlockSpec((tm, d), lambda i, j: (i, 0)),
            _resident((1, d)),
            pl.BlockSpec((d, tn), lambda i, j: (0, j)),
            _resident((d, LANES)),
        ],
        out_specs=[
            pl.BlockSpec((tm, tn), lambda i, j: (i, j)),
            pl.BlockSpec((tm, LANES), lambda i, j: (i, 0)),
        ],
        out_shape=[jax.ShapeDtypeStruct((t, n), BF16), jax.ShapeDtypeStruct((t, LANES), F32)],
        scratch_shapes=[pltpu.VMEM((tm, d), BF16)],
        compiler_params=_cparams(("parallel", "arbitrary")),
        name="inproj",
    )(x2d, gain, w_main, w_ba)


def _pool_kernel(u_ref, gp_ref, pw_ref, scale_ref, wup_ref, out_ref, ext_ref, sa_ref, sb_ref, y_ref):
    ts = u_ref.shape[0]
    gw = pw_ref.shape[1]
    d_pool = ext_ref.shape[1]
    n = POOL_HALO + ts
    s = pl.program_id(1)

    @pl.when(s == 0)
    def _():
        ext_ref[0:POOL_HALO, :] = jnp.zeros((POOL_HALO, d_pool), F32)

    @pl.when(s > 0)
    def _():
        ext_ref[0:POOL_HALO, :] = ext_ref[ts:n, :]

    ext_ref[POOL_HALO:n, :] = u_ref[...].astype(F32)
    t1 = (s * ts + 1 + lax.broadcasted_iota(jnp.int32, (ts, 1), 0)).astype(F32)

    src = ext_ref
    for g, w in enumerate(POOL_WINDOWS):
        half = w // 2
        first = 8 * (g + 1)
        c0 = g * gw
        sw = src[first:n, c0:d_pool] + src[first - half:n - half, c0:d_pool]
        cols = slice(c0, c0 + gw)
        cur = ext_ref[POOL_HALO:n, cols]
        d = sw[POOL_HALO - first:, 0:gw] / jnp.minimum(t1, float(w)) - cur
        y = jnp.dot(d.astype(BF16), pw_ref[g], preferred_element_type=F32) * scale_ref[:, cols]
        y_ref[:, cols] = y.astype(BF16)
        if g + 1 < len(POOL_WINDOWS):
            dst = sa_ref if g % 2 == 0 else sb_ref
            dst[first:n, c0 + gw:d_pool] = sw[:, gw:]
            src = dst
    up = jnp.dot(y_ref[...], wup_ref[...], preferred_element_type=F32)
    out_ref[...] = (jax.nn.sigmoid(gp_ref[...].astype(F32)) * up).astype(out_ref.dtype)


def _pool(proj, pool_w, pool_scale, w_up_pool, *, bsz, seq, ts):
    t = proj.shape[0]
    ns = seq // ts
    d_pool = w_up_pool.shape[0]
    d = w_up_pool.shape[1]
    return pl.pallas_call(
        _pool_kernel,
        grid=(bsz, ns),
        in_specs=[
            pl.BlockSpec((ts, d_pool), lambda b, s: (b * ns + s, COL_POOL // d_pool)),
            pl.BlockSpec((ts, d), lambda b, s: (b * ns + s, COL_GATE_POOL // d)),
            _resident(pool_w.shape),
            _resident((1, d_pool)),
            _resident(w_up_pool.shape),
        ],
        out_specs=pl.BlockSpec((ts, d), lambda b, s: (b * ns + s, 0)),
        out_shape=jax.ShapeDtypeStruct((t, d), BF16),
        scratch_shapes=[pltpu.VMEM((POOL_HALO + ts, d_pool), F32)] * 3 + [pltpu.VMEM((ts, d_pool), BF16)],
        compiler_params=_cparams(("parallel", "arbitrary")),
        name="pool",
    )(proj, proj, pool_w, pool_scale, w_up_pool)


def _dot_nt(a, b):
    return lax.dot_general(a, b, (((1,), (1,)), ((), ())), preferred_element_type=F32)


def _dot_tn(a, b):
    return lax.dot_general(a, b, (((0,), (0,)), ((), ())), preferred_element_type=F32)


GROUP = 256
PAIR = 2 * CHUNK
WY_HEADS = 2


def _wy_kernel(q_ref, k_ref, v_ref, ba_ref, cw_ref, alog_ref, dtb_ref,
               u_ref, w_ref, qg_ref, kd_ref, at_ref, egl_ref, ext, carry_ref, gate_ref, qkv_ref, *, ns):
    ts, hw = q_ref.shape
    hp = pl.program_id(1)
    s = pl.program_id(0) % ns

    @pl.when(s == 0)
    def _():
        ext[0:CONV_HALO, :] = jnp.zeros((CONV_HALO, 3 * hw), F32)

    @pl.when(s > 0)
    def _():
        ext[0:CONV_HALO, :] = carry_ref[hp]

    ext[CONV_HALO:CONV_HALO + ts, 0:hw] = q_ref[...].astype(F32)
    ext[CONV_HALO:CONV_HALO + ts, hw:2 * hw] = k_ref[...].astype(F32)
    ext[CONV_HALO:CONV_HALO + ts, 2 * hw:3 * hw] = v_ref[...].astype(F32)
    carry_ref[hp] = ext[ts:ts + CONV_HALO, :]

    def conv_silu(c0):
        cols = slice(c0, c0 + DN_DK)
        first = CONV_HALO - (CONV_K - 1)
        y = ext[first:first + ts, cols] * cw_ref[0, 0:1, cols]
        for tap in range(1, CONV_K):
            y = y + ext[first + tap:first + tap + ts, cols] * cw_ref[0, tap:tap + 1, cols]
        return y * jax.nn.sigmoid(y)

    @pl.when(hp == 0)
    def _():
        ba = ba_ref[...]
        xs = ba + dtb_ref[...]
        softplus = jnp.maximum(xs, 0.0) + jnp.log1p(jnp.exp(-jnp.abs(xs)))
        g_all = -jnp.exp(alog_ref[...]) * softplus
        ri = lax.broadcasted_iota(jnp.int32, (ts, ts), 0)
        ci = lax.broadcasted_iota(jnp.int32, (ts, ts), 1)
        tri = jnp.where((ri // CHUNK == ci // CHUNK) & (ci <= ri), 1.0, 0.0).astype(F32)
        gate_ref[0] = jax.nn.sigmoid(ba)
        gate_ref[1] = jnp.dot(tri, g_all, preferred_element_type=F32, precision=lax.Precision.HIGHEST)

    beta_all = gate_ref[0]
    gcum = gate_ref[1]
    lane = lax.broadcasted_iota(jnp.int32, (ts, LANES), 1)
    pick = lambda arr, ln: jnp.sum(jnp.where(lane == ln, arr, 0.0), axis=-1, keepdims=True)
    row8 = lax.broadcasted_iota(jnp.int32, (8, LANES), 0)
    lane8 = lax.broadcasted_iota(jnp.int32, (8, LANES), 1)
    pick_rows = jnp.where(lane8 == DN_HEADS + hp * WY_HEADS + row8, 1.0, 0.0).astype(F32)
    gr_all = lax.dot_general(pick_rows, gcum, (((1,), (1,)), ((), ())), preferred_element_type=F32,
                             precision=lax.Precision.HIGHEST)

    betas, gcs = [], []
    for hh in range(WY_HEADS):
        hc = slice(hh * DN_DK, (hh + 1) * DN_DK)
        qh = conv_silu(hh * DN_DK)
        qkv_ref[0, :, hc] = qh * (lax.rsqrt(jnp.sum(qh * qh, axis=-1, keepdims=True) + EPS) * (DN_DK ** -0.5))
        kh = conv_silu(hw + hh * DN_DK)
        qkv_ref[1, :, hc] = kh * lax.rsqrt(jnp.sum(kh * kh, axis=-1, keepdims=True) + EPS)
        qkv_ref[2, :, hc] = conv_silu(2 * hw + hh * DN_DK)
        head = hp * WY_HEADS + hh
        betas.append(jnp.broadcast_to(pick(beta_all, head), (ts, DN_DK)))
        gcs.append(jnp.broadcast_to(pick(gcum, DN_HEADS + head), (ts, DN_DK)))

    ii = lax.broadcasted_iota(jnp.int32, (GROUP, GROUP), 0)
    jj = lax.broadcasted_iota(jnp.int32, (GROUP, GROUP), 1)
    same = (ii // CHUNK) == (jj // CHUNK)
    incl = same & (ii >= jj)
    strict = same & (ii > jj)

    items = [(slice(g * GROUP, (g + 1) * GROUP), hh) for hh in range(WY_HEADS)
             for g in range(ts // GROUP)]
    lows, sols = [], []
    for rows, hh in items:
        hc = slice(hh * DN_DK, (hh + 1) * DN_DK)
        q = qkv_ref[0, rows, hc]
        k = qkv_ref[1, rows, hc]
        beta = betas[hh][rows, :]
        gc = gcs[hh][rows, :]
        gr = gr_all[hh:hh + 1, rows]
        decay = jnp.where(incl, jnp.exp(jnp.concatenate([gc, gc], axis=1) - gr), 0.0)
        kb = k * beta
        a = _dot_nt(jnp.concatenate([kb, q], axis=0).astype(BF16), k.astype(BF16))
        lows.append(jnp.where(strict, a[:GROUP] * decay, 0.0).astype(BF16))
        attn = a[GROUP:] * decay
        for j in range(GROUP // PAIR):
            pr = slice(j * PAIR, (j + 1) * PAIR)
            at_ref[rows.start + j * PAIR:rows.start + (j + 1) * PAIR, hc] = attn[pr, pr].astype(BF16)
        egc = jnp.exp(gc)
        qg_ref[rows, hc] = (q * egc).astype(BF16)
        lasts = [gc[(c + 1) * CHUNK - 1:(c + 1) * CHUNK, :] for c in range(GROUP // CHUNK)]
        for c, g_last in enumerate(lasts):
            row = rows.start // CHUNK + c
            egl_ref[row:row + 1, hc] = jnp.exp(g_last)
        g_last_rows = jnp.concatenate([jnp.broadcast_to(gl, (CHUNK, DN_DK)) for gl in lasts], axis=0)
        kd_ref[rows, hc] = (k * jnp.exp(g_last_rows - gc)).astype(BF16)
        sols.append(jnp.concatenate([qkv_ref[2, rows, hc] * beta, kb * egc], axis=1))

    mm = lambda a16, b: jnp.dot(a16, b.astype(BF16), preferred_element_type=F32)
    sols = [s - mm(l, s) for l, s in zip(lows, sols)]
    pws = lows
    for _ in range(CHUNK.bit_length() - 2):
        pws = [mm(p, p).astype(BF16) for p in pws]
        sols = [s + mm(p, s) for p, s in zip(pws, sols)]
    for (rows, hh), s in zip(items, sols):
        hc = slice(hh * DN_DK, (hh + 1) * DN_DK)
        u_ref[rows, hc] = s[:, :DN_DK]
        w_ref[rows, hc] = s[:, DN_DK:].astype(BF16)


def _wy(proj, ba, conv_w, alog_pad, dtb_pad, *, seq, ts):
    t = proj.shape[0]
    hw = WY_HEADS * DN_DK
    n_hp = DN_HEADS // WY_HEADS
    dq = DN_HEADS * DN_DK
    cw = conv_w.reshape(CONV_K, 3, n_hp, hw).transpose(2, 0, 1, 3).reshape(n_hp, CONV_K, 3 * hw)
    src = lambda col: pl.BlockSpec((ts, hw), lambda r, h: (r, col // hw + h))
    hblk = pl.BlockSpec((ts, hw), lambda r, h: (r, h))
    wide = jax.ShapeDtypeStruct((t, dq), BF16)
    return pl.pallas_call(
        functools.partial(_wy_kernel, ns=seq // ts),
        grid=(t // ts, n_hp),
        in_specs=[src(COL_Q), src(COL_K), src(COL_V),
                  pl.BlockSpec((ts, LANES), lambda r, h: (r, 0)),
                  pl.BlockSpec((1, CONV_K, 3 * hw), lambda r, h: (h, 0, 0)),
                  _resident((1, LANES)), _resident((1, LANES))],
        out_specs=[hblk, hblk, hblk, hblk, hblk,
                   pl.BlockSpec((ts // CHUNK, hw), lambda r, h: (r, h))],
        out_shape=[jax.ShapeDtypeStruct((t, dq), F32), wide, wide, wide, wide,
                   jax.ShapeDtypeStruct((t // CHUNK, dq), F32)],
        scratch_shapes=[pltpu.VMEM((CONV_HALO + ts, 3 * hw), F32),
                        pltpu.VMEM((n_hp, CONV_HALO, 3 * hw), F32),
                        pltpu.VMEM((2, ts, LANES), F32),
                        pltpu.VMEM((3, ts, hw), F32)],
        compiler_params=_cparams(("arbitrary", "arbitrary")),
        name="wy",
    )(proj, proj, proj, ba, cw, alog_pad, dtb_pad)


def _scan_kernel(u_ref, w_ref, qg_ref, kd_ref, at_ref, egl_ref, z_ref, dnw_ref, out_ref, state_ref):
    ts = u_ref.shape[0]

    @pl.when(pl.program_id(1) == 0)
    def _():
        state_ref[...] = jnp.zeros(state_ref.shape, F32)

    heads = range(DN_HEADS)
    hcs = [slice(h * DN_DK, (h + 1) * DN_DK) for h in heads]
    states = [state_ref[h] for h in heads]
    even_vn = [None] * DN_HEADS
    for c in range(ts // CHUNK):
        rows = slice(c * CHUNK, (c + 1) * CHUNK)
        rs = [jnp.dot(jnp.concatenate([w_ref[rows, hcs[h]], qg_ref[rows, hcs[h]]], axis=0),
                      states[h].astype(BF16), preferred_element_type=F32) for h in heads]
        vns = [(u_ref[rows, hcs[h]] - rs[h][:CHUNK]).astype(BF16) for h in heads]
        states = [states[h] * egl_ref[c:c + 1, hcs[h]] + _dot_tn(kd_ref[rows, hcs[h]], vns[h]) for h in heads]
        for h in heads:
            if c % 2 == 0:
                even_vn[h] = vns[h]
                pair = jnp.concatenate([vns[h], jnp.zeros_like(vns[h])], axis=0)
            else:
                pair = jnp.concatenate([even_vn[h], vns[h]], axis=0)
            o = rs[h][CHUNK:] + jnp.dot(at_ref[rows, hcs[h]], pair, preferred_element_type=F32)
            on = o * lax.rsqrt(jnp.mean(o * o, axis=-1, keepdims=True) + EPS) * dnw_ref[...]
            z = z_ref[rows, hcs[h]].astype(F32)
            out_ref[rows, hcs[h]] = (on * (z * jax.nn.sigmoid(z))).astype(out_ref.dtype)
    for h in heads:
        state_ref[h] = states[h]


def _scan(u, w, qg, kd, at, egl, proj, dn_norm, *, bsz, seq, ts):
    t, dv = u.shape
    ns = seq // ts
    blk = pl.BlockSpec((ts, dv), lambda b, s: (b * ns + s, 0))
    return pl.pallas_call(
        _scan_kernel,
        grid=(bsz, ns),
        in_specs=[blk, blk, blk, blk, blk,
                  pl.BlockSpec((ts // CHUNK, dv), lambda b, s: (b * ns + s, 0)),
                  pl.BlockSpec((ts, dv), lambda b, s: (b * ns + s, COL_Z // dv)),
                  _resident((1, DN_DK))],
        out_specs=blk,
        out_shape=jax.ShapeDtypeStruct((t, dv), BF16),
        scratch_shapes=[pltpu.VMEM((DN_HEADS, DN_DK, DN_DK), F32)],
        compiler_params=_cparams(("parallel", "arbitrary")),
        name="scan",
    )(u, w, qg, kd, at, egl, proj, dn_norm)


ROUTE_E1, ROUTE_E2, ROUTE_W1, ROUTE_W2, ROUTE_R1, ROUTE_R2 = range(6)
ROUTER_LANE0 = N_GROUPS
NEG_BIG = -1e30


def _pack_rows(x):
    n = x.shape[1] // 2
    bits = lambda v: lax.bitcast_convert_type(v.astype(BF16).astype(F32), jnp.uint32)
    return bits(x[:, n:]) | (bits(x[:, :n]) >> 16)


def _unpack_rows(p):
    lo = lax.bitcast_convert_type(p << 16, F32).astype(BF16)
    hi = lax.bitcast_convert_type(p & jnp.uint32(0xFFFF0000), F32).astype(BF16)
    return jnp.concatenate([lo, hi], axis=1)


def _merge_kernel(ydn_ref, gd_ref, mp_ref, x_ref, wupdn_ref, wout_ref, gain_ref, wr_ref, br_ref,
                  h1_ref, n2_ref, route_ref, cnt_ref, carry_ref):
    tm = x_ref.shape[0]

    @pl.when(pl.program_id(0) == 0)
    def _():
        carry_ref[...] = jnp.zeros(carry_ref.shape, F32)

    md = (jax.nn.sigmoid(gd_ref[...].astype(F32))
          * jnp.dot(ydn_ref[...], wupdn_ref[...], preferred_element_type=F32))
    merged = mp_ref[...].astype(F32) + md
    h1 = x_ref[...] + jnp.dot(merged.astype(BF16), wout_ref[...], preferred_element_type=F32)
    h1_ref[...] = h1
    n2 = h1 * lax.rsqrt(jnp.mean(h1 * h1, axis=-1, keepdims=True) + EPS) * gain_ref[...]
    n2_ref[...] = _pack_rows(n2)

    hi = n2.astype(BF16)
    lo = (n2 - hi.astype(F32)).astype(BF16)
    r_hi = jnp.dot(hi, wr_ref[...], preferred_element_type=F32)
    r_lo = jnp.dot(lo, wr_ref[:, :LANES], preferred_element_type=F32)
    logits = r_hi[:, :LANES] + r_hi[:, LANES:] + r_lo + br_ref[...]

    lane = lax.broadcasted_iota(jnp.int32, (tm, LANES), 1)
    lane_f = lane.astype(F32)
    first_max = lambda v, m: jnp.min(jnp.where(v == m, lane_f, float(LANES)), axis=-1, keepdims=True)

    is_grp = lane < N_GROUPS
    lg = jnp.where(is_grp, logits, NEG_BIG)
    gmax = jnp.max(lg, axis=-1, keepdims=True)
    gi = first_max(lg, gmax)
    p_sel = 1.0 / jnp.sum(jnp.where(is_grp, jnp.exp(lg - gmax), 0.0), axis=-1, keepdims=True)

    e_lane = lane - ROUTER_LANE0
    in_grp = (e_lane >= 0) & (e_lane < N_EXPERTS) & ((e_lane // EXPERTS_PER_GROUP).astype(F32) == gi)
    le = jnp.where(in_grp, logits, NEG_BIG)
    v1 = jnp.max(le, axis=-1, keepdims=True)
    i1 = first_max(le, v1)
    le2 = jnp.where(lane_f == i1, NEG_BIG, le)
    v2 = jnp.max(le2, axis=-1, keepdims=True)
    i2 = first_max(le2, v2)
    s = jnp.exp(v2 - v1)
    w1 = p_sel / (1.0 + s)
    w2 = p_sel * s / (1.0 + s)
    e1 = i1 - float(ROUTER_LANE0)
    e2 = i2 - float(ROUTER_LANE0)

    oh1 = jnp.where(lane_f == e1, 1.0, 0.0)
    oh2 = jnp.where(lane_f == e2, 1.0, 0.0)
    both = oh1 + oh2
    ri = lax.broadcasted_iota(jnp.int32, (tm, tm), 0)
    ci = lax.broadcasted_iota(jnp.int32, (tm, tm), 1)
    tri = jnp.where(ri > ci, 1.0, 0.0).astype(BF16)
    prior = jnp.dot(tri, both.astype(BF16), preferred_element_type=F32) + carry_ref[...]
    r1 = jnp.sum(prior * oh1, axis=-1, keepdims=True)
    r2 = jnp.sum(prior * oh2, axis=-1, keepdims=True)
    carry = carry_ref[...] + jnp.sum(both, axis=0, keepdims=True)
    carry_ref[...] = carry
    cnt_ref[...] = carry

    route = jnp.zeros((tm, LANES), F32)
    for ln, val in ((ROUTE_E1, e1), (ROUTE_E2, e2), (ROUTE_W1, w1), (ROUTE_W2, w2),
                    (ROUTE_R1, r1), (ROUTE_R2, r2)):
        route = jnp.where(lane == ln, val, route)
    route_ref[...] = route


def _merge(y_dn, proj, mp, x2d, w_up_dn, w_out, gain, w_router, b_router, *, tm):
    t, d = x2d.shape
    dv = y_dn.shape[1]
    rows = lambda blk: pl.BlockSpec(blk, lambda i: (i, 0))
    return pl.pallas_call(
        _merge_kernel,
        grid=(t // tm,),
        in_specs=[
            rows((tm, dv)),
            pl.BlockSpec((tm, d), lambda i: (i, COL_GATE_DN // d)),
            rows((tm, d)), rows((tm, d)),
            _resident(w_up_dn.shape), _resident(w_out.shape), _resident((1, d)),
            _resident(w_router.shape), _resident((1, LANES)),
        ],
        out_specs=[rows((tm, d)), rows((tm, d // 2)), rows((tm, LANES)),
                   pl.BlockSpec((1, LANES), lambda i: (0, 0))],
        out_shape=[jax.ShapeDtypeStruct((t, d), F32), jax.ShapeDtypeStruct((t, d // 2), jnp.uint32),
                   jax.ShapeDtypeStruct((t, LANES), F32), jax.ShapeDtypeStruct((1, LANES), F32)],
        scratch_shapes=[pltpu.VMEM((1, LANES), F32)],
        compiler_params=_cparams(("arbitrary",)),
        name="merge",
    )(y_dn, proj, mp, x2d, w_up_dn, w_out, gain, w_router, b_router)


N_GATHER_BUF = 3


def _dispatch_kernel(pos_ref, zrow_ref, nu_ref, n2_ref, xs_ref, zbuf, sem, zsem, *, n_blocks):
    tb = n2_ref.shape[0]
    bm = zbuf.shape[0]
    i = pl.program_id(0)

    @pl.when(i == 0)
    def _():
        zbuf[...] = jnp.zeros(zbuf.shape, zbuf.dtype)
        zero_block = lambda row: pltpu.make_async_copy(
            zbuf, xs_ref.at[pl.ds(pl.multiple_of(row, bm), bm)], zsem)
        for start in (True, False):
            for e in range(N_EXPERTS):
                @pl.when(zrow_ref[e] >= 0)
                def _(e=e, start=start):
                    cp = zero_block(jnp.maximum(zrow_ref[e], 0))
                    cp.start() if start else cp.wait()

            def tail(b, _, start=start):
                cp = zero_block(b * bm)
                cp.start() if start else cp.wait()
                return 0
            lax.fori_loop(nu_ref[0], n_blocks, tail, 0)

    for j in range(tb):
        for k in range(2):
            dst = pos_ref[2 * (i * tb + j) + k]
            pltpu.make_async_copy(n2_ref.at[pl.ds(j, 1)], xs_ref.at[pl.ds(dst, 1)],
                                  sem).start(priority=k)
    for _ in range(2):
        pltpu.make_async_copy(n2_ref, xs_ref.at[pl.ds(0, tb)], sem).wait()


def _dispatch(pos, zrow, n_used, n2, *, cap, bm, tb):
    t, dp = n2.shape
    return pl.pallas_call(
        functools.partial(_dispatch_kernel, n_blocks=cap // bm),
        grid_spec=pltpu.PrefetchScalarGridSpec(
            num_scalar_prefetch=3,
            grid=(t // tb,),
            in_specs=[pl.BlockSpec((tb, dp), lambda i, pos, zrow, nu: (i, 0))],
            out_specs=pl.BlockSpec(memory_space=pl.ANY),
            scratch_shapes=[pltpu.VMEM((bm, dp), n2.dtype), pltpu.SemaphoreType.DMA,
                            pltpu.SemaphoreType.DMA],
        ),
        out_shape=jax.ShapeDtypeStruct((cap, dp), n2.dtype),
        compiler_params=_cparams(("arbitrary",)),
        name="dispatch",
    )(pos, zrow, n_used, n2)


def _experts_kernel(be_ref, nu_ref, x_ref, wg_ref, wu_ref, wd_ref, y_ref, wg16, wu16, wd16):
    i = pl.program_id(0)
    nu = nu_ref[0]

    @pl.when((i < nu) & ((i == 0) | (be_ref[i] != be_ref[jnp.maximum(i - 1, 0)])))
    def _():
        wg16[...] = wg_ref[0].astype(BF16)
        wu16[...] = wu_ref[0].astype(BF16)
        wd16[...] = wd_ref[0].astype(BF16)

    @pl.when(i < nu)
    def _():
        xb = _unpack_rows(x_ref[...])
        g = jnp.dot(xb, wg16[...], preferred_element_type=F32)
        u = jnp.dot(xb, wu16[...], preferred_element_type=F32)
        hmid = (g * jax.nn.sigmoid(g)) * u
        y_ref[...] = _pack_rows(jnp.dot(hmid.astype(BF16), wd16[...], preferred_element_type=F32))

    @pl.when(i >= nu)
    def _():
        y_ref[...] = jnp.zeros(y_ref.shape, y_ref.dtype)


def _experts(blk_e, n_used, xs, w_gate, w_up, w_down, *, bm):
    cap, dp = xs.shape
    d, de = w_gate.shape[1], w_gate.shape[2]
    wspec = lambda shape: pl.BlockSpec(shape, lambda i, be, nu: (be[i], 0, 0))
    return pl.pallas_call(
        _experts_kernel,
        grid_spec=pltpu.PrefetchScalarGridSpec(
            num_scalar_prefetch=2,
            grid=(cap // bm,),
            in_specs=[pl.BlockSpec((bm, dp), lambda i, be, nu: (jnp.minimum(i, nu[0] - 1), 0)),
                      wspec((1, d, de)), wspec((1, d, de)), wspec((1, de, d))],
            out_specs=pl.BlockSpec((bm, dp), lambda i, be, nu: (i, 0)),
            scratch_shapes=[pltpu.VMEM((d, de), BF16), pltpu.VMEM((d, de), BF16), pltpu.VMEM((de, d), BF16)],
        ),
        out_shape=jax.ShapeDtypeStruct((cap, dp), jnp.uint32),
        compiler_params=_cparams(("arbitrary",)),
        name="experts",
    )(blk_e, n_used, xs, w_gate, w_up, w_down)


def _combine_kernel(pos_ref, h1_ref, route_ref, p_ref, ys_ref, wpg_ref, wpp_ref, gple_ref, gfin_ref,
                    out_ref, ybuf0, ybuf1, ybuf2, sem, *, final_norm, n_steps):
    bufs = (ybuf0, ybuf1, ybuf2)
    tm = h1_ref.shape[0]
    i = pl.program_id(0)

    def issue(blk, slot):
        for j in range(tm):
            for k in range(2):
                src = pos_ref[2 * (blk * tm + j) + k]
                pltpu.make_async_copy(ys_ref.at[pl.ds(src, 1)], bufs[slot].at[pl.ds(k * tm + j, 1)],
                                      sem.at[slot]).start(priority=k)

    def drain(slot):
        pltpu.make_async_copy(ys_ref.at[pl.ds(0, 2 * tm)], bufs[slot], sem.at[slot]).wait()

    @pl.when(i == 0)
    def _():
        issue(0, 0)
        issue(1, 1)

    for r in range(N_GATHER_BUF):
        @pl.when(i % N_GATHER_BUF == r)
        def _(r=r):
            drain(r)
            issue(i + 2, (r + 2) % N_GATHER_BUF)
            cur = bufs[r]
            route = route_ref[...]
            y_moe = (_unpack_rows(cur[0:tm, :]).astype(F32) * route[:, ROUTE_W1:ROUTE_W1 + 1]
                     + _unpack_rows(cur[tm:2 * tm, :]).astype(F32) * route[:, ROUTE_W2:ROUTE_W2 + 1])
            h2 = h1_ref[...] + y_moe
            n3 = h2 * lax.rsqrt(jnp.mean(h2 * h2, axis=-1, keepdims=True) + EPS) * gple_ref[...]
            gate = jax.nn.sigmoid(jnp.dot(n3.astype(BF16), wpg_ref[...], preferred_element_type=F32))
            h3 = h2 + gate * jnp.dot(p_ref[...].astype(BF16), wpp_ref[...], preferred_element_type=F32)
            if final_norm:
                h3 = h3 * lax.rsqrt(jnp.mean(h3 * h3, axis=-1, keepdims=True) + EPS) * gfin_ref[...]
            out_ref[...] = h3

    last = n_steps - 1

    @pl.when(i == last)
    def _():
        drain((last + 1) % N_GATHER_BUF)
        drain((last + 2) % N_GATHER_BUF)


def _combine(pos, h1, route, p2d, ys, w_ple_gate, w_ple_proj, g_ple, g_fin, *, tm, final_norm):
    t, d = h1.shape
    rows = lambda w: pl.BlockSpec((tm, w), lambda i, pos: (i, 0))
    res = lambda shape: pl.BlockSpec(shape, lambda i, pos: (0,) * len(shape), pipeline_mode=pl.Buffered(1))
    return pl.pallas_call(
        functools.partial(_combine_kernel, final_norm=final_norm, n_steps=t // tm),
        grid_spec=pltpu.PrefetchScalarGridSpec(
            num_scalar_prefetch=1,
            grid=(t // tm,),
            in_specs=[rows(d), rows(LANES), rows(p2d.shape[1]), pl.BlockSpec(memory_space=pl.ANY),
                      res(w_ple_gate.shape), res(w_ple_proj.shape), res((1, d)), res((1, d))],
            out_specs=rows(d),
            scratch_shapes=[pltpu.VMEM((2 * tm, ys.shape[1]), ys.dtype)] * N_GATHER_BUF
                           + [pltpu.SemaphoreType.DMA((N_GATHER_BUF,))],
        ),
        out_shape=jax.ShapeDtypeStruct((t, d), F32),
        compiler_params=_cparams(("arbitrary",)),
        name="combine",
    )(jnp.pad(pos, (0, 2 * 2 * tm)), h1, route, p2d, ys, w_ple_gate, w_ple_proj, g_ple, g_fin)


def _pick(n, pref):
    b = min(n, pref)
    while n % b:
        b -= 8
    return b


def _stage_mixers(x, prm, i):
    bsz, seq, d = x.shape
    t = bsz * seq
    x2d = x.reshape(t, d)
    w_in = prm["w_in"][i]
    d_pool = prm["w_up_pool"].shape[1]
    dq = DN_HEADS * DN_DK
    o_qkv = d_pool
    o_z = o_qkv + 3 * dq
    o_b = o_z + dq
    o_gp = o_b + 2 * DN_HEADS
    o_gd = o_gp + d
    w_main = jnp.concatenate(
        [w_in[:, o_gp:o_gp + d], w_in[:, o_gd:o_gd + d], w_in[:, :d_pool],
         w_in[:, o_qkv:o_qkv + 3 * dq], w_in[:, o_z:o_z + dq]], axis=1).astype(BF16)
    assert w_main.shape[1] == N_MAIN
    w_ba = jnp.pad(w_in[:, o_b:o_b + 2 * DN_HEADS], ((0, 0), (0, LANES - 2 * DN_HEADS))).astype(BF16)
    proj, ba = _inproj(x2d, prm["norm_mix"][i][None, :], w_main, w_ba,
                       tm=_pick(t, 1024), tn=1536)

    ts = _pick(seq, 512)
    mp = _pool(proj, prm["pool_w"][i].astype(BF16), prm["pool_scale"][i][None, :],
               prm["w_up_pool"][i].astype(BF16), bsz=bsz, seq=seq, ts=ts)

    lane_pad = lambda a: jnp.pad(a, (DN_HEADS, LANES - 2 * DN_HEADS))[None, :]
    u, w, qg, kd, at, egl = _wy(proj, ba, prm["conv_w"][i], lane_pad(prm["a_log"][i]),
                                lane_pad(prm["dt_bias"][i]), seq=seq, ts=ts)
    y_dn = _scan(u, w, qg, kd, at, egl, proj, prm["dn_norm"][i][None, :], bsz=bsz, seq=seq, ts=ts)
    return dict(proj=proj, ba=ba, mp=mp, y_dn=y_dn)


MOE_BM = 256


def _router_weights(w_rg, b_rg, w_re, b_re):
    d = w_rg.shape[0]
    w = jnp.concatenate([w_rg, w_re, jnp.zeros((d, LANES - N_GROUPS - N_EXPERTS), F32)], axis=1)
    w_hi = w.astype(BF16)
    w_lo = (w - w_hi.astype(F32)).astype(BF16)
    b = jnp.concatenate([b_rg, b_re, jnp.zeros((LANES - N_GROUPS - N_EXPERTS,), F32)])[None, :]
    return jnp.concatenate([w_hi, w_lo], axis=1), b


def _stage_moe_plan(route, cnt, t):
    counts = cnt[0, :N_EXPERTS].astype(jnp.int32)
    padded = ((counts + MOE_BM - 1) // MOE_BM) * MOE_BM
    pends = jnp.cumsum(padded)
    pstarts = pends - padded
    eid = route[:, ROUTE_E1:ROUTE_E2 + 1].astype(jnp.int32)
    rank = route[:, ROUTE_R1:ROUTE_R2 + 1].astype(jnp.int32)
    pos = (pstarts[eid] + rank).reshape(2 * t)
    cap = 2 * t + N_EXPERTS * MOE_BM
    n_blocks = cap // MOE_BM
    n_used = (pends[-1] // MOE_BM).astype(jnp.int32)
    blk = jnp.minimum(jnp.arange(n_blocks, dtype=jnp.int32), n_used - 1)
    blk_e = jnp.sum((pends[None, :] <= (blk * MOE_BM)[:, None]).astype(jnp.int32), axis=1)
    blk_e = jnp.minimum(blk_e, N_EXPERTS - 1)
    zrow = jnp.where(padded > 0, pends - MOE_BM, -1).astype(jnp.int32)
    return pos, blk_e, n_used.reshape(1), zrow, cap


def kernel(x, p, norm_mix, w_in, pool_w, pool_scale, conv_w, a_log, dt_bias, dn_norm, w_up_pool, w_up_dn, w_out, norm_moe, w_router_group, b_router_group, w_router_expert, b_router_expert, w_gate, w_up, w_down, norm_ple, w_ple_gate, w_ple_proj, norm_final):
    bsz, seq, d = x.shape
    t = bsz * seq
    depth = norm_mix.shape[0]
    prm = dict(norm_mix=norm_mix, w_in=w_in, pool_w=pool_w, pool_scale=pool_scale, conv_w=conv_w,
               a_log=a_log, dt_bias=dt_bias, dn_norm=dn_norm, w_up_pool=w_up_pool)
    h = x
    for i in range(depth):
        st = _stage_mixers(h, prm, i)
        w_router, b_router = _router_weights(w_router_group[i], b_router_group[i],
                                             w_router_expert[i], b_router_expert[i])
        tm = _pick(t, 256)
        h1, n2, route, cnt = _merge(st["y_dn"], st["proj"], st["mp"], h.reshape(t, d),
                                    w_up_dn[i].astype(BF16), w_out[i].astype(BF16), norm_moe[i][None, :],
                                    w_router, b_router, tm=tm)
        pos, blk_e, n_used, zrow, cap = _stage_moe_plan(route, cnt, t)
        xs = _dispatch(pos, zrow, n_used, n2, cap=cap, bm=MOE_BM, tb=tm)
        ys = _experts(blk_e, n_used, xs, w_gate[i], w_up[i], w_down[i], bm=MOE_BM)
        out = _combine(pos, h1, route, p[i].reshape(t, -1), ys, w_ple_gate[i].astype(BF16),
                       w_ple_proj[i].astype(BF16), norm_ple[i][None, :], norm_final[None, :],
                       tm=tm, final_norm=(i == depth - 1))
        h = out.reshape(bsz, seq, d)
    return h
```

```python
import functools

import jax
import jax.numpy as jnp
from jax import lax
from jax.experimental import pallas as pl
from jax.experimental.pallas import tpu as pltpu

F32 = jnp.float32
BF16 = jnp.bfloat16

EPS = 1e-6
POOL_WINDOWS = (2, 4, 8, 16)
POOL_HALO = 32
assert all(w == 2 ** (g + 1) for g, w in enumerate(POOL_WINDOWS)) and POOL_HALO == 8 * len(POOL_WINDOWS)
CONV_K = 4
CONV_HALO = 8
DN_HEADS = 8
DN_DK = 128
CHUNK = 64
N_GROUPS = 4
EXPERTS_PER_GROUP = 8
N_EXPERTS = N_GROUPS * EXPERTS_PER_GROUP
LANES = 128
VMEM_LIMIT = 56 * 1024 * 1024

COL_GATE_POOL = 0
COL_GATE_DN = 2048
COL_POOL = 4096
COL_Q = 5120
COL_K = 6144
COL_V = 7168
COL_Z = 8192
N_MAIN = 9216


def _cparams(sem):
    return pltpu.CompilerParams(dimension_semantics=sem, vmem_limit_bytes=VMEM_LIMIT)


def _resident(shape):
    nd = len(shape)
    return pl.BlockSpec(shape, lambda *_: (0,) * nd, pipeline_mode=pl.Buffered(1))


def _inproj_kernel(x_ref, gain_ref, w_ref, wba_ref, out_ref, ba_ref, n1_ref, *, sub):
    tm = x_ref.shape[0]

    @pl.when(pl.program_id(1) == 0)
    def _():
        def body(r, _):
            rows = pl.ds(pl.multiple_of(r * sub, sub), sub)
            x = x_ref[rows, :]
            y = x * lax.rsqrt(jnp.mean(x * x, axis=-1, keepdims=True) + EPS) * gain_ref[...]
            n1_ref[rows, :] = y.astype(BF16)
            return 0
        lax.fori_loop(0, tm // sub, body, 0)
        ba_ref[...] = jnp.dot(n1_ref[...], wba_ref[...], preferred_element_type=F32)

    out_ref[...] = jnp.dot(n1_ref[...], w_ref[...], preferred_element_type=F32).astype(out_ref.dtype)


def _inproj(x2d, gain, w_main, w_ba, *, tm, tn):
    t, d = x2d.shape
    n = w_main.shape[1]
    return pl.pallas_call(
        functools.partial(_inproj_kernel, sub=min(tm, 256)),
        grid=(t // tm, n // tn),
        in_specs=[
            pl.BlockSpec((tm, d), lambda i, j: (i, 0)),
            _resident((1, d)),
            pl.BlockSpec((d, tn), lambda i, j: (0, j)),
            _resident((d, LANES)),
        ],
        out_specs=[
            pl.BlockSpec((tm, tn), lambda i, j: (i, j)),
            pl.BlockSpec((tm, LANES), lambda i, j: (i, 0)),
        ],
        out_shape=[jax.ShapeDtypeStruct((t, n), BF16), jax.ShapeDtypeStruct((t, LANES), F32)],
        scratch_shapes=[pltpu.VMEM((tm, d), BF16)],
        compiler_params=_cparams(("parallel", "arbitrary")),
        name="inproj",
    )(x2d, gain, w_main, w_ba)


def _pool_kernel(u_ref, gp_ref, pw_ref, scale_ref, wup_ref, out_ref, ext_ref, sa_ref, sb_ref, y_ref):
    ts = u_ref.shape[0]
    gw = pw_ref.shape[1]
    d_pool = ext_ref.shape[1]
    n = POOL_HALO + ts
    s = pl.program_id(1)

    @pl.when(s == 0)
    def _():
        ext_ref[0:POOL_HALO, :] = jnp.zeros((POOL_HALO, d_pool), F32)

    @pl.when(s > 0)
    def _():
        ext_ref[0:POOL_HALO, :] = ext_ref[ts:n, :]

    ext_ref[POOL_HALO:n, :] = u_ref[...].astype(F32)
    t1 = (s * ts + 1 + lax.broadcasted_iota(jnp.int32, (ts, 1), 0)).astype(F32)

    src = ext_ref
    for g, w in enumerate(POOL_WINDOWS):
        half = w // 2
        first = 8 * (g + 1)
        c0 = g * gw
        sw = src[first:n, c0:d_pool] + src[first - half:n - half, c0:d_pool]
        cols = slice(c0, c0 + gw)
        cur = ext_ref[POOL_HALO:n, cols]
        d = sw[POOL_HALO - first:, 0:gw] / jnp.minimum(t1, float(w)) - cur
        y = jnp.dot(d.astype(BF16), pw_ref[g], preferred_element_type=F32) * scale_ref[:, cols]
        y_ref[:, cols] = y.astype(BF16)
        if g + 1 < len(POOL_WINDOWS):
            dst = sa_ref if g % 2 == 0 else sb_ref
            dst[first:n, c0 + gw:d_pool] = sw[:, gw:]
            src = dst
    up = jnp.dot(y_ref[...], wup_ref[...], preferred_element_type=F32)
    out_ref[...] = (jax.nn.sigmoid(gp_ref[...].astype(F32)) * up).astype(out_ref.dtype)


def _pool(proj, pool_w, pool_scale, w_up_pool, *, bsz, seq, ts):
    t = proj.shape[0]
    ns = seq // ts
    d_pool = w_up_pool.shape[0]
    d = w_up_pool.shape[1]
    return pl.pallas_call(
        _pool_kernel,
        grid=(bsz, ns),
        in_specs=[
            pl.BlockSpec((ts, d_pool), lambda b, s: (b * ns + s, COL_POOL // d_pool)),
            pl.BlockSpec((ts, d), lambda b, s: (b * ns + s, COL_GATE_POOL // d)),
            _resident(pool_w.shape),
            _resident((1, d_pool)),
            _resident(w_up_pool.shape),
        ],
        out_specs=pl.BlockSpec((ts, d), lambda b, s: (b * ns + s, 0)),
        out_shape=jax.ShapeDtypeStruct((t, d), BF16),
        scratch_shapes=[pltpu.VMEM((POOL_HALO + ts, d_pool), F32)] * 3 + [pltpu.VMEM((ts, d_pool), BF16)],
        compiler_params=_cparams(("parallel", "arbitrary")),
        name="pool",
    )(proj, proj, pool_w, pool_scale, w_up_pool)


def _dot_nt(a, b):
    return lax.dot_general(a, b, (((1,), (1,)), ((), ())), preferred_element_type=F32)


def _dot_tn(a, b):
    return lax.dot_general(a, b, (((0,), (0,)), ((), ())), preferred_element_type=F32)


GROUP = 256
PAIR = 2 * CHUNK
WY_HEADS = 2


def _wy_kernel(q_ref, k_ref, v_ref, ba_ref, cw_ref, alog_ref, dtb_ref,
               u_ref, w_ref, qg_ref, kd_ref, at_ref, egl_ref, ext, carry_ref, gate_ref, qkv_ref, *, ns):
    ts, hw = q_ref.shape
    hp = pl.program_id(1)
    s = pl.program_id(0) % ns

    @pl.when(s == 0)
    def _():
        ext[0:CONV_HALO, :] = jnp.zeros((CONV_HALO, 3 * hw), F32)

    @pl.when(s > 0)
    def _():
        ext[0:CONV_HALO, :] = carry_ref[hp]

    ext[CONV_HALO:CONV_HALO + ts, 0:hw] = q_ref[...].astype(F32)
    ext[CONV_HALO:CONV_HALO + ts, hw:2 * hw] = k_ref[...].astype(F32)
    ext[CONV_HALO:CONV_HALO + ts, 2 * hw:3 * hw] = v_ref[...].astype(F32)
    carry_ref[hp] = ext[ts:ts + CONV_HALO, :]

    def conv_silu(c0):
        cols = slice(c0, c0 + DN_DK)
        first = CONV_HALO - (CONV_K - 1)
        y = ext[first:first + ts, cols] * cw_ref[0, 0:1, cols]
        for tap in range(1, CONV_K):
            y = y + ext[first + tap:first + tap + ts, cols] * cw_ref[0, tap:tap + 1, cols]
        return y * jax.nn.sigmoid(y)

    @pl.when(hp == 0)
    def _():
        ba = ba_ref[...]
        xs = ba + dtb_ref[...]
        softplus = jnp.maximum(xs, 0.0) + jnp.log1p(jnp.exp(-jnp.abs(xs)))
        g_all = -jnp.exp(alog_ref[...]) * softplus
        ri = lax.broadcasted_iota(jnp.int32, (ts, ts), 0)
        ci = lax.broadcasted_iota(jnp.int32, (ts, ts), 1)
        tri = jnp.where((ri // CHUNK == ci // CHUNK) & (ci <= ri), 1.0, 0.0).astype(F32)
        gate_ref[0] = jax.nn.sigmoid(ba)
        gate_ref[1] = jnp.dot(tri, g_all, preferred_element_type=F32, precision=lax.Precision.HIGHEST)

    beta_all = gate_ref[0]
    gcum = gate_ref[1]
    lane = lax.broadcasted_iota(jnp.int32, (ts, LANES), 1)
    pick = lambda arr, ln: jnp.sum(jnp.where(lane == ln, arr, 0.0), axis=-1, keepdims=True)
    row8 = lax.broadcasted_iota(jnp.int32, (8, LANES), 0)
    lane8 = lax.broadcasted_iota(jnp.int32, (8, LANES), 1)
    pick_rows = jnp.where(lane8 == DN_HEADS + hp * WY_HEADS + row8, 1.0, 0.0).astype(F32)
    gr_all = lax.dot_general(pick_rows, gcum, (((1,), (1,)), ((), ())), preferred_element_type=F32,
                             precision=lax.Precision.HIGHEST)

    betas, gcs = [], []
    for hh in range(WY_HEADS):
        hc = slice(hh * DN_DK, (hh + 1) * DN_DK)
        qh = conv_silu(hh * DN_DK)
        qkv_ref[0, :, hc] = qh * (lax.rsqrt(jnp.sum(qh * qh, axis=-1, keepdims=True) + EPS) * (DN_DK ** -0.5))
        kh = conv_silu(hw + hh * DN_DK)
        qkv_ref[1, :, hc] = kh * lax.rsqrt(jnp.sum(kh * kh, axis=-1, keepdims=True) + EPS)
        qkv_ref[2, :, hc] = conv_silu(2 * hw + hh * DN_DK)
        head = hp * WY_HEADS + hh
        betas.append(jnp.broadcast_to(pick(beta_all, head), (ts, DN_DK)))
        gcs.append(jnp.broadcast_to(pick(gcum, DN_HEADS + head), (ts, DN_DK)))

    ii = lax.broadcasted_iota(jnp.int32, (GROUP, GROUP), 0)
    jj = lax.broadcasted_iota(jnp.int32, (GROUP, GROUP), 1)
    same = (ii // CHUNK) == (jj // CHUNK)
    incl = same & (ii >= jj)
    strict = same & (ii > jj)

    items = [(slice(g * GROUP, (g + 1) * GROUP), hh) for hh in range(WY_HEADS)
             for g in range(ts // GROUP)]
    lows, sols = [], []
    for rows, hh in items:
        hc = slice(hh * DN_DK, (hh + 1) * DN_DK)
        q = qkv_ref[0, rows, hc]
        k = qkv_ref[1, rows, hc]
        beta = betas[hh][rows, :]
        gc = gcs[hh][rows, :]
        gr = gr_all[hh:hh + 1, rows]
        decay = jnp.where(incl, jnp.exp(jnp.concatenate([gc, gc], axis=1) - gr), 0.0)
        kb = k * beta
        a = _dot_nt(jnp.concatenate([kb, q], axis=0).astype(BF16), k.astype(BF16))
        lows.append(jnp.where(strict, a[:GROUP] * decay, 0.0).astype(BF16))
        attn = a[GROUP:] * decay
        for j in range(GROUP // PAIR):
            pr = slice(j * PAIR, (j + 1) * PAIR)
            at_ref[rows.start + j * PAIR:rows.start + (j + 1) * PAIR, hc] = attn[pr, pr].astype(BF16)
        egc = jnp.exp(gc)
        qg_ref[rows, hc] = (q * egc).astype(BF16)
        lasts = [gc[(c + 1) * CHUNK - 1:(c + 1) * CHUNK, :] for c in range(GROUP // CHUNK)]
        for c, g_last in enumerate(lasts):
            row = rows.start // CHUNK + c
            egl_ref[row:row + 1, hc] = jnp.exp(g_last)
        g_last_rows = jnp.concatenate([jnp.broadcast_to(gl, (CHUNK, DN_DK)) for gl in lasts], axis=0)
        kd_ref[rows, hc] = (k * jnp.exp(g_last_rows - gc)).astype(BF16)
        sols.append(jnp.concatenate([qkv_ref[2, rows, hc] * beta, kb * egc], axis=1))

    mm = lambda a16, b: jnp.dot(a16, b.astype(BF16), preferred_element_type=F32)
    sols = [s - mm(l, s) for l, s in zip(lows, sols)]
    pws = lows
    for _ in range(CHUNK.bit_length() - 2):
        pws = [mm(p, p).astype(BF16) for p in pws]
        sols = [s + mm(p, s) for p, s in zip(pws, sols)]
    for (rows, hh), s in zip(items, sols):
        hc = slice(hh * DN_DK, (hh + 1) * DN_DK)
        u_ref[rows, hc] = s[:, :DN_DK]
        w_ref[rows, hc] = s[:, DN_DK:].astype(BF16)


def _wy(proj, ba, conv_w, alog_pad, dtb_pad, *, seq, ts):
    t = proj.shape[0]
    hw = WY_HEADS * DN_DK
    n_hp = DN_HEADS // WY_HEADS
    dq = DN_HEADS * DN_DK
    cw = conv_w.reshape(CONV_K, 3, n_hp, hw).transpose(2, 0, 1, 3).reshape(n_hp, CONV_K, 3 * hw)
    src = lambda col: pl.BlockSpec((ts, hw), lambda r, h: (r, col // hw + h))
    hblk = pl.BlockSpec((ts, hw), lambda r, h: (r, h))
    wide = jax.ShapeDtypeStruct((t, dq), BF16)
    return pl.pallas_call(
        functools.partial(_wy_kernel, ns=seq // ts),
        grid=(t // ts, n_hp),
        in_specs=[src(COL_Q), src(COL_K), src(COL_V),
                  pl.BlockSpec((ts, LANES), lambda r, h: (r, 0)),
                  pl.BlockSpec((1, CONV_K, 3 * hw), lambda r, h: (h, 0, 0)),
                  _resident((1, LANES)), _resident((1, LANES))],
        out_specs=[hblk, hblk, hblk, hblk, hblk,
                   pl.BlockSpec((ts // CHUNK, hw), lambda r, h: (r, h))],
        out_shape=[jax.ShapeDtypeStruct((t, dq), F32), wide, wide, wide, wide,
                   jax.ShapeDtypeStruct((t // CHUNK, dq), F32)],
        scratch_shapes=[pltpu.VMEM((CONV_HALO + ts, 3 * hw), F32),
                        pltpu.VMEM((n_hp, CONV_HALO, 3 * hw), F32),
                        pltpu.VMEM((2, ts, LANES), F32),
                        pltpu.VMEM((3, ts, hw), F32)],
        compiler_params=_cparams(("arbitrary", "arbitrary")),
        name="wy",
    )(proj, proj, proj, ba, cw, alog_pad, dtb_pad)


def _scan_kernel(u_ref, w_ref, qg_ref, kd_ref, at_ref, egl_ref, z_ref, dnw_ref, out_ref, state_ref):
    ts = u_ref.shape[0]

    @pl.when(pl.program_id(1) == 0)
    def _():
        state_ref[...] = jnp.zeros(state_ref.shape, F32)

    heads = range(DN_HEADS)
    hcs = [slice(h * DN_DK, (h + 1) * DN_DK) for h in heads]
    states = [state_ref[h] for h in heads]
    even_vn = [None] * DN_HEADS
    for c in range(ts // CHUNK):
        rows = slice(c * CHUNK, (c + 1) * CHUNK)
        rs = [jnp.dot(jnp.concatenate([w_ref[rows, hcs[h]], qg_ref[rows, hcs[h]]], axis=0),
                      states[h].astype(BF16), preferred_element_type=F32) for h in heads]
        vns = [(u_ref[rows, hcs[h]] - rs[h][:CHUNK]).astype(BF16) for h in heads]
        states = [states[h] * egl_ref[c:c + 1, hcs[h]] + _dot_tn(kd_ref[rows, hcs[h]], vns[h]) for h in heads]
        for h in heads:
            if c % 2 == 0:
                even_vn[h] = vns[h]
                pair = jnp.concatenate([vns[h], jnp.zeros_like(vns[h])], axis=0)
            else:
                pair = jnp.concatenate([even_vn[h], vns[h]], axis=0)
            o = rs[h][CHUNK:] + jnp.dot(at_ref[rows, hcs[h]], pair, preferred_element_type=F32)
            on = o * lax.rsqrt(jnp.mean(o * o, axis=-1, keepdims=True) + EPS) * dnw_ref[...]
            z = z_ref[rows, hcs[h]].astype(F32)
            out_ref[rows, hcs[h]] = (on * (z * jax.nn.sigmoid(z))).astype(out_ref.dtype)
    for h in heads:
        state_ref[h] = states[h]


def _scan(u, w, qg, kd, at, egl, proj, dn_norm, *, bsz, seq, ts):
    t, dv = u.shape
    ns = seq // ts
    blk = pl.BlockSpec((ts, dv), lambda b, s: (b * ns + s, 0))
    return pl.pallas_call(
        _scan_kernel,
        grid=(bsz, ns),
        in_specs=[blk, blk, blk, blk, blk,
                  pl.BlockSpec((ts // CHUNK, dv), lambda b, s: (b * ns + s, 0)),
                  pl.BlockSpec((ts, dv), lambda b, s: (b * ns + s, COL_Z // dv)),
                  _resident((1, DN_DK))],
        out_specs=blk,
        out_shape=jax.ShapeDtypeStruct((t, dv), BF16),
        scratch_shapes=[pltpu.VMEM((DN_HEADS, DN_DK, DN_DK), F32)],
        compiler_params=_cparams(("parallel", "arbitrary")),
        name="scan",
    )(u, w, qg, kd, at, egl, proj, dn_norm)


ROUTE_E1, ROUTE_E2, ROUTE_W1, ROUTE_W2, ROUTE_R1, ROUTE_R2 = range(6)
ROUTER_LANE0 = N_GROUPS
NEG_BIG = -1e30


def _pack_rows(x):
    n = x.shape[1] // 2
    bits = lambda v: lax.bitcast_convert_type(v.astype(BF16).astype(F32), jnp.uint32)
    return bits(x[:, n:]) | (bits(x[:, :n]) >> 16)


def _unpack_rows(p):
    lo = lax.bitcast_convert_type(p << 16, F32).astype(BF16)
    hi = lax.bitcast_convert_type(p & jnp.uint32(0xFFFF0000), F32).astype(BF16)
    return jnp.concatenate([lo, hi], axis=1)


def _merge_kernel(ydn_ref, gd_ref, mp_ref, x_ref, wupdn_ref, wout_ref, gain_ref, wr_ref, br_ref,
                  h1_ref, n2_ref, route_ref, cnt_ref, carry_ref):
    tm = x_ref.shape[0]

    @pl.when(pl.program_id(0) == 0)
    def _():
        carry_ref[...] = jnp.zeros(carry_ref.shape, F32)

    md = (jax.nn.sigmoid(gd_ref[...].astype(F32))
          * jnp.dot(ydn_ref[...], wupdn_ref[...], preferred_element_type=F32))
    merged = mp_ref[...].astype(F32) + md
    h1 = x_ref[...] + jnp.dot(merged.astype(BF16), wout_ref[...], preferred_element_type=F32)
    h1_ref[...] = h1
    n2 = h1 * lax.rsqrt(jnp.mean(h1 * h1, axis=-1, keepdims=True) + EPS) * gain_ref[...]
    n2_ref[...] = _pack_rows(n2)

    hi = n2.astype(BF16)
    lo = (n2 - hi.astype(F32)).astype(BF16)
    r_hi = jnp.dot(hi, wr_ref[...], preferred_element_type=F32)
    r_lo = jnp.dot(lo, wr_ref[:, :LANES], preferred_element_type=F32)
    logits = r_hi[:, :LANES] + r_hi[:, LANES:] + r_lo + br_ref[...]

    lane = lax.broadcasted_iota(jnp.int32, (tm, LANES), 1)
    lane_f = lane.astype(F32)
    first_max = lambda v, m: jnp.min(jnp.where(v == m, lane_f, float(LANES)), axis=-1, keepdims=True)

    is_grp = lane < N_GROUPS
    lg = jnp.where(is_grp, logits, NEG_BIG)
    gmax = jnp.max(lg, axis=-1, keepdims=True)
    gi = first_max(lg, gmax)
    p_sel = 1.0 / jnp.sum(jnp.where(is_grp, jnp.exp(lg - gmax), 0.0), axis=-1, keepdims=True)

    e_lane = lane - ROUTER_LANE0
    in_grp = (e_lane >= 0) & (e_lane < N_EXPERTS) & ((e_lane // EXPERTS_PER_GROUP).astype(F32) == gi)
    le = jnp.where(in_grp, logits, NEG_BIG)
    v1 = jnp.max(le, axis=-1, keepdims=True)
    i1 = first_max(le, v1)
    le2 = jnp.where(lane_f == i1, NEG_BIG, le)
    v2 = jnp.max(le2, axis=-1, keepdims=True)
    i2 = first_max(le2, v2)
    s = jnp.exp(v2 - v1)
    w1 = p_sel / (1.0 + s)
    w2 = p_sel * s / (1.0 + s)
    e1 = i1 - float(ROUTER_LANE0)
    e2 = i2 - float(ROUTER_LANE0)

    oh1 = jnp.where(lane_f == e1, 1.0, 0.0)
    oh2 = jnp.where(lane_f == e2, 1.0, 0.0)
    both = oh1 + oh2
    ri = lax.broadcasted_iota(jnp.int32, (tm, tm), 0)
    ci = lax.broadcasted_iota(jnp.int32, (tm, tm), 1)
    tri = jnp.where(ri > ci, 1.0, 0.0).astype(BF16)
    prior = jnp.dot(tri, both.astype(BF16), preferred_element_type=F32) + carry_ref[...]
    r1 = jnp.sum(prior * oh1, axis=-1, keepdims=True)
    r2 = jnp.sum(prior * oh2, axis=-1, keepdims=True)
    carry = carry_ref[...] + jnp.sum(both, axis=0, keepdims=True)
    carry_ref[...] = carry
    cnt_ref[...] = carry

    route = jnp.zeros((tm, LANES), F32)
    for ln, val in ((ROUTE_E1, e1), (ROUTE_E2, e2), (ROUTE_W1, w1), (ROUTE_W2, w2),
                    (ROUTE_R1, r1), (ROUTE_R2, r2)):
        route = jnp.where(lane == ln, val, route)
    route_ref[...] = route


def _merge(y_dn, proj, mp, x2d, w_up_dn, w_out, gain, w_router, b_router, *, tm):
    t, d = x2d.shape
    dv = y_dn.shape[1]
    rows = lambda blk: pl.BlockSpec(blk, lambda i: (i, 0))
    return pl.pallas_call(
        _merge_kernel,
        grid=(t // tm,),
        in_specs=[
            rows((tm, dv)),
            pl.BlockSpec((tm, d), lambda i: (i, COL_GATE_DN // d)),
            rows((tm, d)), rows((tm, d)),
            _resident(w_up_dn.shape), _resident(w_out.shape), _resident((1, d)),
            _resident(w_router.shape), _resident((1, LANES)),
        ],
        out_specs=[rows((tm, d)), rows((tm, d // 2)), rows((tm, LANES)),
                   pl.BlockSpec((1, LANES), lambda i: (0, 0))],
        out_shape=[jax.ShapeDtypeStruct((t, d), F32), jax.ShapeDtypeStruct((t, d // 2), jnp.uint32),
                   jax.ShapeDtypeStruct((t, LANES), F32), jax.ShapeDtypeStruct((1, LANES), F32)],
        scratch_shapes=[pltpu.VMEM((1, LANES), F32)],
        compiler_params=_cparams(("arbitrary",)),
        name="merge",
    )(y_dn, proj, mp, x2d, w_up_dn, w_out, gain, w_router, b_router)


N_GATHER_BUF = 3


def _dispatch_kernel(pos_ref, zrow_ref, nu_ref, n2_ref, xs_ref, zbuf, sem, zsem, *, n_blocks):
    tb = n2_ref.shape[0]
    bm = zbuf.shape[0]
    i = pl.program_id(0)

    @pl.when(i == 0)
    def _():
        zbuf[...] = jnp.zeros(zbuf.shape, zbuf.dtype)
        zero_block = lambda row: pltpu.make_async_copy(
            zbuf, xs_ref.at[pl.ds(pl.multiple_of(row, bm), bm)], zsem)
        for start in (True, False):
            for e in range(N_EXPERTS):
                @pl.when(zrow_ref[e] >= 0)
                def _(e=e, start=start):
                    cp = zero_block(jnp.maximum(zrow_ref[e], 0))
                    cp.start() if start else cp.wait()

            def tail(b, _, start=start):
                cp = zero_block(b * bm)
                cp.start() if start else cp.wait()
                return 0
            lax.fori_loop(nu_ref[0], n_blocks, tail, 0)

    for j in range(tb):
        for k in range(2):
            dst = pos_ref[2 * (i * tb + j) + k]
            pltpu.make_async_copy(n2_ref.at[pl.ds(j, 1)], xs_ref.at[pl.ds(dst, 1)],
                                  sem).start(priority=k)
    for _ in range(2):
        pltpu.make_async_copy(n2_ref, xs_ref.at[pl.ds(0, tb)], sem).wait()


def _dispatch(pos, zrow, n_used, n2, *, cap, bm, tb):
    t, dp = n2.shape
    return pl.pallas_call(
        functools.partial(_dispatch_kernel, n_blocks=cap // bm),
        grid_spec=pltpu.PrefetchScalarGridSpec(
            num_scalar_prefetch=3,
            grid=(t // tb,),
            in_specs=[pl.BlockSpec((tb, dp), lambda i, pos, zrow, nu: (i, 0))],
            out_specs=pl.BlockSpec(memory_space=pl.ANY),
            scratch_shapes=[pltpu.VMEM((bm, dp), n2.dtype), pltpu.SemaphoreType.DMA,
                            pltpu.SemaphoreType.DMA],
        ),
        out_shape=jax.ShapeDtypeStruct((cap, dp), n2.dtype),
        compiler_params=_cparams(("arbitrary",)),
        name="dispatch",
    )(pos, zrow, n_used, n2)


def _experts_kernel(be_ref, nu_ref, x_ref, wg_ref, wu_ref, wd_ref, y_ref, wg16, wu16, wd16):
    i = pl.program_id(0)
    nu = nu_ref[0]

    @pl.when((i < nu) & ((i == 0) | (be_ref[i] != be_ref[jnp.maximum(i - 1, 0)])))
    def _():
        wg16[...] = wg_ref[0].astype(BF16)
        wu16[...] = wu_ref[0].astype(BF16)
        wd16[...] = wd_ref[0].astype(BF16)

    @pl.when(i < nu)
    def _():
        xb = _unpack_rows(x_ref[...])
        g = jnp.dot(xb, wg16[...], preferred_element_type=F32)
        u = jnp.dot(xb, wu16[...], preferred_element_type=F32)
        hmid = (g * jax.nn.sigmoid(g)) * u
        y_ref[...] = _pack_rows(jnp.dot(hmid.astype(BF16), wd16[...], preferred_element_type=F32))

    @pl.when(i >= nu)
    def _():
        y_ref[...] = jnp.zeros(y_ref.shape, y_ref.dtype)


def _experts(blk_e, n_used, xs, w_gate, w_up, w_down, *, bm):
    cap, dp = xs.shape
    d, de = w_gate.shape[1], w_gate.shape[2]
    wspec = lambda shape: pl.BlockSpec(shape, lambda i, be, nu: (be[i], 0, 0))
    return pl.pallas_call(
        _experts_kernel,
        grid_spec=pltpu.PrefetchScalarGridSpec(
            num_scalar_prefetch=2,
            grid=(cap // bm,),
            in_specs=[pl.BlockSpec((bm, dp), lambda i, be, nu: (jnp.minimum(i, nu[0] - 1), 0)),
                      wspec((1, d, de)), wspec((1, d, de)), wspec((1, de, d))],
            out_specs=pl.BlockSpec((bm, dp), lambda i, be, nu: (i, 0)),
            scratch_shapes=[pltpu.VMEM((d, de), BF16), pltpu.VMEM((d, de), BF16), pltpu.VMEM((de, d), BF16)],
        ),
        out_shape=jax.ShapeDtypeStruct((cap, dp), jnp.uint32),
        compiler_params=_cparams(("arbitrary",)),
        name="experts",
    )(blk_e, n_used, xs, w_gate, w_up, w_down)


def _combine_kernel(pos_ref, h1_ref, route_ref, p_ref, ys_ref, wpg_ref, wpp_ref, gple_ref, gfin_ref,
                    out_ref, ybuf0, ybuf1, ybuf2, sem, *, final_norm, n_steps):
    bufs = (ybuf0, ybuf1, ybuf2)
    tm = h1_ref.shape[0]
    i = pl.program_id(0)

    def issue(blk, slot):
        for j in range(tm):
            for k in range(2):
                src = pos_ref[2 * (blk * tm + j) + k]
                pltpu.make_async_copy(ys_ref.at[pl.ds(src, 1)], bufs[slot].at[pl.ds(k * tm + j, 1)],
                                      sem.at[slot]).start(priority=k)

    def drain(slot):
        pltpu.make_async_copy(ys_ref.at[pl.ds(0, 2 * tm)], bufs[slot], sem.at[slot]).wait()

    @pl.when(i == 0)
    def _():
        issue(0, 0)
        issue(1, 1)

    for r in range(N_GATHER_BUF):
        @pl.when(i % N_GATHER_BUF == r)
        def _(r=r):
            drain(r)
            issue(i + 2, (r + 2) % N_GATHER_BUF)
            cur = bufs[r]
            route = route_ref[...]
            y_moe = (_unpack_rows(cur[0:tm, :]).astype(F32) * route[:, ROUTE_W1:ROUTE_W1 + 1]
                     + _unpack_rows(cur[tm:2 * tm, :]).astype(F32) * route[:, ROUTE_W2:ROUTE_W2 + 1])
            h2 = h1_ref[...] + y_moe
            n3 = h2 * lax.rsqrt(jnp.mean(h2 * h2, axis=-1, keepdims=True) + EPS) * gple_ref[...]
            gate = jax.nn.sigmoid(jnp.dot(n3.astype(BF16), wpg_ref[...], preferred_element_type=F32))
            h3 = h2 + gate * jnp.dot(p_ref[...].astype(BF16), wpp_ref[...], preferred_element_type=F32)
            if final_norm:
                h3 = h3 * lax.rsqrt(jnp.mean(h3 * h3, axis=-1, keepdims=True) + EPS) * gfin_ref[...]
            out_ref[...] = h3

    last = n_steps - 1

    @pl.when(i == last)
    def _():
        drain((last + 1) % N_GATHER_BUF)
        drain((last + 2) % N_GATHER_BUF)


def _combine(pos, h1, route, p2d, ys, w_ple_gate, w_ple_proj, g_ple, g_fin, *, tm, final_norm):
    t, d = h1.shape
    rows = lambda w: pl.BlockSpec((tm, w), lambda i, pos: (i, 0))
    res = lambda shape: pl.BlockSpec(shape, lambda i, pos: (0,) * len(shape), pipeline_mode=pl.Buffered(1))
    return pl.pallas_call(
        functools.partial(_combine_kernel, final_norm=final_norm, n_steps=t // tm),
        grid_spec=pltpu.PrefetchScalarGridSpec(
            num_scalar_prefetch=1,
            grid=(t // tm,),
            in_specs=[rows(d), rows(LANES), rows(p2d.shape[1]), pl.BlockSpec(memory_space=pl.ANY),
                      res(w_ple_gate.shape), res(w_ple_proj.shape), res((1, d)), res((1, d))],
            out_specs=rows(d),
            scratch_shapes=[pltpu.VMEM((2 * tm, ys.shape[1]), ys.dtype)] * N_GATHER_BUF
                           + [pltpu.SemaphoreType.DMA((N_GATHER_BUF,))],
        ),
        out_shape=jax.ShapeDtypeStruct((t, d), F32),
        compiler_params=_cparams(("arbitrary",)),
        name="combine",
    )(jnp.pad(pos, (0, 2 * 2 * tm)), h1, route, p2d, ys, w_ple_gate, w_ple_proj, g_ple, g_fin)


def _pick(n, pref):
    b = min(n, pref)
    while n % b:
        b -= 8
    return b


def _stage_mixers(x, prm, i):
    bsz, seq, d = x.shape
    t = bsz * seq
    x2d = x.reshape(t, d)
    w_in = prm["w_in"][i]
    d_pool = prm["w_up_pool"].shape[1]
    dq = DN_HEADS * DN_DK
    o_qkv = d_pool
    o_z = o_qkv + 3 * dq
    o_b = o_z + dq
    o_gp = o_b + 2 * DN_HEADS
    o_gd = o_gp + d
    w_main = jnp.concatenate(
        [w_in[:, o_gp:o_gp + d], w_in[:, o_gd:o_gd + d], w_in[:, :d_pool],
         w_in[:, o_qkv:o_qkv + 3 * dq], w_in[:, o_z:o_z + dq]], axis=1).astype(BF16)
    assert w_main.shape[1] == N_MAIN
    w_ba = jnp.pad(w_in[:, o_b:o_b + 2 * DN_HEADS], ((0, 0), (0, LANES - 2 * DN_HEADS))).astype(BF16)
    proj, ba = _inproj(x2d, prm["norm_mix"][i][None, :], w_main, w_ba,
                       tm=_pick(t, 1024), tn=2304)

    ts = _pick(seq, 512)
    mp = _pool(proj, prm["pool_w"][i].astype(BF16), prm["pool_scale"][i][None, :],
               prm["w_up_pool"][i].astype(BF16), bsz=bsz, seq=seq, ts=ts)

    lane_pad = lambda a: jnp.pad(a, (DN_HEADS, LANES - 2 * DN_HEADS))[None, :]
    u, w, qg, kd, at, egl = _wy(proj, ba, prm["conv_w"][i], lane_pad(prm["a_log"][i]),
                                lane_pad(prm["dt_bias"][i]), seq=seq, ts=ts)
    y_dn = _scan(u, w, qg, kd, at, egl, proj, prm["dn_norm"][i][None, :], bsz=bsz, seq=seq, ts=ts)
    return dict(proj=proj, ba=ba, mp=mp, y_dn=y_dn)


MOE_BM = 512


def _router_weights(w_rg, b_rg, w_re, b_re):
    d = w_rg.shape[0]
    w = jnp.concatenate([w_rg, w_re, jnp.zeros((d, LANES - N_GROUPS - N_EXPERTS), F32)], axis=1)
    w_hi = w.astype(BF16)
    w_lo = (w - w_hi.astype(F32)).astype(BF16)
    b = jnp.concatenate([b_rg, b_re, jnp.zeros((LANES - N_GROUPS - N_EXPERTS,), F32)])[None, :]
    return jnp.concatenate([w_hi, w_lo], axis=1), b


def _stage_moe_plan(route, cnt, t):
    counts = cnt[0, :N_EXPERTS].astype(jnp.int32)
    padded = ((counts + MOE_BM - 1) // MOE_BM) * MOE_BM
    pends = jnp.cumsum(padded)
    pstarts = pends - padded
    eid = route[:, ROUTE_E1:ROUTE_E2 + 1].astype(jnp.int32)
    rank = route[:, ROUTE_R1:ROUTE_R2 + 1].astype(jnp.int32)
    pos = (pstarts[eid] + rank).reshape(2 * t)
    cap = 2 * t + N_EXPERTS * MOE_BM
    n_blocks = cap // MOE_BM
    n_used = (pends[-1] // MOE_BM).astype(jnp.int32)
    blk = jnp.minimum(jnp.arange(n_blocks, dtype=jnp.int32), n_used - 1)
    blk_e = jnp.sum((pends[None, :] <= (blk * MOE_BM)[:, None]).astype(jnp.int32), axis=1)
    blk_e = jnp.minimum(blk_e, N_EXPERTS - 1)
    zrow = jnp.where(padded > 0, pends - MOE_BM, -1).astype(jnp.int32)
    return pos, blk_e, n_used.reshape(1), zrow, cap


def kernel(x, p, norm_mix, w_in, pool_w, pool_scale, conv_w, a_log, dt_bias, dn_norm, w_up_pool, w_up_dn, w_out, norm_moe, w_router_group, b_router_group, w_router_expert, b_router_expert, w_gate, w_up, w_down, norm_ple, w_ple_gate, w_ple_proj, norm_final):
    bsz, seq, d = x.shape
    t = bsz * seq
    depth = norm_mix.shape[0]
    prm = dict(norm_mix=norm_mix, w_in=w_in, pool_w=pool_w, pool_scale=pool_scale, conv_w=conv_w,
               a_log=a_log, dt_bias=dt_bias, dn_norm=dn_norm, w_up_pool=w_up_pool)
    h = x
    for i in range(depth):
        st = _stage_mixers(h, prm, i)
        w_router, b_router = _router_weights(w_router_group[i], b_router_group[i],
                                             w_router_expert[i], b_router_expert[i])
        tm = _pick(t, 256)
        h1, n2, route, cnt = _merge(st["y_dn"], st["proj"], st["mp"], h.reshape(t, d),
                                    w_up_dn[i].astype(BF16), w_out[i].astype(BF16), norm_moe[i][None, :],
                                    w_router, b_router, tm=_pick(t, 512))
        pos, blk_e, n_used, zrow, cap = _stage_moe_plan(route, cnt, t)
        xs = _dispatch(pos, zrow, n_used, n2, cap=cap, bm=MOE_BM, tb=tm)
        ys = _experts(blk_e, n_used, xs, w_gate[i], w_up[i], w_down[i], bm=MOE_BM)
        out = _combine(pos, h1, route, p[i].reshape(t, -1), ys, w_ple_gate[i].astype(BF16),
                       w_ple_proj[i].astype(BF16), norm_ple[i][None, :], norm_final[None, :],
                       tm=tm, final_norm=(i == depth - 1))
        h = out.reshape(bsz, seq, d)
    return h
```

```python
import functools
from typing import NamedTuple

import jax
import jax.numpy as jnp
from jax import lax
from jax.experimental import pallas as pl
from jax.experimental.pallas import tpu as pltpu

F32 = jnp.float32
BF16 = jnp.bfloat16

EPS = 1e-6
POOL_WINDOWS = (2, 4, 8, 16)
POOL_HALO = 32
assert all(w == 2 ** (g + 1) for g, w in enumerate(POOL_WINDOWS)) and POOL_HALO == 8 * len(POOL_WINDOWS)
CONV_K = 4
CONV_HALO = 8
DN_HEADS = 8
DN_DK = 128
CHUNK = 64
N_GROUPS = 4
EXPERTS_PER_GROUP = 8
N_EXPERTS = N_GROUPS * EXPERTS_PER_GROUP
LANES = 128
VMEM_LIMIT = 56 * 1024 * 1024

COL_GATE_POOL = 0
COL_GATE_DN = 2048
COL_POOL = 4096
COL_Q = 5120
COL_K = 6144
COL_V = 7168
COL_Z = 8192
N_MAIN = 9216


def _cparams(sem):
    return pltpu.CompilerParams(dimension_semantics=sem, vmem_limit_bytes=VMEM_LIMIT)


def _resident(shape):
    nd = len(shape)
    return pl.BlockSpec(shape, lambda *_: (0,) * nd, pipeline_mode=pl.Buffered(1))


def _inproj_kernel(x_ref, gain_ref, w_ref, wba_ref, out_ref, ba_ref, n1_ref, *, sub):
    tm = x_ref.shape[0]

    @pl.when(pl.program_id(1) == 0)
    def _():
        def body(r, _):
            rows = pl.ds(pl.multiple_of(r * sub, sub), sub)
            x = x_ref[rows, :]
            y = x * lax.rsqrt(jnp.mean(x * x, axis=-1, keepdims=True) + EPS) * gain_ref[...]
            n1_ref[rows, :] = y.astype(BF16)
            return 0
        lax.fori_loop(0, tm // sub, body, 0)
        ba_ref[...] = jnp.dot(n1_ref[...], wba_ref[...], preferred_element_type=F32)

    out_ref[...] = jnp.dot(n1_ref[...], w_ref[...], preferred_element_type=F32).astype(out_ref.dtype)


def _inproj(x2d, gain, w_main, w_ba, *, tm, tn):
    t, d = x2d.shape
    n = w_main.shape[1]
    return pl.pallas_call(
        functools.partial(_inproj_kernel, sub=min(tm, 256)),
        grid=(t // tm, n // tn),
        in_specs=[
            pl.BlockSpec((tm, d), lambda i, j: (i, 0)),
            _resident((1, d)),
            pl.BlockSpec((d, tn), lambda i, j: (0, j)),
            _resident((d, LANES)),
        ],
        out_specs=[
            pl.BlockSpec((tm, tn), lambda i, j: (i, j)),
            pl.BlockSpec((tm, LANES), lambda i, j: (i, 0)),
        ],
        out_shape=[jax.ShapeDtypeStruct((t, n), BF16), jax.ShapeDtypeStruct((t, LANES), F32)],
        scratch_shapes=[pltpu.VMEM((tm, d), BF16)],
        compiler_params=_cparams(("parallel", "arbitrary")),
        name="inproj",
    )(x2d, gain, w_main, w_ba)


def _pool_kernel(u_ref, gp_ref, pw_ref, scale_ref, wup_ref, out_ref, ext_ref, sa_ref, sb_ref, y_ref):
    ts = u_ref.shape[0]
    gw = pw_ref.shape[1]
    d_pool = ext_ref.shape[1]
    n = POOL_HALO + ts
    s = pl.program_id(1)

    @pl.when(s == 0)
    def _():
        ext_ref[0:POOL_HALO, :] = jnp.zeros((POOL_HALO, d_pool), F32)

    @pl.when(s > 0)
    def _():
        ext_ref[0:POOL_HALO, :] = ext_ref[ts:n, :]

    ext_ref[POOL_HALO:n, :] = u_ref[...].astype(F32)
    t1 = (s * ts + 1 + lax.broadcasted_iota(jnp.int32, (ts, 1), 0)).astype(F32)

    src = ext_ref
    for g, w in enumerate(POOL_WINDOWS):
        half = w // 2
        first = 8 * (g + 1)
        c0 = g * gw
        sw = src[first:n, c0:d_pool] + src[first - half:n - half, c0:d_pool]
        cols = slice(c0, c0 + gw)
        cur = ext_ref[POOL_HALO:n, cols]
        d = sw[POOL_HALO - first:, 0:gw] / jnp.minimum(t1, float(w)) - cur
        y = jnp.dot(d.astype(BF16), pw_ref[g], preferred_element_type=F32) * scale_ref[:, cols]
        y_ref[:, cols] = y.astype(BF16)
        if g + 1 < len(POOL_WINDOWS):
            dst = sa_ref if g % 2 == 0 else sb_ref
            dst[first:n, c0 + gw:d_pool] = sw[:, gw:]
            src = dst
    up = jnp.dot(y_ref[...], wup_ref[...], preferred_element_type=F32)
    out_ref[...] = (jax.nn.sigmoid(gp_ref[...].astype(F32)) * up).astype(out_ref.dtype)


def _pool(proj, pool_w, pool_scale, w_up_pool, *, bsz, seq, ts):
    t = proj.shape[0]
    ns = seq // ts
    d_pool = w_up_pool.shape[0]
    d = w_up_pool.shape[1]
    return pl.pallas_call(
        _pool_kernel,
        grid=(bsz, ns),
        in_specs=[
            pl.BlockSpec((ts, d_pool), lambda b, s: (b * ns + s, COL_POOL // d_pool)),
            pl.BlockSpec((ts, d), lambda b, s: (b * ns + s, COL_GATE_POOL // d)),
            _resident(pool_w.shape),
            _resident((1, d_pool)),
            _resident(w_up_pool.shape),
        ],
        out_specs=pl.BlockSpec((ts, d), lambda b, s: (b * ns + s, 0)),
        out_shape=jax.ShapeDtypeStruct((t, d), BF16),
        scratch_shapes=[pltpu.VMEM((POOL_HALO + ts, d_pool), F32)] * 3 + [pltpu.VMEM((ts, d_pool), BF16)],
        compiler_params=_cparams(("parallel", "arbitrary")),
        name="pool",
    )(proj, proj, pool_w, pool_scale, w_up_pool)


def _dot_nt(a, b):
    return lax.dot_general(a, b, (((1,), (1,)), ((), ())), preferred_element_type=F32)


def _dot_tn(a, b):
    return lax.dot_general(a, b, (((0,), (0,)), ((), ())), preferred_element_type=F32)


GROUP = 256
PAIR = 2 * CHUNK
WY_HEADS = 4


def _wy_kernel(q_ref, k_ref, v_ref, ba_ref, cw_ref, alog_ref, dtb_ref,
               u_ref, w_ref, qg_ref, kd_ref, at_ref, egl_ref, ext, carry_ref, gate_ref, qkv_ref, *, ns):
    ts, hw = q_ref.shape
    hp = pl.program_id(1)
    s = pl.program_id(0) % ns

    @pl.when(s == 0)
    def _():
        ext[0:CONV_HALO, :] = jnp.zeros((CONV_HALO, 3 * hw), F32)

    @pl.when(s > 0)
    def _():
        ext[0:CONV_HALO, :] = carry_ref[hp]

    ext[CONV_HALO:CONV_HALO + ts, 0:hw] = q_ref[...].astype(F32)
    ext[CONV_HALO:CONV_HALO + ts, hw:2 * hw] = k_ref[...].astype(F32)
    ext[CONV_HALO:CONV_HALO + ts, 2 * hw:3 * hw] = v_ref[...].astype(F32)
    carry_ref[hp] = ext[ts:ts + CONV_HALO, :]

    def conv_silu(c0):
        cols = slice(c0, c0 + DN_DK)
        first = CONV_HALO - (CONV_K - 1)
        y = ext[first:first + ts, cols] * cw_ref[0, 0:1, cols]
        for tap in range(1, CONV_K):
            y = y + ext[first + tap:first + tap + ts, cols] * cw_ref[0, tap:tap + 1, cols]
        return y * jax.nn.sigmoid(y)

    @pl.when(hp == 0)
    def _():
        ba = ba_ref[...]
        xs = ba + dtb_ref[...]
        softplus = jnp.maximum(xs, 0.0) + jnp.log1p(jnp.exp(-jnp.abs(xs)))
        g_all = -jnp.exp(alog_ref[...]) * softplus
        ri = lax.broadcasted_iota(jnp.int32, (ts, ts), 0)
        ci = lax.broadcasted_iota(jnp.int32, (ts, ts), 1)
        tri = jnp.where((ri // CHUNK == ci // CHUNK) & (ci <= ri), 1.0, 0.0).astype(F32)
        gate_ref[0] = jax.nn.sigmoid(ba)
        gate_ref[1] = jnp.dot(tri, g_all, preferred_element_type=F32, precision=lax.Precision.HIGHEST)

    beta_all = gate_ref[0]
    gcum = gate_ref[1]
    lane = lax.broadcasted_iota(jnp.int32, (ts, LANES), 1)
    pick = lambda arr, ln: jnp.sum(jnp.where(lane == ln, arr, 0.0), axis=-1, keepdims=True)
    row8 = lax.broadcasted_iota(jnp.int32, (8, LANES), 0)
    lane8 = lax.broadcasted_iota(jnp.int32, (8, LANES), 1)
    pick_rows = jnp.where(lane8 == DN_HEADS + hp * WY_HEADS + row8, 1.0, 0.0).astype(F32)
    gr_all = lax.dot_general(pick_rows, gcum, (((1,), (1,)), ((), ())), preferred_element_type=F32,
                             precision=lax.Precision.HIGHEST)

    betas, gcs = [], []
    for hh in range(WY_HEADS):
        hc = slice(hh * DN_DK, (hh + 1) * DN_DK)
        qh = conv_silu(hh * DN_DK)
        qkv_ref[0, :, hc] = qh * (lax.rsqrt(jnp.sum(qh * qh, axis=-1, keepdims=True) + EPS) * (DN_DK ** -0.5))
        kh = conv_silu(hw + hh * DN_DK)
        qkv_ref[1, :, hc] = kh * lax.rsqrt(jnp.sum(kh * kh, axis=-1, keepdims=True) + EPS)
        qkv_ref[2, :, hc] = conv_silu(2 * hw + hh * DN_DK)
        head = hp * WY_HEADS + hh
        betas.append(jnp.broadcast_to(pick(beta_all, head), (ts, DN_DK)))
        gcs.append(jnp.broadcast_to(pick(gcum, DN_HEADS + head), (ts, DN_DK)))

    ii = lax.broadcasted_iota(jnp.int32, (GROUP, GROUP), 0)
    jj = lax.broadcasted_iota(jnp.int32, (GROUP, GROUP), 1)
    same = (ii // CHUNK) == (jj // CHUNK)
    incl = same & (ii >= jj)
    strict = same & (ii > jj)

    items = [(slice(g * GROUP, (g + 1) * GROUP), hh) for hh in range(WY_HEADS)
             for g in range(ts // GROUP)]
    lows, sols = [], []
    for rows, hh in items:
        hc = slice(hh * DN_DK, (hh + 1) * DN_DK)
        q = qkv_ref[0, rows, hc]
        k = qkv_ref[1, rows, hc]
        beta = betas[hh][rows, :]
        gc = gcs[hh][rows, :]
        gr = gr_all[hh:hh + 1, rows]
        decay = jnp.where(incl, jnp.exp(jnp.concatenate([gc, gc], axis=1) - gr), 0.0)
        kb = k * beta
        a = _dot_nt(jnp.concatenate([kb, q], axis=0).astype(BF16), k.astype(BF16))
        lows.append(jnp.where(strict, a[:GROUP] * decay, 0.0).astype(BF16))
        attn = a[GROUP:] * decay
        for j in range(GROUP // PAIR):
            pr = slice(j * PAIR, (j + 1) * PAIR)
            at_ref[rows.start + j * PAIR:rows.start + (j + 1) * PAIR, hc] = attn[pr, pr].astype(BF16)
        egc = jnp.exp(gc)
        qg_ref[rows, hc] = (q * egc).astype(BF16)
        lasts = [gc[(c + 1) * CHUNK - 1:(c + 1) * CHUNK, :] for c in range(GROUP // CHUNK)]
        for c, g_last in enumerate(lasts):
            row = rows.start // CHUNK + c
            egl_ref[row:row + 1, hc] = jnp.exp(g_last)
        g_last_rows = jnp.concatenate([jnp.broadcast_to(gl, (CHUNK, DN_DK)) for gl in lasts], axis=0)
        kd_ref[rows, hc] = (k * jnp.exp(g_last_rows - gc)).astype(BF16)
        sols.append(jnp.concatenate([qkv_ref[2, rows, hc] * beta, kb * egc], axis=1))

    mm = lambda a16, b: jnp.dot(a16, b.astype(BF16), preferred_element_type=F32)
    sols = [s - mm(l, s) for l, s in zip(lows, sols)]
    pws = lows
    for _ in range(CHUNK.bit_length() - 2):
        pws = [mm(p, p).astype(BF16) for p in pws]
        sols = [s + mm(p, s) for p, s in zip(pws, sols)]
    for (rows, hh), s in zip(items, sols):
        hc = slice(hh * DN_DK, (hh + 1) * DN_DK)
        u_ref[rows, hc] = s[:, :DN_DK]
        w_ref[rows, hc] = s[:, DN_DK:].astype(BF16)


def _wy(proj, ba, conv_w, alog_pad, dtb_pad, *, seq, ts):
    t = proj.shape[0]
    hw = WY_HEADS * DN_DK
    n_hp = DN_HEADS // WY_HEADS
    dq = DN_HEADS * DN_DK
    cw = conv_w.reshape(CONV_K, 3, n_hp, hw).transpose(2, 0, 1, 3).reshape(n_hp, CONV_K, 3 * hw)
    src = lambda col: pl.BlockSpec((ts, hw), lambda r, h: (r, col // hw + h))
    hblk = pl.BlockSpec((ts, hw), lambda r, h: (r, h))
    wide = jax.ShapeDtypeStruct((t, dq), BF16)
    return pl.pallas_call(
        functools.partial(_wy_kernel, ns=seq // ts),
        grid=(t // ts, n_hp),
        in_specs=[src(COL_Q), src(COL_K), src(COL_V),
                  pl.BlockSpec((ts, LANES), lambda r, h: (r, 0)),
                  pl.BlockSpec((1, CONV_K, 3 * hw), lambda r, h: (h, 0, 0)),
                  _resident((1, LANES)), _resident((1, LANES))],
        out_specs=[hblk, hblk, hblk, hblk, hblk,
                   pl.BlockSpec((ts // CHUNK, hw), lambda r, h: (r, h))],
        out_shape=[jax.ShapeDtypeStruct((t, dq), F32), wide, wide, wide, wide,
                   jax.ShapeDtypeStruct((t // CHUNK, dq), F32)],
        scratch_shapes=[pltpu.VMEM((CONV_HALO + ts, 3 * hw), F32),
                        pltpu.VMEM((n_hp, CONV_HALO, 3 * hw), F32),
                        pltpu.VMEM((2, ts, LANES), F32),
                        pltpu.VMEM((3, ts, hw), F32)],
        compiler_params=_cparams(("arbitrary", "arbitrary")),
        name="wy",
    )(proj, proj, proj, ba, cw, alog_pad, dtb_pad)


def _scan_kernel(u_ref, w_ref, qg_ref, kd_ref, at_ref, egl_ref, z_ref, dnw_ref, out_ref, state_ref):
    ts = u_ref.shape[0]

    @pl.when(pl.program_id(1) == 0)
    def _():
        state_ref[...] = jnp.zeros(state_ref.shape, F32)

    heads = range(DN_HEADS)
    hcs = [slice(h * DN_DK, (h + 1) * DN_DK) for h in heads]
    states = [state_ref[h] for h in heads]
    even_vn = [None] * DN_HEADS
    for c in range(ts // CHUNK):
        rows = slice(c * CHUNK, (c + 1) * CHUNK)
        rs = [jnp.dot(jnp.concatenate([w_ref[rows, hcs[h]], qg_ref[rows, hcs[h]]], axis=0),
                      states[h].astype(BF16), preferred_element_type=F32) for h in heads]
        vns = [(u_ref[rows, hcs[h]] - rs[h][:CHUNK]).astype(BF16) for h in heads]
        states = [states[h] * egl_ref[c:c + 1, hcs[h]] + _dot_tn(kd_ref[rows, hcs[h]], vns[h]) for h in heads]
        for h in heads:
            if c % 2 == 0:
                even_vn[h] = vns[h]
                pair = jnp.concatenate([vns[h], jnp.zeros_like(vns[h])], axis=0)
            else:
                pair = jnp.concatenate([even_vn[h], vns[h]], axis=0)
            o = rs[h][CHUNK:] + jnp.dot(at_ref[rows, hcs[h]], pair, preferred_element_type=F32)
            on = o * lax.rsqrt(jnp.mean(o * o, axis=-1, keepdims=True) + EPS) * dnw_ref[...]
            z = z_ref[rows, hcs[h]].astype(F32)
            out_ref[rows, hcs[h]] = (on * (z * jax.nn.sigmoid(z))).astype(out_ref.dtype)
    for h in heads:
        state_ref[h] = states[h]


def _scan(u, w, qg, kd, at, egl, proj, dn_norm, *, bsz, seq, ts):
    t, dv = u.shape
    ns = seq // ts
    blk = pl.BlockSpec((ts, dv), lambda b, s: (b * ns + s, 0))
    return pl.pallas_call(
        _scan_kernel,
        grid=(bsz, ns),
        in_specs=[blk, blk, blk, blk, blk,
                  pl.BlockSpec((ts // CHUNK, dv), lambda b, s: (b * ns + s, 0)),
                  pl.BlockSpec((ts, dv), lambda b, s: (b * ns + s, COL_Z // dv)),
                  _resident((1, DN_DK))],
        out_specs=blk,
        out_shape=jax.ShapeDtypeStruct((t, dv), BF16),
        scratch_shapes=[pltpu.VMEM((DN_HEADS, DN_DK, DN_DK), F32)],
        compiler_params=_cparams(("parallel", "arbitrary")),
        name="scan",
    )(u, w, qg, kd, at, egl, proj, dn_norm)


ROUTE_E1, ROUTE_E2, ROUTE_W1, ROUTE_W2, ROUTE_R1, ROUTE_R2 = range(6)
ROUTER_LANE0 = N_GROUPS
NEG_BIG = -1e30


def _pack_rows(x):
    n = x.shape[1] // 2
    bits = lambda v: lax.bitcast_convert_type(v.astype(BF16).astype(F32), jnp.uint32)
    return bits(x[:, n:]) | (bits(x[:, :n]) >> 16)


def _unpack_rows(p):
    lo = lax.bitcast_convert_type(p << 16, F32).astype(BF16)
    hi = lax.bitcast_convert_type(p & jnp.uint32(0xFFFF0000), F32).astype(BF16)
    return jnp.concatenate([lo, hi], axis=1)


def _merge_kernel(ydn_ref, gd_ref, mp_ref, x_ref, wupdn_ref, wout_ref, gain_ref, wr_ref, br_ref,
                  h1_ref, n2_ref, route_ref, cnt_ref, carry_ref):
    tm = x_ref.shape[0]

    @pl.when(pl.program_id(0) == 0)
    def _():
        carry_ref[...] = jnp.zeros(carry_ref.shape, F32)

    md = (jax.nn.sigmoid(gd_ref[...].astype(F32))
          * jnp.dot(ydn_ref[...], wupdn_ref[...], preferred_element_type=F32))
    merged = mp_ref[...].astype(F32) + md
    h1 = x_ref[...] + jnp.dot(merged.astype(BF16), wout_ref[...], preferred_element_type=F32)
    h1_ref[...] = h1
    n2 = h1 * lax.rsqrt(jnp.mean(h1 * h1, axis=-1, keepdims=True) + EPS) * gain_ref[...]
    n2_ref[...] = _pack_rows(n2)

    hi = n2.astype(BF16)
    lo = (n2 - hi.astype(F32)).astype(BF16)
    r_hi = jnp.dot(hi, wr_ref[...], preferred_element_type=F32)
    r_lo = jnp.dot(lo, wr_ref[:, :LANES], preferred_element_type=F32)
    logits = r_hi[:, :LANES] + r_hi[:, LANES:] + r_lo + br_ref[...]

    lane = lax.broadcasted_iota(jnp.int32, (tm, LANES), 1)
    lane_f = lane.astype(F32)
    first_max = lambda v, m: jnp.min(jnp.where(v == m, lane_f, float(LANES)), axis=-1, keepdims=True)

    is_grp = lane < N_GROUPS
    lg = jnp.where(is_grp, logits, NEG_BIG)
    gmax = jnp.max(lg, axis=-1, keepdims=True)
    gi = first_max(lg, gmax)
    p_sel = 1.0 / jnp.sum(jnp.where(is_grp, jnp.exp(lg - gmax), 0.0), axis=-1, keepdims=True)

    e_lane = lane - ROUTER_LANE0
    in_grp = (e_lane >= 0) & (e_lane < N_EXPERTS) & ((e_lane // EXPERTS_PER_GROUP).astype(F32) == gi)
    le = jnp.where(in_grp, logits, NEG_BIG)
    v1 = jnp.max(le, axis=-1, keepdims=True)
    i1 = first_max(le, v1)
    le2 = jnp.where(lane_f == i1, NEG_BIG, le)
    v2 = jnp.max(le2, axis=-1, keepdims=True)
    i2 = first_max(le2, v2)
    s = jnp.exp(v2 - v1)
    w1 = p_sel / (1.0 + s)
    w2 = p_sel * s / (1.0 + s)
    e1 = i1 - float(ROUTER_LANE0)
    e2 = i2 - float(ROUTER_LANE0)

    oh1 = jnp.where(lane_f == e1, 1.0, 0.0)
    oh2 = jnp.where(lane_f == e2, 1.0, 0.0)
    both = oh1 + oh2
    ri = lax.broadcasted_iota(jnp.int32, (tm, tm), 0)
    ci = lax.broadcasted_iota(jnp.int32, (tm, tm), 1)
    tri = jnp.where(ri > ci, 1.0, 0.0).astype(BF16)
    prior = jnp.dot(tri, both.astype(BF16), preferred_element_type=F32) + carry_ref[...]
    r1 = jnp.sum(prior * oh1, axis=-1, keepdims=True)
    r2 = jnp.sum(prior * oh2, axis=-1, keepdims=True)
    carry = carry_ref[...] + jnp.sum(both, axis=0, keepdims=True)
    carry_ref[...] = carry
    cnt_ref[...] = carry

    route = jnp.zeros((tm, LANES), F32)
    for ln, val in ((ROUTE_E1, e1), (ROUTE_E2, e2), (ROUTE_W1, w1), (ROUTE_W2, w2),
                    (ROUTE_R1, r1), (ROUTE_R2, r2)):
        route = jnp.where(lane == ln, val, route)
    route_ref[...] = route


def _merge(y_dn, proj, mp, x2d, w_up_dn, w_out, gain, w_router, b_router, *, tm):
    t, d = x2d.shape
    dv = y_dn.shape[1]
    rows = lambda blk: pl.BlockSpec(blk, lambda i: (i, 0))
    return pl.pallas_call(
        _merge_kernel,
        grid=(t // tm,),
        in_specs=[
            rows((tm, dv)),
            pl.BlockSpec((tm, d), lambda i: (i, COL_GATE_DN // d)),
            rows((tm, d)), rows((tm, d)),
            _resident(w_up_dn.shape), _resident(w_out.shape), _resident((1, d)),
            _resident(w_router.shape), _resident((1, LANES)),
        ],
        out_specs=[rows((tm, d)), rows((tm, d // 2)), rows((tm, LANES)),
                   pl.BlockSpec((1, LANES), lambda i: (0, 0))],
        out_shape=[jax.ShapeDtypeStruct((t, d), F32), jax.ShapeDtypeStruct((t, d // 2), jnp.uint32),
                   jax.ShapeDtypeStruct((t, LANES), F32), jax.ShapeDtypeStruct((1, LANES), F32)],
        scratch_shapes=[pltpu.VMEM((1, LANES), F32)],
        compiler_params=_cparams(("arbitrary",)),
        name="merge",
    )(y_dn, proj, mp, x2d, w_up_dn, w_out, gain, w_router, b_router)


def _dispatch_kernel(pos_ref, zrow_ref, nu_ref, n2_ref, xs_ref, zbuf, sem, zsem, *, n_blocks):
    tb = n2_ref.shape[0]
    bm = zbuf.shape[0]
    i = pl.program_id(0)

    @pl.when(i == 0)
    def _():
        zbuf[...] = jnp.zeros(zbuf.shape, zbuf.dtype)
        zero_block = lambda row: pltpu.make_async_copy(
            zbuf, xs_ref.at[pl.ds(pl.multiple_of(row, bm), bm)], zsem)
        for start in (True, False):
            for e in range(N_EXPERTS):
                @pl.when(zrow_ref[e] >= 0)
                def _(e=e, start=start):
                    cp = zero_block(jnp.maximum(zrow_ref[e], 0))
                    cp.start() if start else cp.wait()

            def tail(b, _, start=start):
                cp = zero_block(b * bm)
                cp.start() if start else cp.wait()
                return 0
            lax.fori_loop(nu_ref[0], n_blocks, tail, 0)

    for j in range(tb):
        for k in range(2):
            dst = pos_ref[2 * (i * tb + j) + k]
            pltpu.make_async_copy(n2_ref.at[pl.ds(j, 1)], xs_ref.at[pl.ds(dst, 1)],
                                  sem).start(priority=k)
    for _ in range(2):
        pltpu.make_async_copy(n2_ref, xs_ref.at[pl.ds(0, tb)], sem).wait()


def _dispatch(pos, zrow, n_used, n2, *, cap, bm, tb):
    t, dp = n2.shape
    return pl.pallas_call(
        functools.partial(_dispatch_kernel, n_blocks=cap // bm),
        grid_spec=pltpu.PrefetchScalarGridSpec(
            num_scalar_prefetch=3,
            grid=(t // tb,),
            in_specs=[pl.BlockSpec((tb, dp), lambda i, pos, zrow, nu: (i, 0))],
            out_specs=pl.BlockSpec(memory_space=pl.ANY),
            scratch_shapes=[pltpu.VMEM((bm, dp), n2.dtype), pltpu.SemaphoreType.DMA,
                            pltpu.SemaphoreType.DMA],
        ),
        out_shape=jax.ShapeDtypeStruct((cap, dp), n2.dtype),
        compiler_params=_cparams(("arbitrary",)),
        name="dispatch",
    )(pos, zrow, n_used, n2)


def _experts_kernel(be_ref, nu_ref, x_ref, wg_ref, wu_ref, wd_ref, y_ref, wg16, wu16, wd16):
    i = pl.program_id(0)
    nu = nu_ref[0]

    @pl.when((i < nu) & ((i == 0) | (be_ref[i] != be_ref[jnp.maximum(i - 1, 0)])))
    def _():
        wg16[...] = wg_ref[0].astype(BF16)
        wu16[...] = wu_ref[0].astype(BF16)
        wd16[...] = wd_ref[0].astype(BF16)

    @pl.when(i < nu)
    def _():
        xb = _unpack_rows(x_ref[...])
        g = jnp.dot(xb, wg16[...], preferred_element_type=F32)
        u = jnp.dot(xb, wu16[...], preferred_element_type=F32)
        hmid = (g * jax.nn.sigmoid(g)) * u
        y_ref[...] = _pack_rows(jnp.dot(hmid.astype(BF16), wd16[...], preferred_element_type=F32))

    @pl.when(i >= nu)
    def _():
        y_ref[...] = jnp.zeros(y_ref.shape, y_ref.dtype)


def _experts(blk_e, n_used, xs, w_gate, w_up, w_down, *, bm):
    cap, dp = xs.shape
    d, de = w_gate.shape[1], w_gate.shape[2]
    wspec = lambda shape: pl.BlockSpec(shape, lambda i, be, nu: (be[i], 0, 0))
    return pl.pallas_call(
        _experts_kernel,
        grid_spec=pltpu.PrefetchScalarGridSpec(
            num_scalar_prefetch=2,
            grid=(cap // bm,),
            in_specs=[pl.BlockSpec((bm, dp), lambda i, be, nu: (jnp.minimum(i, nu[0] - 1), 0)),
                      wspec((1, d, de)), wspec((1, d, de)), wspec((1, de, d))],
            out_specs=pl.BlockSpec((bm, dp), lambda i, be, nu: (i, 0)),
            scratch_shapes=[pltpu.VMEM((d, de), BF16), pltpu.VMEM((d, de), BF16), pltpu.VMEM((de, d), BF16)],
        ),
        out_shape=jax.ShapeDtypeStruct((cap, dp), jnp.uint32),
        compiler_params=_cparams(("arbitrary",)),
        name="experts",
    )(blk_e, n_used, xs, w_gate, w_up, w_down)


N_GATHER_BUF = 3


def _combine_kernel(pos_ref, h1_ref, route_ref, p_ref, ys_ref, wpg_ref, wpp_ref, gple_ref, gfin_ref,
                    out_ref, ybuf0, ybuf1, ybuf2, sem, *, final_norm, n_steps):
    bufs = (ybuf0, ybuf1, ybuf2)
    tm = h1_ref.shape[0]
    i = pl.program_id(0)

    def issue(blk, slot):
        for j in range(tm):
            for k in range(2):
                src = pos_ref[2 * (blk * tm + j) + k]
                pltpu.make_async_copy(ys_ref.at[pl.ds(src, 1)], bufs[slot].at[pl.ds(k * tm + j, 1)],
                                      sem.at[slot]).start(priority=k)

    def drain(slot):
        pltpu.make_async_copy(ys_ref.at[pl.ds(0, 2 * tm)], bufs[slot], sem.at[slot]).wait()

    @pl.when(i == 0)
    def _():
        issue(0, 0)
        issue(1, 1)

    for r in range(N_GATHER_BUF):
        @pl.when(i % N_GATHER_BUF == r)
        def _(r=r):
            drain(r)
            issue(i + 2, (r + 2) % N_GATHER_BUF)
            cur = bufs[r]
            route = route_ref[...]
            y_moe = (_unpack_rows(cur[0:tm, :]).astype(F32) * route[:, ROUTE_W1:ROUTE_W1 + 1]
                     + _unpack_rows(cur[tm:2 * tm, :]).astype(F32) * route[:, ROUTE_W2:ROUTE_W2 + 1])
            h2 = h1_ref[...] + y_moe
            n3 = h2 * lax.rsqrt(jnp.mean(h2 * h2, axis=-1, keepdims=True) + EPS) * gple_ref[...]
            gate = jax.nn.sigmoid(jnp.dot(n3.astype(BF16), wpg_ref[...], preferred_element_type=F32))
            h3 = h2 + gate * jnp.dot(p_ref[...].astype(BF16), wpp_ref[...], preferred_element_type=F32)
            if final_norm:
                h3 = h3 * lax.rsqrt(jnp.mean(h3 * h3, axis=-1, keepdims=True) + EPS) * gfin_ref[...]
            out_ref[...] = h3

    last = n_steps - 1

    @pl.when(i == last)
    def _():
        drain((last + 1) % N_GATHER_BUF)
        drain((last + 2) % N_GATHER_BUF)


def _combine(pos, h1, route, p2d, ys, w_ple_gate, w_ple_proj, g_ple, g_fin, *, tm, final_norm):
    t, d = h1.shape
    rows = lambda w: pl.BlockSpec((tm, w), lambda i, pos: (i, 0))
    res = lambda shape: pl.BlockSpec(shape, lambda i, pos: (0,) * len(shape), pipeline_mode=pl.Buffered(1))
    return pl.pallas_call(
        functools.partial(_combine_kernel, final_norm=final_norm, n_steps=t // tm),
        grid_spec=pltpu.PrefetchScalarGridSpec(
            num_scalar_prefetch=1,
            grid=(t // tm,),
            in_specs=[rows(d), rows(LANES), rows(p2d.shape[1]), pl.BlockSpec(memory_space=pl.ANY),
                      res(w_ple_gate.shape), res(w_ple_proj.shape), res((1, d)), res((1, d))],
            out_specs=rows(d),
            scratch_shapes=[pltpu.VMEM((2 * tm, ys.shape[1]), ys.dtype)] * N_GATHER_BUF
                           + [pltpu.SemaphoreType.DMA((N_GATHER_BUF,))],
        ),
        out_shape=jax.ShapeDtypeStruct((t, d), F32),
        compiler_params=_cparams(("arbitrary",)),
        name="combine",
    )(jnp.pad(pos, (0, 2 * 2 * tm)), h1, route, p2d, ys, w_ple_gate, w_ple_proj, g_ple, g_fin)


def _pick(n, pref):
    b = min(n, pref)
    while n % b:
        b -= 8
    return b


MOE_BM = 512


class _Blocks(NamedTuple):
    inproj_rows: int
    inproj_cols: int
    seq_rows: int
    merge_rows: int
    moe_rows: int


def _blocks(t, seq):
    seq_rows = _pick(seq, 512)
    assert seq_rows % GROUP == 0 and (seq_rows // CHUNK) % 8 == 0, "time block must hold whole solve groups"
    return _Blocks(inproj_rows=_pick(t, 1024), inproj_cols=N_MAIN // 4, seq_rows=seq_rows,
                   merge_rows=_pick(t, 512), moe_rows=_pick(t, 256))


def _stage_mixers(x, prm, i, blk):
    bsz, seq, d = x.shape
    t = bsz * seq
    x2d = x.reshape(t, d)
    w_in = prm["w_in"][i]
    d_pool = prm["w_up_pool"].shape[1]
    dq = DN_HEADS * DN_DK
    o_qkv = d_pool
    o_z = o_qkv + 3 * dq
    o_b = o_z + dq
    o_gp = o_b + 2 * DN_HEADS
    o_gd = o_gp + d
    w_main = jnp.concatenate(
        [w_in[:, o_gp:o_gp + d], w_in[:, o_gd:o_gd + d], w_in[:, :d_pool],
         w_in[:, o_qkv:o_qkv + 3 * dq], w_in[:, o_z:o_z + dq]], axis=1).astype(BF16)
    assert w_main.shape[1] == N_MAIN
    w_ba = jnp.pad(w_in[:, o_b:o_b + 2 * DN_HEADS], ((0, 0), (0, LANES - 2 * DN_HEADS))).astype(BF16)
    proj, ba = _inproj(x2d, prm["norm_mix"][i][None, :], w_main, w_ba,
                       tm=blk.inproj_rows, tn=blk.inproj_cols)

    ts = blk.seq_rows
    mp = _pool(proj, prm["pool_w"][i].astype(BF16), prm["pool_scale"][i][None, :],
               prm["w_up_pool"][i].astype(BF16), bsz=bsz, seq=seq, ts=ts)

    lane_pad = lambda a: jnp.pad(a, (DN_HEADS, LANES - 2 * DN_HEADS))[None, :]
    u, w, qg, kd, at, egl = _wy(proj, ba, prm["conv_w"][i], lane_pad(prm["a_log"][i]),
                                lane_pad(prm["dt_bias"][i]), seq=seq, ts=ts)
    y_dn = _scan(u, w, qg, kd, at, egl, proj, prm["dn_norm"][i][None, :], bsz=bsz, seq=seq, ts=ts)
    return dict(proj=proj, ba=ba, mp=mp, y_dn=y_dn)


def _router_weights(w_rg, b_rg, w_re, b_re):
    d = w_rg.shape[0]
    w = jnp.concatenate([w_rg, w_re, jnp.zeros((d, LANES - N_GROUPS - N_EXPERTS), F32)], axis=1)
    w_hi = w.astype(BF16)
    w_lo = (w - w_hi.astype(F32)).astype(BF16)
    b = jnp.concatenate([b_rg, b_re, jnp.zeros((LANES - N_GROUPS - N_EXPERTS,), F32)])[None, :]
    return jnp.concatenate([w_hi, w_lo], axis=1), b


def _stage_moe_plan(route, cnt, t):
    counts = cnt[0, :N_EXPERTS].astype(jnp.int32)
    padded = ((counts + MOE_BM - 1) // MOE_BM) * MOE_BM
    pends = jnp.cumsum(padded)
    pstarts = pends - padded
    eid = route[:, ROUTE_E1:ROUTE_E2 + 1].astype(jnp.int32)
    rank = route[:, ROUTE_R1:ROUTE_R2 + 1].astype(jnp.int32)
    pos = (pstarts[eid] + rank).reshape(2 * t)
    cap = 2 * t + N_EXPERTS * MOE_BM
    n_blocks = cap // MOE_BM
    n_used = (pends[-1] // MOE_BM).astype(jnp.int32)
    blk = jnp.minimum(jnp.arange(n_blocks, dtype=jnp.int32), n_used - 1)
    blk_e = jnp.sum((pends[None, :] <= (blk * MOE_BM)[:, None]).astype(jnp.int32), axis=1)
    blk_e = jnp.minimum(blk_e, N_EXPERTS - 1)
    zrow = jnp.where(padded > 0, pends - MOE_BM, -1).astype(jnp.int32)
    return pos, blk_e, n_used.reshape(1), zrow, cap


def kernel(x, p, norm_mix, w_in, pool_w, pool_scale, conv_w, a_log, dt_bias, dn_norm, w_up_pool, w_up_dn, w_out, norm_moe, w_router_group, b_router_group, w_router_expert, b_router_expert, w_gate, w_up, w_down, norm_ple, w_ple_gate, w_ple_proj, norm_final):
    bsz, seq, d = x.shape
    t = bsz * seq
    depth = norm_mix.shape[0]
    prm = dict(norm_mix=norm_mix, w_in=w_in, pool_w=pool_w, pool_scale=pool_scale, conv_w=conv_w,
               a_log=a_log, dt_bias=dt_bias, dn_norm=dn_norm, w_up_pool=w_up_pool)
    blk = _blocks(t, seq)
    h = x
    for i in range(depth):
        st = _stage_mixers(h, prm, i, blk)
        w_router, b_router = _router_weights(w_router_group[i], b_router_group[i],
                                             w_router_expert[i], b_router_expert[i])
        h1, n2, route, cnt = _merge(st["y_dn"], st["proj"], st["mp"], h.reshape(t, d),
                                    w_up_dn[i].astype(BF16), w_out[i].astype(BF16), norm_moe[i][None, :],
                                    w_router, b_router, tm=blk.merge_rows)
        pos, blk_e, n_used, zrow, cap = _stage_moe_plan(route, cnt, t)
        xs = _dispatch(pos, zrow, n_used, n2, cap=cap, bm=MOE_BM, tb=blk.moe_rows)
        ys = _experts(blk_e, n_used, xs, w_gate[i], w_up[i], w_down[i], bm=MOE_BM)
        out = _combine(pos, h1, route, p[i].reshape(t, -1), ys, w_ple_gate[i].astype(BF16),
                       w_ple_proj[i].astype(BF16), norm_ple[i][None, :], norm_final[None, :],
                       tm=blk.moe_rows, final_norm=(i == depth - 1))
        h = out.reshape(bsz, seq, d)
    return h
```

```python
import functools
from typing import NamedTuple

import jax
import jax.numpy as jnp
from jax import lax
from jax.experimental import pallas as pl
from jax.experimental.pallas import tpu as pltpu

F32 = jnp.float32
BF16 = jnp.bfloat16

EPS = 1e-6
POOL_WINDOWS = (2, 4, 8, 16)
POOL_HALO = 32
assert all(w == 2 ** (g + 1) for g, w in enumerate(POOL_WINDOWS)) and POOL_HALO == 8 * len(POOL_WINDOWS)
CONV_K = 4
CONV_HALO = 8
DN_HEADS = 8
DN_DK = 128
CHUNK = 64
N_GROUPS = 4
EXPERTS_PER_GROUP = 8
N_EXPERTS = N_GROUPS * EXPERTS_PER_GROUP
LANES = 128
VMEM_LIMIT = 56 * 1024 * 1024

COL_GATE_POOL = 0
COL_GATE_DN = 2048
COL_POOL = 4096
COL_Q = 5120
COL_K = 6144
COL_V = 7168
COL_Z = 8192
N_MAIN = 9216


def _cparams(sem):
    return pltpu.CompilerParams(dimension_semantics=sem, vmem_limit_bytes=VMEM_LIMIT)


def _resident(shape):
    nd = len(shape)
    return pl.BlockSpec(shape, lambda *_: (0,) * nd, pipeline_mode=pl.Buffered(1))


def _inproj_kernel(x_ref, gain_ref, w_ref, wba_ref, out_ref, ba_ref, n1_ref, *, sub):
    tm = x_ref.shape[0]

    @pl.when(pl.program_id(1) == 0)
    def _():
        def body(r, _):
            rows = pl.ds(pl.multiple_of(r * sub, sub), sub)
            x = x_ref[rows, :]
            y = x * lax.rsqrt(jnp.mean(x * x, axis=-1, keepdims=True) + EPS) * gain_ref[...]
            n1_ref[rows, :] = y.astype(BF16)
            return 0
        lax.fori_loop(0, tm // sub, body, 0)
        ba_ref[...] = jnp.dot(n1_ref[...], wba_ref[...], preferred_element_type=F32)

    out_ref[...] = jnp.dot(n1_ref[...], w_ref[...], preferred_element_type=F32).astype(out_ref.dtype)


def _inproj(x2d, gain, w_main, w_ba, *, tm, tn):
    t, d = x2d.shape
    n = w_main.shape[1]
    return pl.pallas_call(
        functools.partial(_inproj_kernel, sub=min(tm, 256)),
        grid=(t // tm, n // tn),
        in_specs=[
            pl.BlockSpec((tm, d), lambda i, j: (i, 0)),
            _resident((1, d)),
            pl.BlockSpec((d, tn), lambda i, j: (0, j)),
            _resident((d, LANES)),
        ],
        out_specs=[
            pl.BlockSpec((tm, tn), lambda i, j: (i, j)),
            pl.BlockSpec((tm, LANES), lambda i, j: (i, 0)),
        ],
        out_shape=[jax.ShapeDtypeStruct((t, n), BF16), jax.ShapeDtypeStruct((t, LANES), F32)],
        scratch_shapes=[pltpu.VMEM((tm, d), BF16)],
        compiler_params=_cparams(("parallel", "arbitrary")),
        name="inproj",
    )(x2d, gain, w_main, w_ba)


def _pool_kernel(u_ref, gp_ref, pw_ref, scale_ref, wup_ref, out_ref, ext_ref, sa_ref, sb_ref, y_ref):
    ts = u_ref.shape[0]
    gw = pw_ref.shape[1]
    d_pool = ext_ref.shape[1]
    n = POOL_HALO + ts
    s = pl.program_id(1)

    @pl.when(s == 0)
    def _():
        ext_ref[0:POOL_HALO, :] = jnp.zeros((POOL_HALO, d_pool), F32)

    @pl.when(s > 0)
    def _():
        ext_ref[0:POOL_HALO, :] = ext_ref[ts:n, :]

    ext_ref[POOL_HALO:n, :] = u_ref[...].astype(F32)
    t1 = (s * ts + 1 + lax.broadcasted_iota(jnp.int32, (ts, 1), 0)).astype(F32)

    src = ext_ref
    for g, w in enumerate(POOL_WINDOWS):
        half = w // 2
        first = 8 * (g + 1)
        c0 = g * gw
        sw = src[first:n, c0:d_pool] + src[first - half:n - half, c0:d_pool]
        cols = slice(c0, c0 + gw)
        cur = ext_ref[POOL_HALO:n, cols]
        d = sw[POOL_HALO - first:, 0:gw] / jnp.minimum(t1, float(w)) - cur
        y = jnp.dot(d.astype(BF16), pw_ref[g], preferred_element_type=F32) * scale_ref[:, cols]
        y_ref[:, cols] = y.astype(BF16)
        if g + 1 < len(POOL_WINDOWS):
            dst = sa_ref if g % 2 == 0 else sb_ref
            dst[first:n, c0 + gw:d_pool] = sw[:, gw:]
            src = dst
    up = jnp.dot(y_ref[...], wup_ref[...], preferred_element_type=F32)
    out_ref[...] = (jax.nn.sigmoid(gp_ref[...].astype(F32)) * up).astype(out_ref.dtype)


def _pool(proj, pool_w, pool_scale, w_up_pool, *, bsz, seq, ts):
    t = proj.shape[0]
    ns = seq // ts
    d_pool = w_up_pool.shape[0]
    d = w_up_pool.shape[1]
    return pl.pallas_call(
        _pool_kernel,
        grid=(bsz, ns),
        in_specs=[
            pl.BlockSpec((ts, d_pool), lambda b, s: (b * ns + s, COL_POOL // d_pool)),
            pl.BlockSpec((ts, d), lambda b, s: (b * ns + s, COL_GATE_POOL // d)),
            _resident(pool_w.shape),
            _resident((1, d_pool)),
            _resident(w_up_pool.shape),
        ],
        out_specs=pl.BlockSpec((ts, d), lambda b, s: (b * ns + s, 0)),
        out_shape=jax.ShapeDtypeStruct((t, d), BF16),
        scratch_shapes=[pltpu.VMEM((POOL_HALO + ts, d_pool), F32)] * 3 + [pltpu.VMEM((ts, d_pool), BF16)],
        compiler_params=_cparams(("parallel", "arbitrary")),
        name="pool",
    )(proj, proj, pool_w, pool_scale, w_up_pool)


def _dot_nt(a, b):
    return lax.dot_general(a, b, (((1,), (1,)), ((), ())), preferred_element_type=F32)


def _dot_tn(a, b):
    return lax.dot_general(a, b, (((0,), (0,)), ((), ())), preferred_element_type=F32)


GROUP = 256
PAIR = 2 * CHUNK
WY_HEADS = 4


def _wy_kernel(q_ref, k_ref, v_ref, ba_ref, cw_ref, alog_ref, dtb_ref,
               u_ref, w_ref, qg_ref, kd_ref, at_ref, egl_ref, ext, carry_ref, gate_ref, qkv_ref, *, ns):
    ts, hw = q_ref.shape
    hp = pl.program_id(1)
    s = pl.program_id(0) % ns

    @pl.when(s == 0)
    def _():
        ext[0:CONV_HALO, :] = jnp.zeros((CONV_HALO, 3 * hw), F32)

    @pl.when(s > 0)
    def _():
        ext[0:CONV_HALO, :] = carry_ref[hp]

    ext[CONV_HALO:CONV_HALO + ts, 0:hw] = q_ref[...].astype(F32)
    ext[CONV_HALO:CONV_HALO + ts, hw:2 * hw] = k_ref[...].astype(F32)
    ext[CONV_HALO:CONV_HALO + ts, 2 * hw:3 * hw] = v_ref[...].astype(F32)
    carry_ref[hp] = ext[ts:ts + CONV_HALO, :]

    def conv_silu(c0):
        cols = slice(c0, c0 + DN_DK)
        first = CONV_HALO - (CONV_K - 1)
        y = ext[first:first + ts, cols] * cw_ref[0, 0:1, cols]
        for tap in range(1, CONV_K):
            y = y + ext[first + tap:first + tap + ts, cols] * cw_ref[0, tap:tap + 1, cols]
        return y * jax.nn.sigmoid(y)

    @pl.when(hp == 0)
    def _():
        ba = ba_ref[...]
        xs = ba + dtb_ref[...]
        softplus = jnp.maximum(xs, 0.0) + jnp.log1p(jnp.exp(-jnp.abs(xs)))
        g_all = -jnp.exp(alog_ref[...]) * softplus
        ri = lax.broadcasted_iota(jnp.int32, (ts, ts), 0)
        ci = lax.broadcasted_iota(jnp.int32, (ts, ts), 1)
        tri = jnp.where((ri // CHUNK == ci // CHUNK) & (ci <= ri), 1.0, 0.0).astype(F32)
        gate_ref[0] = jax.nn.sigmoid(ba)
        gate_ref[1] = jnp.dot(tri, g_all, preferred_element_type=F32, precision=lax.Precision.HIGHEST)

    beta_all = gate_ref[0]
    gcum = gate_ref[1]
    lane = lax.broadcasted_iota(jnp.int32, (ts, LANES), 1)
    pick = lambda arr, ln: jnp.sum(jnp.where(lane == ln, arr, 0.0), axis=-1, keepdims=True)
    row8 = lax.broadcasted_iota(jnp.int32, (8, LANES), 0)
    lane8 = lax.broadcasted_iota(jnp.int32, (8, LANES), 1)
    pick_rows = jnp.where(lane8 == DN_HEADS + hp * WY_HEADS + row8, 1.0, 0.0).astype(F32)
    gr_all = lax.dot_general(pick_rows, gcum, (((1,), (1,)), ((), ())), preferred_element_type=F32,
                             precision=lax.Precision.HIGHEST)

    betas, gcs = [], []
    for hh in range(WY_HEADS):
        hc = slice(hh * DN_DK, (hh + 1) * DN_DK)
        qh = conv_silu(hh * DN_DK)
        qkv_ref[0, :, hc] = qh * (lax.rsqrt(jnp.sum(qh * qh, axis=-1, keepdims=True) + EPS) * (DN_DK ** -0.5))
        kh = conv_silu(hw + hh * DN_DK)
        qkv_ref[1, :, hc] = kh * lax.rsqrt(jnp.sum(kh * kh, axis=-1, keepdims=True) + EPS)
        qkv_ref[2, :, hc] = conv_silu(2 * hw + hh * DN_DK)
        head = hp * WY_HEADS + hh
        betas.append(jnp.broadcast_to(pick(beta_all, head), (ts, DN_DK)))
        gcs.append(jnp.broadcast_to(pick(gcum, DN_HEADS + head), (ts, DN_DK)))

    ii = lax.broadcasted_iota(jnp.int32, (GROUP, GROUP), 0)
    jj = lax.broadcasted_iota(jnp.int32, (GROUP, GROUP), 1)
    same = (ii // CHUNK) == (jj // CHUNK)
    incl = same & (ii >= jj)
    strict = same & (ii > jj)

    items = [(slice(g * GROUP, (g + 1) * GROUP), hh) for hh in range(WY_HEADS)
             for g in range(ts // GROUP)]
    lows, sols = [], []
    for rows, hh in items:
        hc = slice(hh * DN_DK, (hh + 1) * DN_DK)
        q = qkv_ref[0, rows, hc]
        k = qkv_ref[1, rows, hc]
        beta = betas[hh][rows, :]
        gc = gcs[hh][rows, :]
        gr = gr_all[hh:hh + 1, rows]
        decay = jnp.where(incl, jnp.exp(jnp.concatenate([gc, gc], axis=1) - gr), 0.0)
        kb = k * beta
        a = _dot_nt(jnp.concatenate([kb, q], axis=0).astype(BF16), k.astype(BF16))
        lows.append(jnp.where(strict, a[:GROUP] * decay, 0.0).astype(BF16))
        attn = a[GROUP:] * decay
        for j in range(GROUP // PAIR):
            pr = slice(j * PAIR, (j + 1) * PAIR)
            at_ref[rows.start + j * PAIR:rows.start + (j + 1) * PAIR, hc] = attn[pr, pr].astype(BF16)
        egc = jnp.exp(gc)
        qg_ref[rows, hc] = (q * egc).astype(BF16)
        lasts = [gc[(c + 1) * CHUNK - 1:(c + 1) * CHUNK, :] for c in range(GROUP // CHUNK)]
        for c, g_last in enumerate(lasts):
            row = rows.start // CHUNK + c
            egl_ref[row:row + 1, hc] = jnp.exp(g_last)
        g_last_rows = jnp.concatenate([jnp.broadcast_to(gl, (CHUNK, DN_DK)) for gl in lasts], axis=0)
        kd_ref[rows, hc] = (k * jnp.exp(g_last_rows - gc)).astype(BF16)
        sols.append(jnp.concatenate([qkv_ref[2, rows, hc] * beta, kb * egc], axis=1))

    mm = lambda a16, b: jnp.dot(a16, b.astype(BF16), preferred_element_type=F32)
    sols = [s - mm(l, s) for l, s in zip(lows, sols)]
    pws = lows
    for _ in range(CHUNK.bit_length() - 2):
        pws = [mm(p, p).astype(BF16) for p in pws]
        sols = [s + mm(p, s) for p, s in zip(pws, sols)]
    for (rows, hh), s in zip(items, sols):
        hc = slice(hh * DN_DK, (hh + 1) * DN_DK)
        u_ref[rows, hc] = s[:, :DN_DK]
        w_ref[rows, hc] = s[:, DN_DK:].astype(BF16)


def _wy(proj, ba, conv_w, alog_pad, dtb_pad, *, seq, ts):
    t = proj.shape[0]
    hw = WY_HEADS * DN_DK
    n_hp = DN_HEADS // WY_HEADS
    dq = DN_HEADS * DN_DK
    cw = conv_w.reshape(CONV_K, 3, n_hp, hw).transpose(2, 0, 1, 3).reshape(n_hp, CONV_K, 3 * hw)
    src = lambda col: pl.BlockSpec((ts, hw), lambda r, h: (r, col // hw + h))
    hblk = pl.BlockSpec((ts, hw), lambda r, h: (r, h))
    wide = jax.ShapeDtypeStruct((t, dq), BF16)
    return pl.pallas_call(
        functools.partial(_wy_kernel, ns=seq // ts),
        grid=(t // ts, n_hp),
        in_specs=[src(COL_Q), src(COL_K), src(COL_V),
                  pl.BlockSpec((ts, LANES), lambda r, h: (r, 0)),
                  pl.BlockSpec((1, CONV_K, 3 * hw), lambda r, h: (h, 0, 0)),
                  _resident((1, LANES)), _resident((1, LANES))],
        out_specs=[hblk, hblk, hblk, hblk, hblk,
                   pl.BlockSpec((ts // CHUNK, hw), lambda r, h: (r, h))],
        out_shape=[jax.ShapeDtypeStruct((t, dq), F32), wide, wide, wide, wide,
                   jax.ShapeDtypeStruct((t // CHUNK, dq), F32)],
        scratch_shapes=[pltpu.VMEM((CONV_HALO + ts, 3 * hw), F32),
                        pltpu.VMEM((n_hp, CONV_HALO, 3 * hw), F32),
                        pltpu.VMEM((2, ts, LANES), F32),
                        pltpu.VMEM((3, ts, hw), F32)],
        compiler_params=_cparams(("arbitrary", "arbitrary")),
        name="wy",
    )(proj, proj, proj, ba, cw, alog_pad, dtb_pad)


def _scan_kernel(u_ref, w_ref, qg_ref, kd_ref, at_ref, egl_ref, z_ref, dnw_ref, out_ref, state_ref):
    ts = u_ref.shape[0]

    @pl.when(pl.program_id(1) == 0)
    def _():
        state_ref[...] = jnp.zeros(state_ref.shape, F32)

    heads = range(DN_HEADS)
    hcs = [slice(h * DN_DK, (h + 1) * DN_DK) for h in heads]
    states = [state_ref[h] for h in heads]
    even_vn = [None] * DN_HEADS
    for c in range(ts // CHUNK):
        rows = slice(c * CHUNK, (c + 1) * CHUNK)
        rs = [jnp.dot(jnp.concatenate([w_ref[rows, hcs[h]], qg_ref[rows, hcs[h]]], axis=0),
                      states[h].astype(BF16), preferred_element_type=F32) for h in heads]
        vns = [(u_ref[rows, hcs[h]] - rs[h][:CHUNK]).astype(BF16) for h in heads]
        states = [states[h] * egl_ref[c:c + 1, hcs[h]] + _dot_tn(kd_ref[rows, hcs[h]], vns[h]) for h in heads]
        for h in heads:
            if c % 2 == 0:
                even_vn[h] = vns[h]
                pair = jnp.concatenate([vns[h], jnp.zeros_like(vns[h])], axis=0)
            else:
                pair = jnp.concatenate([even_vn[h], vns[h]], axis=0)
            o = rs[h][CHUNK:] + jnp.dot(at_ref[rows, hcs[h]], pair, preferred_element_type=F32)
            on = o * lax.rsqrt(jnp.mean(o * o, axis=-1, keepdims=True) + EPS) * dnw_ref[...]
            z = z_ref[rows, hcs[h]].astype(F32)
            out_ref[rows, hcs[h]] = (on * (z * jax.nn.sigmoid(z))).astype(out_ref.dtype)
    for h in heads:
        state_ref[h] = states[h]


def _scan(u, w, qg, kd, at, egl, proj, dn_norm, *, bsz, seq, ts):
    t, dv = u.shape
    ns = seq // ts
    blk = pl.BlockSpec((ts, dv), lambda b, s: (b * ns + s, 0))
    return pl.pallas_call(
        _scan_kernel,
        grid=(bsz, ns),
        in_specs=[blk, blk, blk, blk, blk,
                  pl.BlockSpec((ts // CHUNK, dv), lambda b, s: (b * ns + s, 0)),
                  pl.BlockSpec((ts, dv), lambda b, s: (b * ns + s, COL_Z // dv)),
                  _resident((1, DN_DK))],
        out_specs=blk,
        out_shape=jax.ShapeDtypeStruct((t, dv), BF16),
        scratch_shapes=[pltpu.VMEM((DN_HEADS, DN_DK, DN_DK), F32)],
        compiler_params=_cparams(("parallel", "arbitrary")),
        name="scan",
    )(u, w, qg, kd, at, egl, proj, dn_norm)


ROUTE_E1, ROUTE_E2, ROUTE_W1, ROUTE_W2, ROUTE_R1, ROUTE_R2 = range(6)
ROUTER_LANE0 = N_GROUPS
NEG_BIG = -1e30


def _pack_rows(x):
    n = x.shape[1] // 2
    bits = lambda v: lax.bitcast_convert_type(v.astype(BF16).astype(F32), jnp.uint32)
    return bits(x[:, n:]) | (bits(x[:, :n]) >> 16)


def _unpack_rows(p):
    lo = lax.bitcast_convert_type(p << 16, F32).astype(BF16)
    hi = lax.bitcast_convert_type(p & jnp.uint32(0xFFFF0000), F32).astype(BF16)
    return jnp.concatenate([lo, hi], axis=1)


def _merge_kernel(ydn_ref, gd_ref, mp_ref, x_ref, wupdn_ref, wout_ref, gain_ref, wr_ref, br_ref,
                  h1_ref, n2_ref, route_ref, route_t_ref, cnt_ref, carry_ref):
    tm = x_ref.shape[0]

    @pl.when(pl.program_id(0) == 0)
    def _():
        carry_ref[...] = jnp.zeros(carry_ref.shape, F32)

    md = (jax.nn.sigmoid(gd_ref[...].astype(F32))
          * jnp.dot(ydn_ref[...], wupdn_ref[...], preferred_element_type=F32))
    merged = mp_ref[...].astype(F32) + md
    h1 = x_ref[...] + jnp.dot(merged.astype(BF16), wout_ref[...], preferred_element_type=F32)
    h1_ref[...] = h1
    n2 = h1 * lax.rsqrt(jnp.mean(h1 * h1, axis=-1, keepdims=True) + EPS) * gain_ref[...]
    n2_ref[...] = _pack_rows(n2)

    hi = n2.astype(BF16)
    lo = (n2 - hi.astype(F32)).astype(BF16)
    r_hi = jnp.dot(hi, wr_ref[...], preferred_element_type=F32)
    r_lo = jnp.dot(lo, wr_ref[:, :LANES], preferred_element_type=F32)
    logits = r_hi[:, :LANES] + r_hi[:, LANES:] + r_lo + br_ref[...]

    lane = lax.broadcasted_iota(jnp.int32, (tm, LANES), 1)
    lane_f = lane.astype(F32)
    first_max = lambda v, m: jnp.min(jnp.where(v == m, lane_f, float(LANES)), axis=-1, keepdims=True)

    is_grp = lane < N_GROUPS
    lg = jnp.where(is_grp, logits, NEG_BIG)
    gmax = jnp.max(lg, axis=-1, keepdims=True)
    gi = first_max(lg, gmax)
    p_sel = 1.0 / jnp.sum(jnp.where(is_grp, jnp.exp(lg - gmax), 0.0), axis=-1, keepdims=True)

    e_lane = lane - ROUTER_LANE0
    in_grp = (e_lane >= 0) & (e_lane < N_EXPERTS) & ((e_lane // EXPERTS_PER_GROUP).astype(F32) == gi)
    le = jnp.where(in_grp, logits, NEG_BIG)
    v1 = jnp.max(le, axis=-1, keepdims=True)
    i1 = first_max(le, v1)
    le2 = jnp.where(lane_f == i1, NEG_BIG, le)
    v2 = jnp.max(le2, axis=-1, keepdims=True)
    i2 = first_max(le2, v2)
    s = jnp.exp(v2 - v1)
    w1 = p_sel / (1.0 + s)
    w2 = p_sel * s / (1.0 + s)
    e1 = i1 - float(ROUTER_LANE0)
    e2 = i2 - float(ROUTER_LANE0)

    oh1 = jnp.where(lane_f == e1, 1.0, 0.0)
    oh2 = jnp.where(lane_f == e2, 1.0, 0.0)
    both = oh1 + oh2
    ri = lax.broadcasted_iota(jnp.int32, (tm, tm), 0)
    ci = lax.broadcasted_iota(jnp.int32, (tm, tm), 1)
    tri = jnp.where(ri > ci, 1.0, 0.0).astype(BF16)
    prior = jnp.dot(tri, both.astype(BF16), preferred_element_type=F32) + carry_ref[...]
    r1 = jnp.sum(prior * oh1, axis=-1, keepdims=True)
    r2 = jnp.sum(prior * oh2, axis=-1, keepdims=True)
    carry = carry_ref[...] + jnp.sum(both, axis=0, keepdims=True)
    carry_ref[...] = carry
    cnt_ref[...] = carry

    route = jnp.zeros((tm, LANES), F32)
    for ln, val in ((ROUTE_E1, e1), (ROUTE_E2, e2), (ROUTE_W1, w1), (ROUTE_W2, w2),
                    (ROUTE_R1, r1), (ROUTE_R2, r2)):
        route = jnp.where(lane == ln, val, route)
    route_ref[...] = route
    route_t_ref[...] = route.T[0:8, :]


def _merge(y_dn, proj, mp, x2d, w_up_dn, w_out, gain, w_router, b_router, *, tm):
    t, d = x2d.shape
    dv = y_dn.shape[1]
    rows = lambda blk: pl.BlockSpec(blk, lambda i: (i, 0))
    return pl.pallas_call(
        _merge_kernel,
        grid=(t // tm,),
        in_specs=[
            rows((tm, dv)),
            pl.BlockSpec((tm, d), lambda i: (i, COL_GATE_DN // d)),
            rows((tm, d)), rows((tm, d)),
            _resident(w_up_dn.shape), _resident(w_out.shape), _resident((1, d)),
            _resident(w_router.shape), _resident((1, LANES)),
        ],
        out_specs=[rows((tm, d)), rows((tm, d // 2)), rows((tm, LANES)),
                   pl.BlockSpec((8, tm), lambda i: (0, i)),
                   pl.BlockSpec((1, LANES), lambda i: (0, 0))],
        out_shape=[jax.ShapeDtypeStruct((t, d), F32), jax.ShapeDtypeStruct((t, d // 2), jnp.uint32),
                   jax.ShapeDtypeStruct((t, LANES), F32), jax.ShapeDtypeStruct((8, t), F32),
                   jax.ShapeDtypeStruct((1, LANES), F32)],
        scratch_shapes=[pltpu.VMEM((1, LANES), F32)],
        compiler_params=_cparams(("arbitrary",)),
        name="merge",
    )(y_dn, proj, mp, x2d, w_up_dn, w_out, gain, w_router, b_router)


def _dispatch_kernel(pos_ref, zrow_ref, nu_ref, n2_ref, xs_ref, zbuf, sem, zsem, *, n_blocks, n_tok):
    tb = n2_ref.shape[0]
    bm = zbuf.shape[0]
    i = pl.program_id(0)

    @pl.when(i == 0)
    def _():
        zbuf[...] = jnp.zeros(zbuf.shape, zbuf.dtype)
        zero_block = lambda row: pltpu.make_async_copy(
            zbuf, xs_ref.at[pl.ds(pl.multiple_of(row, bm), bm)], zsem)
        for start in (True, False):
            for e in range(N_EXPERTS):
                @pl.when(zrow_ref[e] >= 0)
                def _(e=e, start=start):
                    cp = zero_block(jnp.maximum(zrow_ref[e], 0))
                    cp.start() if start else cp.wait()

            def tail(b, _, start=start):
                cp = zero_block(b * bm)
                cp.start() if start else cp.wait()
                return 0
            lax.fori_loop(nu_ref[0], n_blocks, tail, 0)

    for j in range(tb):
        for k in range(2):
            dst = pos_ref[k * n_tok + i * tb + j]
            pltpu.make_async_copy(n2_ref.at[pl.ds(j, 1)], xs_ref.at[pl.ds(dst, 1)],
                                  sem).start(priority=k)
    for _ in range(2):
        pltpu.make_async_copy(n2_ref, xs_ref.at[pl.ds(0, tb)], sem).wait()


def _dispatch(pos, zrow, n_used, n2, *, cap, bm, tb):
    t, dp = n2.shape
    return pl.pallas_call(
        functools.partial(_dispatch_kernel, n_blocks=cap // bm, n_tok=t),
        grid_spec=pltpu.PrefetchScalarGridSpec(
            num_scalar_prefetch=3,
            grid=(t // tb,),
            in_specs=[pl.BlockSpec((tb, dp), lambda i, pos, zrow, nu: (i, 0))],
            out_specs=pl.BlockSpec(memory_space=pl.ANY),
            scratch_shapes=[pltpu.VMEM((bm, dp), n2.dtype), pltpu.SemaphoreType.DMA,
                            pltpu.SemaphoreType.DMA],
        ),
        out_shape=jax.ShapeDtypeStruct((cap, dp), n2.dtype),
        compiler_params=_cparams(("arbitrary",)),
        name="dispatch",
    )(pos, zrow, n_used, n2)


def _experts_kernel(be_ref, nu_ref, x_ref, wg_ref, wu_ref, wd_ref, y_ref, wg16, wu16, wd16):
    i = pl.program_id(0)
    nu = nu_ref[0]

    @pl.when((i < nu) & ((i == 0) | (be_ref[i] != be_ref[jnp.maximum(i - 1, 0)])))
    def _():
        wg16[...] = wg_ref[0].astype(BF16)
        wu16[...] = wu_ref[0].astype(BF16)
        wd16[...] = wd_ref[0].astype(BF16)

    @pl.when(i < nu)
    def _():
        xb = _unpack_rows(x_ref[...])
        g = jnp.dot(xb, wg16[...], preferred_element_type=F32)
        u = jnp.dot(xb, wu16[...], preferred_element_type=F32)
        hmid = (g * jax.nn.sigmoid(g)) * u
        y_ref[...] = _pack_rows(jnp.dot(hmid.astype(BF16), wd16[...], preferred_element_type=F32))

    @pl.when(i >= nu)
    def _():
        y_ref[...] = jnp.zeros(y_ref.shape, y_ref.dtype)


def _experts(blk_e, n_used, xs, w_gate, w_up, w_down, *, bm):
    cap, dp = xs.shape
    d, de = w_gate.shape[1], w_gate.shape[2]
    wspec = lambda shape: pl.BlockSpec(shape, lambda i, be, nu: (be[i], 0, 0))
    return pl.pallas_call(
        _experts_kernel,
        grid_spec=pltpu.PrefetchScalarGridSpec(
            num_scalar_prefetch=2,
            grid=(cap // bm,),
            in_specs=[pl.BlockSpec((bm, dp), lambda i, be, nu: (jnp.minimum(i, nu[0] - 1), 0)),
                      wspec((1, d, de)), wspec((1, d, de)), wspec((1, de, d))],
            out_specs=pl.BlockSpec((bm, dp), lambda i, be, nu: (i, 0)),
            scratch_shapes=[pltpu.VMEM((d, de), BF16), pltpu.VMEM((d, de), BF16), pltpu.VMEM((de, d), BF16)],
        ),
        out_shape=jax.ShapeDtypeStruct((cap, dp), jnp.uint32),
        compiler_params=_cparams(("arbitrary",)),
        name="experts",
    )(blk_e, n_used, xs, w_gate, w_up, w_down)


N_GATHER_BUF = 3


def _combine_kernel(pos_ref, h1_ref, route_ref, p_ref, ys_ref, wpg_ref, wpp_ref, gple_ref, gfin_ref,
                    out_ref, ybuf0, ybuf1, ybuf2, sem, *, final_norm, n_steps):
    bufs = (ybuf0, ybuf1, ybuf2)
    tm = h1_ref.shape[0]
    i = pl.program_id(0)

    def issue(blk, slot):
        for j in range(tm):
            for k in range(2):
                src = pos_ref[k * (n_steps * tm) + blk * tm + j]
                pltpu.make_async_copy(ys_ref.at[pl.ds(src, 1)], bufs[slot].at[pl.ds(k * tm + j, 1)],
                                      sem.at[slot]).start(priority=k)

    def drain(slot):
        pltpu.make_async_copy(ys_ref.at[pl.ds(0, 2 * tm)], bufs[slot], sem.at[slot]).wait()

    @pl.when(i == 0)
    def _():
        issue(0, 0)
        issue(1, 1)

    for r in range(N_GATHER_BUF):
        @pl.when(i % N_GATHER_BUF == r)
        def _(r=r):
            drain(r)
            issue(i + 2, (r + 2) % N_GATHER_BUF)
            cur = bufs[r]
            route = route_ref[...]
            y_moe = (_unpack_rows(cur[0:tm, :]).astype(F32) * route[:, ROUTE_W1:ROUTE_W1 + 1]
                     + _unpack_rows(cur[tm:2 * tm, :]).astype(F32) * route[:, ROUTE_W2:ROUTE_W2 + 1])
            h2 = h1_ref[...] + y_moe
            n3 = h2 * lax.rsqrt(jnp.mean(h2 * h2, axis=-1, keepdims=True) + EPS) * gple_ref[...]
            gate = jax.nn.sigmoid(jnp.dot(n3.astype(BF16), wpg_ref[...], preferred_element_type=F32))
            h3 = h2 + gate * jnp.dot(p_ref[...].astype(BF16), wpp_ref[...], preferred_element_type=F32)
            if final_norm:
                h3 = h3 * lax.rsqrt(jnp.mean(h3 * h3, axis=-1, keepdims=True) + EPS) * gfin_ref[...]
            out_ref[...] = h3

    last = n_steps - 1

    @pl.when(i == last)
    def _():
        drain((last + 1) % N_GATHER_BUF)
        drain((last + 2) % N_GATHER_BUF)


def _combine(pos, h1, route, p2d, ys, w_ple_gate, w_ple_proj, g_ple, g_fin, *, tm, final_norm):
    t, d = h1.shape
    rows = lambda w: pl.BlockSpec((tm, w), lambda i, pos: (i, 0))
    res = lambda shape: pl.BlockSpec(shape, lambda i, pos: (0,) * len(shape), pipeline_mode=pl.Buffered(1))
    return pl.pallas_call(
        functools.partial(_combine_kernel, final_norm=final_norm, n_steps=t // tm),
        grid_spec=pltpu.PrefetchScalarGridSpec(
            num_scalar_prefetch=1,
            grid=(t // tm,),
            in_specs=[rows(d), rows(LANES), rows(p2d.shape[1]), pl.BlockSpec(memory_space=pl.ANY),
                      res(w_ple_gate.shape), res(w_ple_proj.shape), res((1, d)), res((1, d))],
            out_specs=rows(d),
            scratch_shapes=[pltpu.VMEM((2 * tm, ys.shape[1]), ys.dtype)] * N_GATHER_BUF
                           + [pltpu.SemaphoreType.DMA((N_GATHER_BUF,))],
        ),
        out_shape=jax.ShapeDtypeStruct((t, d), F32),
        compiler_params=_cparams(("arbitrary",)),
        name="combine",
    )(jnp.pad(pos, (0, 2 * tm)), h1, route, p2d, ys, w_ple_gate, w_ple_proj, g_ple, g_fin)


def _pick(n, pref):
    b = min(n, pref)
    while n % b:
        b -= 8
    return b


MOE_BM = 512


class _Blocks(NamedTuple):
    inproj_rows: int
    inproj_cols: int
    seq_rows: int
    merge_rows: int
    moe_rows: int


def _blocks(t, seq):
    seq_rows = _pick(seq, 512)
    assert seq_rows % GROUP == 0 and (seq_rows // CHUNK) % 8 == 0, "time block must hold whole solve groups"
    return _Blocks(inproj_rows=_pick(t, 1024), inproj_cols=N_MAIN // 4, seq_rows=seq_rows,
                   merge_rows=_pick(t, 512), moe_rows=_pick(t, 256))


def _stage_mixers(x, prm, i, blk):
    bsz, seq, d = x.shape
    t = bsz * seq
    x2d = x.reshape(t, d)
    w_in = prm["w_in"][i]
    d_pool = prm["w_up_pool"].shape[1]
    dq = DN_HEADS * DN_DK
    o_qkv = d_pool
    o_z = o_qkv + 3 * dq
    o_b = o_z + dq
    o_gp = o_b + 2 * DN_HEADS
    o_gd = o_gp + d
    w_main = jnp.concatenate(
        [w_in[:, o_gp:o_gp + d], w_in[:, o_gd:o_gd + d], w_in[:, :d_pool],
         w_in[:, o_qkv:o_qkv + 3 * dq], w_in[:, o_z:o_z + dq]], axis=1).astype(BF16)
    assert w_main.shape[1] == N_MAIN
    w_ba = jnp.pad(w_in[:, o_b:o_b + 2 * DN_HEADS], ((0, 0), (0, LANES - 2 * DN_HEADS))).astype(BF16)
    proj, ba = _inproj(x2d, prm["norm_mix"][i][None, :], w_main, w_ba,
                       tm=blk.inproj_rows, tn=blk.inproj_cols)

    ts = blk.seq_rows
    mp = _pool(proj, prm["pool_w"][i].astype(BF16), prm["pool_scale"][i][None, :],
               prm["w_up_pool"][i].astype(BF16), bsz=bsz, seq=seq, ts=ts)

    lane_pad = lambda a: jnp.pad(a, (DN_HEADS, LANES - 2 * DN_HEADS))[None, :]
    u, w, qg, kd, at, egl = _wy(proj, ba, prm["conv_w"][i], lane_pad(prm["a_log"][i]),
                                lane_pad(prm["dt_bias"][i]), seq=seq, ts=ts)
    y_dn = _scan(u, w, qg, kd, at, egl, proj, prm["dn_norm"][i][None, :], bsz=bsz, seq=seq, ts=ts)
    return dict(proj=proj, ba=ba, mp=mp, y_dn=y_dn)


def _router_weights(w_rg, b_rg, w_re, b_re):
    d = w_rg.shape[0]
    w = jnp.concatenate([w_rg, w_re, jnp.zeros((d, LANES - N_GROUPS - N_EXPERTS), F32)], axis=1)
    w_hi = w.astype(BF16)
    w_lo = (w - w_hi.astype(F32)).astype(BF16)
    b = jnp.concatenate([b_rg, b_re, jnp.zeros((LANES - N_GROUPS - N_EXPERTS,), F32)])[None, :]
    return jnp.concatenate([w_hi, w_lo], axis=1), b


def _stage_moe_plan(route_t, cnt, t):
    counts = cnt[0, :N_EXPERTS].astype(jnp.int32)
    padded = ((counts + MOE_BM - 1) // MOE_BM) * MOE_BM
    pends = jnp.cumsum(padded)
    pstarts = pends - padded
    eid = route_t[ROUTE_E1:ROUTE_E2 + 1].astype(jnp.int32)
    rank = route_t[ROUTE_R1:ROUTE_R2 + 1].astype(jnp.int32)
    first = jnp.sum(jnp.where(eid[..., None] == jnp.arange(N_EXPERTS), pstarts, 0), axis=-1)
    pos = (first + rank).reshape(2 * t)
    cap = 2 * t + N_EXPERTS * MOE_BM
    n_blocks = cap // MOE_BM
    n_used = (pends[-1] // MOE_BM).astype(jnp.int32)
    blk = jnp.minimum(jnp.arange(n_blocks, dtype=jnp.int32), n_used - 1)
    blk_e = jnp.sum((pends[None, :] <= (blk * MOE_BM)[:, None]).astype(jnp.int32), axis=1)
    blk_e = jnp.minimum(blk_e, N_EXPERTS - 1)
    zrow = jnp.where(padded > 0, pends - MOE_BM, -1).astype(jnp.int32)
    return pos, blk_e, n_used.reshape(1), zrow, cap


def kernel(x, p, norm_mix, w_in, pool_w, pool_scale, conv_w, a_log, dt_bias, dn_norm, w_up_pool, w_up_dn, w_out, norm_moe, w_router_group, b_router_group, w_router_expert, b_router_expert, w_gate, w_up, w_down, norm_ple, w_ple_gate, w_ple_proj, norm_final):
    bsz, seq, d = x.shape
    t = bsz * seq
    depth = norm_mix.shape[0]
    prm = dict(norm_mix=norm_mix, w_in=w_in, pool_w=pool_w, pool_scale=pool_scale, conv_w=conv_w,
               a_log=a_log, dt_bias=dt_bias, dn_norm=dn_norm, w_up_pool=w_up_pool)
    blk = _blocks(t, seq)
    h = x
    for i in range(depth):
        st = _stage_mixers(h, prm, i, blk)
        w_router, b_router = _router_weights(w_router_group[i], b_router_group[i],
                                             w_router_expert[i], b_router_expert[i])
        h1, n2, route, route_t, cnt = _merge(st["y_dn"], st["proj"], st["mp"], h.reshape(t, d),
                                             w_up_dn[i].astype(BF16), w_out[i].astype(BF16),
                                             norm_moe[i][None, :], w_router, b_router, tm=blk.merge_rows)
        pos, blk_e, n_used, zrow, cap = _stage_moe_plan(route_t, cnt, t)
        xs = _dispatch(pos, zrow, n_used, n2, cap=cap, bm=MOE_BM, tb=blk.moe_rows)
        ys = _experts(blk_e, n_used, xs, w_gate[i], w_up[i], w_down[i], bm=MOE_BM)
        out = _combine(pos, h1, route, p[i].reshape(t, -1), ys, w_ple_gate[i].astype(BF16),
                       w_ple_proj[i].astype(BF16), norm_ple[i][None, :], norm_final[None, :],
                       tm=blk.moe_rows, final_norm=(i == depth - 1))
        h = out.reshape(bsz, seq, d)
    return h
```

```python
import functools
from typing import NamedTuple

import jax
import jax.numpy as jnp
from jax import lax
from jax.experimental import pallas as pl
from jax.experimental.pallas import tpu as pltpu

F32 = jnp.float32
BF16 = jnp.bfloat16

EPS = 1e-6
POOL_WINDOWS = (2, 4, 8, 16)
POOL_HALO = 32
assert all(w == 2 ** (g + 1) for g, w in enumerate(POOL_WINDOWS)) and POOL_HALO == 8 * len(POOL_WINDOWS)
CONV_K = 4
CONV_HALO = 8
DN_HEADS = 8
DN_DK = 128
CHUNK = 64
N_GROUPS = 4
EXPERTS_PER_GROUP = 8
N_EXPERTS = N_GROUPS * EXPERTS_PER_GROUP
LANES = 128
VMEM_LIMIT = 56 * 1024 * 1024

COL_GATE_POOL = 0
COL_GATE_DN = 2048
COL_POOL = 4096
COL_Q = 5120
COL_K = 6144
COL_V = 7168
COL_Z = 8192
N_MAIN = 9216


def _cparams(sem):
    return pltpu.CompilerParams(dimension_semantics=sem, vmem_limit_bytes=VMEM_LIMIT)


def _resident(shape):
    nd = len(shape)
    return pl.BlockSpec(shape, lambda *_: (0,) * nd, pipeline_mode=pl.Buffered(1))


def _inproj_kernel(x_ref, gain_ref, w_ref, wba_ref, out_ref, ba_ref, n1_ref, *, sub):
    tm = x_ref.shape[0]

    @pl.when(pl.program_id(1) == 0)
    def _():
        def body(r, _):
            rows = pl.ds(pl.multiple_of(r * sub, sub), sub)
            x = x_ref[rows, :]
            y = x * lax.rsqrt(jnp.mean(x * x, axis=-1, keepdims=True) + EPS) * gain_ref[...]
            n1_ref[rows, :] = y.astype(BF16)
            return 0
        lax.fori_loop(0, tm // sub, body, 0)
        ba_ref[...] = jnp.dot(n1_ref[...], wba_ref[...], preferred_element_type=F32)

    out_ref[...] = jnp.dot(n1_ref[...], w_ref[...], preferred_element_type=F32).astype(out_ref.dtype)


def _inproj(x2d, gain, w_main, w_ba, *, tm, tn):
    t, d = x2d.shape
    n = w_main.shape[1]
    return pl.pallas_call(
        functools.partial(_inproj_kernel, sub=min(tm, 256)),
        grid=(t // tm, n // tn),
        in_specs=[
            pl.BlockSpec((tm, d), lambda i, j: (i, 0)),
            _resident((1, d)),
            pl.BlockSpec((d, tn), lambda i, j: (0, j)),
            _resident((d, LANES)),
        ],
        out_specs=[
            pl.BlockSpec((tm, tn), lambda i, j: (i, j)),
            pl.BlockSpec((tm, LANES), lambda i, j: (i, 0)),
        ],
        out_shape=[jax.ShapeDtypeStruct((t, n), BF16), jax.ShapeDtypeStruct((t, LANES), F32)],
        scratch_shapes=[pltpu.VMEM((tm, d), BF16)],
        compiler_params=_cparams(("parallel", "arbitrary")),
        name="inproj",
    )(x2d, gain, w_main, w_ba)


def _pool_kernel(u_ref, gp_ref, pw_ref, scale_ref, wup_ref, out_ref, ext_ref, sa_ref, sb_ref, y_ref):
    ts = u_ref.shape[0]
    gw = pw_ref.shape[1]
    d_pool = ext_ref.shape[1]
    n = POOL_HALO + ts
    s = pl.program_id(1)

    @pl.when(s == 0)
    def _():
        ext_ref[0:POOL_HALO, :] = jnp.zeros((POOL_HALO, d_pool), F32)

    @pl.when(s > 0)
    def _():
        ext_ref[0:POOL_HALO, :] = ext_ref[ts:n, :]

    ext_ref[POOL_HALO:n, :] = u_ref[...].astype(F32)
    t1 = (s * ts + 1 + lax.broadcasted_iota(jnp.int32, (ts, 1), 0)).astype(F32)

    src = ext_ref
    for g, w in enumerate(POOL_WINDOWS):
        half = w // 2
        first = 8 * (g + 1)
        c0 = g * gw
        sw = src[first:n, c0:d_pool] + src[first - half:n - half, c0:d_pool]
        cols = slice(c0, c0 + gw)
        cur = ext_ref[POOL_HALO:n, cols]
        d = sw[POOL_HALO - first:, 0:gw] / jnp.minimum(t1, float(w)) - cur
        y = jnp.dot(d.astype(BF16), pw_ref[g], preferred_element_type=F32) * scale_ref[:, cols]
        y_ref[:, cols] = y.astype(BF16)
        if g + 1 < len(POOL_WINDOWS):
            dst = sa_ref if g % 2 == 0 else sb_ref
            dst[first:n, c0 + gw:d_pool] = sw[:, gw:]
            src = dst
    up = jnp.dot(y_ref[...], wup_ref[...], preferred_element_type=F32)
    out_ref[...] = (jax.nn.sigmoid(gp_ref[...].astype(F32)) * up).astype(out_ref.dtype)


def _pool(proj, pool_w, pool_scale, w_up_pool, *, bsz, seq, ts):
    t = proj.shape[0]
    ns = seq // ts
    d_pool = w_up_pool.shape[0]
    d = w_up_pool.shape[1]
    return pl.pallas_call(
        _pool_kernel,
        grid=(bsz, ns),
        in_specs=[
            pl.BlockSpec((ts, d_pool), lambda b, s: (b * ns + s, COL_POOL // d_pool)),
            pl.BlockSpec((ts, d), lambda b, s: (b * ns + s, COL_GATE_POOL // d)),
            _resident(pool_w.shape),
            _resident((1, d_pool)),
            _resident(w_up_pool.shape),
        ],
        out_specs=pl.BlockSpec((ts, d), lambda b, s: (b * ns + s, 0)),
        out_shape=jax.ShapeDtypeStruct((t, d), BF16),
        scratch_shapes=[pltpu.VMEM((POOL_HALO + ts, d_pool), F32)] * 3 + [pltpu.VMEM((ts, d_pool), BF16)],
        compiler_params=_cparams(("parallel", "arbitrary")),
        name="pool",
    )(proj, proj, pool_w, pool_scale, w_up_pool)


def _dot_nt(a, b):
    return lax.dot_general(a, b, (((1,), (1,)), ((), ())), preferred_element_type=F32)


def _dot_tn(a, b):
    return lax.dot_general(a, b, (((0,), (0,)), ((), ())), preferred_element_type=F32)


GROUP = 256
PAIR = 2 * CHUNK
WY_HEADS = 4


def _wy_kernel(q_ref, k_ref, v_ref, ba_ref, cw_ref, alog_ref, dtb_ref,
               u_ref, w_ref, qg_ref, kd_ref, at_ref, egl_ref, ext, carry_ref, gate_ref, qkv_ref, *, ns):
    ts, hw = q_ref.shape
    hp = pl.program_id(1)
    s = pl.program_id(0) % ns

    @pl.when(s == 0)
    def _():
        ext[0:CONV_HALO, :] = jnp.zeros((CONV_HALO, 3 * hw), F32)

    @pl.when(s > 0)
    def _():
        ext[0:CONV_HALO, :] = carry_ref[hp]

    ext[CONV_HALO:CONV_HALO + ts, 0:hw] = q_ref[...].astype(F32)
    ext[CONV_HALO:CONV_HALO + ts, hw:2 * hw] = k_ref[...].astype(F32)
    ext[CONV_HALO:CONV_HALO + ts, 2 * hw:3 * hw] = v_ref[...].astype(F32)
    carry_ref[hp] = ext[ts:ts + CONV_HALO, :]

    def conv_silu(c0):
        cols = slice(c0, c0 + DN_DK)
        first = CONV_HALO - (CONV_K - 1)
        y = ext[first:first + ts, cols] * cw_ref[0, 0:1, cols]
        for tap in range(1, CONV_K):
            y = y + ext[first + tap:first + tap + ts, cols] * cw_ref[0, tap:tap + 1, cols]
        return y * jax.nn.sigmoid(y)

    @pl.when(hp == 0)
    def _():
        ba = ba_ref[...]
        xs = ba + dtb_ref[...]
        softplus = jnp.maximum(xs, 0.0) + jnp.log1p(jnp.exp(-jnp.abs(xs)))
        g_all = -jnp.exp(alog_ref[...]) * softplus
        ri = lax.broadcasted_iota(jnp.int32, (ts, ts), 0)
        ci = lax.broadcasted_iota(jnp.int32, (ts, ts), 1)
        tri = jnp.where((ri // CHUNK == ci // CHUNK) & (ci <= ri), 1.0, 0.0).astype(F32)
        gate_ref[0] = jax.nn.sigmoid(ba)
        gate_ref[1] = jnp.dot(tri, g_all, preferred_element_type=F32, precision=lax.Precision.HIGHEST)

    beta_all = gate_ref[0]
    gcum = gate_ref[1]
    lane = lax.broadcasted_iota(jnp.int32, (ts, LANES), 1)
    pick = lambda arr, ln: jnp.sum(jnp.where(lane == ln, arr, 0.0), axis=-1, keepdims=True)
    row8 = lax.broadcasted_iota(jnp.int32, (8, LANES), 0)
    lane8 = lax.broadcasted_iota(jnp.int32, (8, LANES), 1)
    pick_rows = jnp.where(lane8 == DN_HEADS + hp * WY_HEADS + row8, 1.0, 0.0).astype(F32)
    gr_all = lax.dot_general(pick_rows, gcum, (((1,), (1,)), ((), ())), preferred_element_type=F32,
                             precision=lax.Precision.HIGHEST)

    betas, gcs = [], []
    for hh in range(WY_HEADS):
        hc = slice(hh * DN_DK, (hh + 1) * DN_DK)
        qh = conv_silu(hh * DN_DK)
        qkv_ref[0, :, hc] = qh * (lax.rsqrt(jnp.sum(qh * qh, axis=-1, keepdims=True) + EPS) * (DN_DK ** -0.5))
        kh = conv_silu(hw + hh * DN_DK)
        qkv_ref[1, :, hc] = kh * lax.rsqrt(jnp.sum(kh * kh, axis=-1, keepdims=True) + EPS)
        qkv_ref[2, :, hc] = conv_silu(2 * hw + hh * DN_DK)
        head = hp * WY_HEADS + hh
        betas.append(jnp.broadcast_to(pick(beta_all, head), (ts, DN_DK)))
        gcs.append(jnp.broadcast_to(pick(gcum, DN_HEADS + head), (ts, DN_DK)))

    ii = lax.broadcasted_iota(jnp.int32, (GROUP, GROUP), 0)
    jj = lax.broadcasted_iota(jnp.int32, (GROUP, GROUP), 1)
    same = (ii // CHUNK) == (jj // CHUNK)
    incl = same & (ii >= jj)
    strict = same & (ii > jj)

    items = [(slice(g * GROUP, (g + 1) * GROUP), hh) for hh in range(WY_HEADS)
             for g in range(ts // GROUP)]
    lows, sols = [], []
    for rows, hh in items:
        hc = slice(hh * DN_DK, (hh + 1) * DN_DK)
        q = qkv_ref[0, rows, hc]
        k = qkv_ref[1, rows, hc]
        beta = betas[hh][rows, :]
        gc = gcs[hh][rows, :]
        gr = gr_all[hh:hh + 1, rows]
        decay = jnp.where(incl, jnp.exp(jnp.concatenate([gc, gc], axis=1) - gr), 0.0)
        kb = k * beta
        a = _dot_nt(jnp.concatenate([kb, q], axis=0).astype(BF16), k.astype(BF16))
        lows.append(jnp.where(strict, a[:GROUP] * decay, 0.0).astype(BF16))
        attn = a[GROUP:] * decay
        for j in range(GROUP // PAIR):
            pr = slice(j * PAIR, (j + 1) * PAIR)
            at_ref[rows.start + j * PAIR:rows.start + (j + 1) * PAIR, hc] = attn[pr, pr].astype(BF16)
        egc = jnp.exp(gc)
        qg_ref[rows, hc] = (q * egc).astype(BF16)
        lasts = [gc[(c + 1) * CHUNK - 1:(c + 1) * CHUNK, :] for c in range(GROUP // CHUNK)]
        for c, g_last in enumerate(lasts):
            row = rows.start // CHUNK + c
            egl_ref[row:row + 1, hc] = jnp.exp(g_last)
        g_last_rows = jnp.concatenate([jnp.broadcast_to(gl, (CHUNK, DN_DK)) for gl in lasts], axis=0)
        kd_ref[rows, hc] = (k * jnp.exp(g_last_rows - gc)).astype(BF16)
        sols.append(jnp.concatenate([qkv_ref[2, rows, hc] * beta, kb * egc], axis=1))

    mm = lambda a16, b: jnp.dot(a16, b.astype(BF16), preferred_element_type=F32)
    sols = [s - mm(l, s) for l, s in zip(lows, sols)]
    pws = lows
    for _ in range(CHUNK.bit_length() - 2):
        pws = [mm(p, p).astype(BF16) for p in pws]
        sols = [s + mm(p, s) for p, s in zip(pws, sols)]
    for (rows, hh), s in zip(items, sols):
        hc = slice(hh * DN_DK, (hh + 1) * DN_DK)
        u_ref[rows, hc] = s[:, :DN_DK]
        w_ref[rows, hc] = s[:, DN_DK:].astype(BF16)


def _wy(proj, ba, conv_w, alog_pad, dtb_pad, *, seq, ts):
    t = proj.shape[0]
    hw = WY_HEADS * DN_DK
    n_hp = DN_HEADS // WY_HEADS
    dq = DN_HEADS * DN_DK
    cw = conv_w.reshape(CONV_K, 3, n_hp, hw).transpose(2, 0, 1, 3).reshape(n_hp, CONV_K, 3 * hw)
    src = lambda col: pl.BlockSpec((ts, hw), lambda r, h: (r, col // hw + h))
    hblk = pl.BlockSpec((ts, hw), lambda r, h: (r, h))
    wide = jax.ShapeDtypeStruct((t, dq), BF16)
    return pl.pallas_call(
        functools.partial(_wy_kernel, ns=seq // ts),
        grid=(t // ts, n_hp),
        in_specs=[src(COL_Q), src(COL_K), src(COL_V),
                  pl.BlockSpec((ts, LANES), lambda r, h: (r, 0)),
                  pl.BlockSpec((1, CONV_K, 3 * hw), lambda r, h: (h, 0, 0)),
                  _resident((1, LANES)), _resident((1, LANES))],
        out_specs=[hblk, hblk, hblk, hblk, hblk,
                   pl.BlockSpec((ts // CHUNK, hw), lambda r, h: (r, h))],
        out_shape=[jax.ShapeDtypeStruct((t, dq), F32), wide, wide, wide, wide,
                   jax.ShapeDtypeStruct((t // CHUNK, dq), F32)],
        scratch_shapes=[pltpu.VMEM((CONV_HALO + ts, 3 * hw), F32),
                        pltpu.VMEM((n_hp, CONV_HALO, 3 * hw), F32),
                        pltpu.VMEM((2, ts, LANES), F32),
                        pltpu.VMEM((3, ts, hw), F32)],
        compiler_params=_cparams(("arbitrary", "arbitrary")),
        name="wy",
    )(proj, proj, proj, ba, cw, alog_pad, dtb_pad)


def _scan_kernel(u_ref, w_ref, qg_ref, kd_ref, at_ref, egl_ref, z_ref, dnw_ref, out_ref, state_ref):
    ts = u_ref.shape[0]

    @pl.when(pl.program_id(1) == 0)
    def _():
        state_ref[...] = jnp.zeros(state_ref.shape, F32)

    heads = range(DN_HEADS)
    hcs = [slice(h * DN_DK, (h + 1) * DN_DK) for h in heads]
    states = [state_ref[h] for h in heads]
    even_vn = [None] * DN_HEADS
    for c in range(ts // CHUNK):
        rows = slice(c * CHUNK, (c + 1) * CHUNK)
        rs = [jnp.dot(jnp.concatenate([w_ref[rows, hcs[h]], qg_ref[rows, hcs[h]]], axis=0),
                      states[h].astype(BF16), preferred_element_type=F32) for h in heads]
        vns = [(u_ref[rows, hcs[h]] - rs[h][:CHUNK]).astype(BF16) for h in heads]
        states = [states[h] * egl_ref[c:c + 1, hcs[h]] + _dot_tn(kd_ref[rows, hcs[h]], vns[h]) for h in heads]
        for h in heads:
            if c % 2 == 0:
                even_vn[h] = vns[h]
                pair = jnp.concatenate([vns[h], jnp.zeros_like(vns[h])], axis=0)
            else:
                pair = jnp.concatenate([even_vn[h], vns[h]], axis=0)
            o = rs[h][CHUNK:] + jnp.dot(at_ref[rows, hcs[h]], pair, preferred_element_type=F32)
            on = o * lax.rsqrt(jnp.mean(o * o, axis=-1, keepdims=True) + EPS) * dnw_ref[...]
            z = z_ref[rows, hcs[h]].astype(F32)
            out_ref[rows, hcs[h]] = (on * (z * jax.nn.sigmoid(z))).astype(out_ref.dtype)
    for h in heads:
        state_ref[h] = states[h]


def _scan(u, w, qg, kd, at, egl, proj, dn_norm, *, bsz, seq, ts):
    t, dv = u.shape
    ns = seq // ts
    blk = pl.BlockSpec((ts, dv), lambda b, s: (b * ns + s, 0))
    return pl.pallas_call(
        _scan_kernel,
        grid=(bsz, ns),
        in_specs=[blk, blk, blk, blk, blk,
                  pl.BlockSpec((ts // CHUNK, dv), lambda b, s: (b * ns + s, 0)),
                  pl.BlockSpec((ts, dv), lambda b, s: (b * ns + s, COL_Z // dv)),
                  _resident((1, DN_DK))],
        out_specs=blk,
        out_shape=jax.ShapeDtypeStruct((t, dv), BF16),
        scratch_shapes=[pltpu.VMEM((DN_HEADS, DN_DK, DN_DK), F32)],
        compiler_params=_cparams(("parallel", "arbitrary")),
        name="scan",
    )(u, w, qg, kd, at, egl, proj, dn_norm)


ROUTE_E1, ROUTE_E2, ROUTE_W1, ROUTE_W2, ROUTE_R1, ROUTE_R2 = range(6)
ROUTER_LANE0 = N_GROUPS
NEG_BIG = -1e30


def _pack_rows(x):
    n = x.shape[1] // 2
    bits = lambda v: lax.bitcast_convert_type(v.astype(BF16).astype(F32), jnp.uint32)
    return bits(x[:, n:]) | (bits(x[:, :n]) >> 16)


def _unpack_rows(p):
    lo = lax.bitcast_convert_type(p << 16, F32).astype(BF16)
    hi = lax.bitcast_convert_type(p & jnp.uint32(0xFFFF0000), F32).astype(BF16)
    return jnp.concatenate([lo, hi], axis=1)


def _merge_kernel(ydn_ref, gd_ref, mp_ref, x_ref, wupdn_ref, wout_ref, gain_ref, wr_ref, br_ref,
                  h1_ref, n2_ref, route_ref, route_t_ref, cnt_ref, carry_ref):
    tm = x_ref.shape[0]

    @pl.when(pl.program_id(0) == 0)
    def _():
        carry_ref[...] = jnp.zeros(carry_ref.shape, F32)

    md = (jax.nn.sigmoid(gd_ref[...].astype(F32))
          * jnp.dot(ydn_ref[...], wupdn_ref[...], preferred_element_type=F32))
    merged = mp_ref[...].astype(F32) + md
    h1 = x_ref[...] + jnp.dot(merged.astype(BF16), wout_ref[...], preferred_element_type=F32)
    h1_ref[...] = h1
    n2 = h1 * lax.rsqrt(jnp.mean(h1 * h1, axis=-1, keepdims=True) + EPS) * gain_ref[...]
    n2_ref[...] = _pack_rows(n2)

    hi = n2.astype(BF16)
    lo = (n2 - hi.astype(F32)).astype(BF16)
    r_hi = jnp.dot(hi, wr_ref[...], preferred_element_type=F32)
    r_lo = jnp.dot(lo, wr_ref[:, :LANES], preferred_element_type=F32)
    logits = r_hi[:, :LANES] + r_hi[:, LANES:] + r_lo + br_ref[...]

    lane = lax.broadcasted_iota(jnp.int32, (tm, LANES), 1)
    lane_f = lane.astype(F32)
    first_max = lambda v, m: jnp.min(jnp.where(v == m, lane_f, float(LANES)), axis=-1, keepdims=True)

    is_grp = lane < N_GROUPS
    lg = jnp.where(is_grp, logits, NEG_BIG)
    gmax = jnp.max(lg, axis=-1, keepdims=True)
    gi = first_max(lg, gmax)
    p_sel = 1.0 / jnp.sum(jnp.where(is_grp, jnp.exp(lg - gmax), 0.0), axis=-1, keepdims=True)

    e_lane = lane - ROUTER_LANE0
    in_grp = (e_lane >= 0) & (e_lane < N_EXPERTS) & ((e_lane // EXPERTS_PER_GROUP).astype(F32) == gi)
    le = jnp.where(in_grp, logits, NEG_BIG)
    v1 = jnp.max(le, axis=-1, keepdims=True)
    i1 = first_max(le, v1)
    le2 = jnp.where(lane_f == i1, NEG_BIG, le)
    v2 = jnp.max(le2, axis=-1, keepdims=True)
    i2 = first_max(le2, v2)
    s = jnp.exp(v2 - v1)
    w1 = p_sel / (1.0 + s)
    w2 = p_sel * s / (1.0 + s)
    e1 = i1 - float(ROUTER_LANE0)
    e2 = i2 - float(ROUTER_LANE0)

    oh1 = jnp.where(lane_f == e1, 1.0, 0.0)
    oh2 = jnp.where(lane_f == e2, 1.0, 0.0)
    both = oh1 + oh2
    ri = lax.broadcasted_iota(jnp.int32, (tm, tm), 0)
    ci = lax.broadcasted_iota(jnp.int32, (tm, tm), 1)
    tri = jnp.where(ri > ci, 1.0, 0.0).astype(BF16)
    prior = jnp.dot(tri, both.astype(BF16), preferred_element_type=F32) + carry_ref[...]
    r1 = jnp.sum(prior * oh1, axis=-1, keepdims=True)
    r2 = jnp.sum(prior * oh2, axis=-1, keepdims=True)
    carry = carry_ref[...] + jnp.sum(both, axis=0, keepdims=True)
    carry_ref[...] = carry
    cnt_ref[...] = carry

    route = jnp.zeros((tm, LANES), F32)
    for ln, val in ((ROUTE_E1, e1), (ROUTE_E2, e2), (ROUTE_W1, w1), (ROUTE_W2, w2),
                    (ROUTE_R1, r1), (ROUTE_R2, r2)):
        route = jnp.where(lane == ln, val, route)
    route_ref[...] = route
    route_t_ref[...] = route.T[0:8, :]


def _merge(y_dn, proj, mp, x2d, w_up_dn, w_out, gain, w_router, b_router, *, tm):
    t, d = x2d.shape
    dv = y_dn.shape[1]
    rows = lambda blk: pl.BlockSpec(blk, lambda i: (i, 0))
    return pl.pallas_call(
        _merge_kernel,
        grid=(t // tm,),
        in_specs=[
            rows((tm, dv)),
            pl.BlockSpec((tm, d), lambda i: (i, COL_GATE_DN // d)),
            rows((tm, d)), rows((tm, d)),
            _resident(w_up_dn.shape), _resident(w_out.shape), _resident((1, d)),
            _resident(w_router.shape), _resident((1, LANES)),
        ],
        out_specs=[rows((tm, d)), rows((tm, d // 2)), rows((tm, LANES)),
                   pl.BlockSpec((8, tm), lambda i: (0, i)),
                   pl.BlockSpec((1, LANES), lambda i: (0, 0))],
        out_shape=[jax.ShapeDtypeStruct((t, d), F32), jax.ShapeDtypeStruct((t, d // 2), jnp.uint32),
                   jax.ShapeDtypeStruct((t, LANES), F32), jax.ShapeDtypeStruct((8, t), F32),
                   jax.ShapeDtypeStruct((1, LANES), F32)],
        scratch_shapes=[pltpu.VMEM((1, LANES), F32)],
        compiler_params=_cparams(("arbitrary",)),
        name="merge",
    )(y_dn, proj, mp, x2d, w_up_dn, w_out, gain, w_router, b_router)


def _dispatch_kernel(pos_ref, zrow_ref, nu_ref, n2_ref, xs_ref, zbuf, sem, zsem, *, n_blocks, n_tok):
    tb = n2_ref.shape[0]
    bm = zbuf.shape[0]
    i = pl.program_id(0)

    @pl.when(i == 0)
    def _():
        zbuf[...] = jnp.zeros(zbuf.shape, zbuf.dtype)
        zero_block = lambda row: pltpu.make_async_copy(
            zbuf, xs_ref.at[pl.ds(pl.multiple_of(row, bm), bm)], zsem)
        for start in (True, False):
            for e in range(N_EXPERTS):
                @pl.when(zrow_ref[e] >= 0)
                def _(e=e, start=start):
                    cp = zero_block(jnp.maximum(zrow_ref[e], 0))
                    cp.start() if start else cp.wait()

            def tail(b, _, start=start):
                cp = zero_block(b * bm)
                cp.start() if start else cp.wait()
                return 0
            lax.fori_loop(nu_ref[0], n_blocks, tail, 0)

    for j in range(tb):
        for k in range(2):
            dst = pos_ref[k * n_tok + i * tb + j]
            pltpu.make_async_copy(n2_ref.at[pl.ds(j, 1)], xs_ref.at[pl.ds(dst, 1)],
                                  sem).start(priority=k)
    for _ in range(2):
        pltpu.make_async_copy(n2_ref, xs_ref.at[pl.ds(0, tb)], sem).wait()


def _dispatch(pos, zrow, n_used, n2, *, cap, bm, tb):
    t, dp = n2.shape
    return pl.pallas_call(
        functools.partial(_dispatch_kernel, n_blocks=cap // bm, n_tok=t),
        grid_spec=pltpu.PrefetchScalarGridSpec(
            num_scalar_prefetch=3,
            grid=(t // tb,),
            in_specs=[pl.BlockSpec((tb, dp), lambda i, pos, zrow, nu: (i, 0))],
            out_specs=pl.BlockSpec(memory_space=pl.ANY),
            scratch_shapes=[pltpu.VMEM((bm, dp), n2.dtype), pltpu.SemaphoreType.DMA,
                            pltpu.SemaphoreType.DMA],
        ),
        out_shape=jax.ShapeDtypeStruct((cap, dp), n2.dtype),
        compiler_params=_cparams(("arbitrary",)),
        name="dispatch",
    )(pos, zrow, n_used, n2)


def _experts_kernel(be_ref, nu_ref, x_ref, wg_ref, wu_ref, wd_ref, y_ref, wg16, wu16, wd16):
    i = pl.program_id(0)
    nu = nu_ref[0]

    @pl.when((i < nu) & ((i == 0) | (be_ref[i] != be_ref[jnp.maximum(i - 1, 0)])))
    def _():
        wg16[...] = wg_ref[0].astype(BF16)
        wu16[...] = wu_ref[0].astype(BF16)
        wd16[...] = wd_ref[0].astype(BF16)

    @pl.when(i < nu)
    def _():
        xb = _unpack_rows(x_ref[...])
        g = jnp.dot(xb, wg16[...], preferred_element_type=F32)
        u = jnp.dot(xb, wu16[...], preferred_element_type=F32)
        hmid = (g * jax.nn.sigmoid(g)) * u
        y_ref[...] = _pack_rows(jnp.dot(hmid.astype(BF16), wd16[...], preferred_element_type=F32))

    @pl.when(i >= nu)
    def _():
        y_ref[...] = jnp.zeros(y_ref.shape, y_ref.dtype)


def _experts(blk_e, n_used, xs, w_gate, w_up, w_down, *, bm):
    cap, dp = xs.shape
    d, de = w_gate.shape[1], w_gate.shape[2]
    wspec = lambda shape: pl.BlockSpec(shape, lambda i, be, nu: (be[i], 0, 0))
    return pl.pallas_call(
        _experts_kernel,
        grid_spec=pltpu.PrefetchScalarGridSpec(
            num_scalar_prefetch=2,
            grid=(cap // bm,),
            in_specs=[pl.BlockSpec((bm, dp), lambda i, be, nu: (jnp.minimum(i, nu[0] - 1), 0)),
                      wspec((1, d, de)), wspec((1, d, de)), wspec((1, de, d))],
            out_specs=pl.BlockSpec((bm, dp), lambda i, be, nu: (i, 0)),
            scratch_shapes=[pltpu.VMEM((d, de), BF16), pltpu.VMEM((d, de), BF16), pltpu.VMEM((de, d), BF16)],
        ),
        out_shape=jax.ShapeDtypeStruct((cap, dp), jnp.uint32),
        compiler_params=_cparams(("arbitrary",)),
        name="experts",
    )(blk_e, n_used, xs, w_gate, w_up, w_down)


N_GATHER_BUF = 3


def _combine_kernel(pos_ref, h1_ref, route_ref, p_ref, ys_ref, wpg_ref, wpp_ref, gple_ref, gfin_ref,
                    out_ref, ybuf0, ybuf1, ybuf2, sem, *, final_norm, n_steps):
    bufs = (ybuf0, ybuf1, ybuf2)
    tm = h1_ref.shape[0]
    i = pl.program_id(0)

    def issue(blk, slot):
        for j in range(tm):
            for k in range(2):
                src = pos_ref[k * (n_steps * tm) + blk * tm + j]
                pltpu.make_async_copy(ys_ref.at[pl.ds(src, 1)], bufs[slot].at[pl.ds(k * tm + j, 1)],
                                      sem.at[slot]).start(priority=k)

    def drain(slot):
        pltpu.make_async_copy(ys_ref.at[pl.ds(0, 2 * tm)], bufs[slot], sem.at[slot]).wait()

    @pl.when(i == 0)
    def _():
        issue(0, 0)
        issue(1, 1)

    for r in range(N_GATHER_BUF):
        @pl.when(i % N_GATHER_BUF == r)
        def _(r=r):
            drain(r)
            issue(i + 2, (r + 2) % N_GATHER_BUF)
            cur = bufs[r]
            route = route_ref[...]
            y_moe = (_unpack_rows(cur[0:tm, :]).astype(F32) * route[:, ROUTE_W1:ROUTE_W1 + 1]
                     + _unpack_rows(cur[tm:2 * tm, :]).astype(F32) * route[:, ROUTE_W2:ROUTE_W2 + 1])
            h2 = h1_ref[...] + y_moe
            n3 = h2 * lax.rsqrt(jnp.mean(h2 * h2, axis=-1, keepdims=True) + EPS) * gple_ref[...]
            gate = jax.nn.sigmoid(jnp.dot(n3.astype(BF16), wpg_ref[...], preferred_element_type=F32))
            h3 = h2 + gate * jnp.dot(p_ref[...].astype(BF16), wpp_ref[...], preferred_element_type=F32)
            if final_norm:
                h3 = h3 * lax.rsqrt(jnp.mean(h3 * h3, axis=-1, keepdims=True) + EPS) * gfin_ref[...]
            out_ref[...] = h3

    last = n_steps - 1

    @pl.when(i == last)
    def _():
        drain((last + 1) % N_GATHER_BUF)
        drain((last + 2) % N_GATHER_BUF)


def _combine(pos, h1, route, p2d, ys, w_ple_gate, w_ple_proj, g_ple, g_fin, *, tm, final_norm):
    t, d = h1.shape
    rows = lambda w: pl.BlockSpec((tm, w), lambda i, pos: (i, 0))
    res = lambda shape: pl.BlockSpec(shape, lambda i, pos: (0,) * len(shape), pipeline_mode=pl.Buffered(1))
    return pl.pallas_call(
        functools.partial(_combine_kernel, final_norm=final_norm, n_steps=t // tm),
        grid_spec=pltpu.PrefetchScalarGridSpec(
            num_scalar_prefetch=1,
            grid=(t // tm,),
            in_specs=[rows(d), rows(LANES), rows(p2d.shape[1]), pl.BlockSpec(memory_space=pl.ANY),
                      res(w_ple_gate.shape), res(w_ple_proj.shape), res((1, d)), res((1, d))],
            out_specs=rows(d),
            scratch_shapes=[pltpu.VMEM((2 * tm, ys.shape[1]), ys.dtype)] * N_GATHER_BUF
                           + [pltpu.SemaphoreType.DMA((N_GATHER_BUF,))],
        ),
        out_shape=jax.ShapeDtypeStruct((t, d), F32),
        compiler_params=_cparams(("arbitrary",)),
        name="combine",
    )(jnp.pad(pos, (0, 2 * tm)), h1, route, p2d, ys, w_ple_gate, w_ple_proj, g_ple, g_fin)


def _pick(n, pref):
    b = min(n, pref)
    while n % b:
        b -= 8
    return b


MOE_BM = 512


class _Blocks(NamedTuple):
    inproj_rows: int
    inproj_cols: int
    seq_rows: int
    merge_rows: int
    dispatch_rows: int
    moe_rows: int


def _blocks(t, seq):
    seq_rows = _pick(seq, 512)
    assert seq_rows % GROUP == 0 and (seq_rows // CHUNK) % 8 == 0, "time block must hold whole solve groups"
    return _Blocks(inproj_rows=_pick(t, 1024), inproj_cols=N_MAIN // 4, seq_rows=seq_rows,
                   merge_rows=_pick(t, 512), dispatch_rows=_pick(t, 512), moe_rows=_pick(t, 256))


def _stage_mixers(x, prm, i, blk):
    bsz, seq, d = x.shape
    t = bsz * seq
    x2d = x.reshape(t, d)
    w_in = prm["w_in"][i]
    d_pool = prm["w_up_pool"].shape[1]
    dq = DN_HEADS * DN_DK
    o_qkv = d_pool
    o_z = o_qkv + 3 * dq
    o_b = o_z + dq
    o_gp = o_b + 2 * DN_HEADS
    o_gd = o_gp + d
    w_main = jnp.concatenate(
        [w_in[:, o_gp:o_gp + d], w_in[:, o_gd:o_gd + d], w_in[:, :d_pool],
         w_in[:, o_qkv:o_qkv + 3 * dq], w_in[:, o_z:o_z + dq]], axis=1).astype(BF16)
    assert w_main.shape[1] == N_MAIN
    w_ba = jnp.pad(w_in[:, o_b:o_b + 2 * DN_HEADS], ((0, 0), (0, LANES - 2 * DN_HEADS))).astype(BF16)
    proj, ba = _inproj(x2d, prm["norm_mix"][i][None, :], w_main, w_ba,
                       tm=blk.inproj_rows, tn=blk.inproj_cols)

    ts = blk.seq_rows
    mp = _pool(proj, prm["pool_w"][i].astype(BF16), prm["pool_scale"][i][None, :],
               prm["w_up_pool"][i].astype(BF16), bsz=bsz, seq=seq, ts=ts)

    lane_pad = lambda a: jnp.pad(a, (DN_HEADS, LANES - 2 * DN_HEADS))[None, :]
    u, w, qg, kd, at, egl = _wy(proj, ba, prm["conv_w"][i], lane_pad(prm["a_log"][i]),
                                lane_pad(prm["dt_bias"][i]), seq=seq, ts=ts)
    y_dn = _scan(u, w, qg, kd, at, egl, proj, prm["dn_norm"][i][None, :], bsz=bsz, seq=seq, ts=ts)
    return dict(proj=proj, ba=ba, mp=mp, y_dn=y_dn)


def _router_weights(w_rg, b_rg, w_re, b_re):
    d = w_rg.shape[0]
    w = jnp.concatenate([w_rg, w_re, jnp.zeros((d, LANES - N_GROUPS - N_EXPERTS), F32)], axis=1)
    w_hi = w.astype(BF16)
    w_lo = (w - w_hi.astype(F32)).astype(BF16)
    b = jnp.concatenate([b_rg, b_re, jnp.zeros((LANES - N_GROUPS - N_EXPERTS,), F32)])[None, :]
    return jnp.concatenate([w_hi, w_lo], axis=1), b


def _stage_moe_plan(route_t, cnt, t):
    counts = cnt[0, :N_EXPERTS].astype(jnp.int32)
    padded = ((counts + MOE_BM - 1) // MOE_BM) * MOE_BM
    pends = jnp.cumsum(padded)
    pstarts = pends - padded
    eid = route_t[ROUTE_E1:ROUTE_E2 + 1].astype(jnp.int32)
    rank = route_t[ROUTE_R1:ROUTE_R2 + 1].astype(jnp.int32)
    first = jnp.sum(jnp.where(eid[..., None] == jnp.arange(N_EXPERTS), pstarts, 0), axis=-1)
    pos = (first + rank).reshape(2 * t)
    cap = 2 * t + N_EXPERTS * MOE_BM
    n_blocks = cap // MOE_BM
    n_used = (pends[-1] // MOE_BM).astype(jnp.int32)
    blk = jnp.minimum(jnp.arange(n_blocks, dtype=jnp.int32), n_used - 1)
    blk_e = jnp.sum((pends[None, :] <= (blk * MOE_BM)[:, None]).astype(jnp.int32), axis=1)
    blk_e = jnp.minimum(blk_e, N_EXPERTS - 1)
    zrow = jnp.where(padded > 0, pends - MOE_BM, -1).astype(jnp.int32)
    return pos, blk_e, n_used.reshape(1), zrow, cap


def kernel(x, p, norm_mix, w_in, pool_w, pool_scale, conv_w, a_log, dt_bias, dn_norm, w_up_pool, w_up_dn, w_out, norm_moe, w_router_group, b_router_group, w_router_expert, b_router_expert, w_gate, w_up, w_down, norm_ple, w_ple_gate, w_ple_proj, norm_final):
    bsz, seq, d = x.shape
    t = bsz * seq
    depth = norm_mix.shape[0]
    prm = dict(norm_mix=norm_mix, w_in=w_in, pool_w=pool_w, pool_scale=pool_scale, conv_w=conv_w,
               a_log=a_log, dt_bias=dt_bias, dn_norm=dn_norm, w_up_pool=w_up_pool)
    blk = _blocks(t, seq)
    h = x
    for i in range(depth):
        st = _stage_mixers(h, prm, i, blk)
        w_router, b_router = _router_weights(w_router_group[i], b_router_group[i],
                                             w_router_expert[i], b_router_expert[i])
        h1, n2, route, route_t, cnt = _merge(st["y_dn"], st["proj"], st["mp"], h.reshape(t, d),
                                             w_up_dn[i].astype(BF16), w_out[i].astype(BF16),
                                             norm_moe[i][None, :], w_router, b_router, tm=blk.merge_rows)
        pos, blk_e, n_used, zrow, cap = _stage_moe_plan(route_t, cnt, t)
        xs = _dispatch(pos, zrow, n_used, n2, cap=cap, bm=MOE_BM, tb=blk.dispatch_rows)
        ys = _experts(blk_e, n_used, xs, w_gate[i], w_up[i], w_down[i], bm=MOE_BM)
        out = _combine(pos, h1, route, p[i].reshape(t, -1), ys, w_ple_gate[i].astype(BF16),
                       w_ple_proj[i].astype(BF16), norm_ple[i][None, :], norm_final[None, :],
                       tm=blk.moe_rows, final_norm=(i == depth - 1))
        h = out.reshape(bsz, seq, d)
    return h
```
